```python
import math
import jax, jax.numpy as jnp
from jax import lax
import numpy as np

D_MODEL = 1024
BATCH = 4
SEQ = 4096
DEPTH = 1

POOL_WIDTH = D_MODEL // 2
POOL_GROUPS = 4
POOL_GROUP_DIM = POOL_WIDTH // POOL_GROUPS
POOL_WINDOWS = (2, 4, 8, 16)
ATTN_WIDTH = D_MODEL - POOL_WIDTH
N_DIFF_HEADS = 4
DIFF_HEAD_DIM = ATTN_WIDTH // (2 * N_DIFF_HEADS)
DIFF_V_DIM = 2 * DIFF_HEAD_DIM
IN_PROJ_WIDTH = POOL_WIDTH + 3 * ATTN_WIDTH
Q_BLOCK = 128
NUM_BUCKETS = 32
MAX_DISTANCE = 128
D_FF = 2816
CONV_WIDTH = 3
NORM_EPS = 1e-6
SUBLN_EPS = 1e-5
NEG_INF = -1e30

kernel_name = "hybrid_pool_diffattn_convffn"


def rms_norm(x, g, eps):
    xf = x.astype(jnp.float32)
    y = xf * lax.rsqrt(jnp.mean(xf * xf, axis=-1, keepdims=True) + eps)
    return (y * g.astype(jnp.float32)).astype(x.dtype)


def lambda_init_fn(layer_idx):
    return 0.8 - 0.6 * math.exp(-0.3 * layer_idx)


def t5_causal_bucket(q_pos, k_pos):
    n = jnp.maximum(q_pos[:, None] - k_pos[None, :], 0)
    max_exact = NUM_BUCKETS // 2
    nf = jnp.maximum(n, 1).astype(jnp.float32)
    large = max_exact + (jnp.log(nf / max_exact) / math.log(MAX_DISTANCE / max_exact)
                         * (NUM_BUCKETS - max_exact)).astype(jnp.int32)
    large = jnp.minimum(large, NUM_BUCKETS - 1)
    return jnp.where(n < max_exact, n, large)


def causal_window_mean(u, w):
    S = u.shape[1]
    cs = jnp.cumsum(u.astype(jnp.float32), axis=1)
    prev = jnp.pad(cs, ((0, 0), (w, 0), (0, 0)))[:, :S]
    cnt = jnp.minimum(jnp.arange(1, S + 1), w).astype(jnp.float32)[None, :, None]
    return ((cs - prev) / cnt).astype(u.dtype)


def pooling_mixer(zp, pool_w, pool_scale):
    B, S, _ = zp.shape
    zg = zp.reshape(B, S, POOL_GROUPS, POOL_GROUP_DIM)
    pooled = jnp.stack(
        [causal_window_mean(zg[:, :, gi], w) - zg[:, :, gi] for gi, w in enumerate(POOL_WINDOWS)],
        axis=2)
    y = jnp.einsum('bsgc,gcd->bsgd', pooled, pool_w.astype(zp.dtype)).reshape(B, S, POOL_WIDTH)
    return y * pool_scale.astype(zp.dtype)


def diff_attention(q, k, v, lam, rel_bias):
    B, S = q.shape[0], q.shape[1]
    nb = S // Q_BLOCK
    scale = DIFF_HEAD_DIM ** -0.5
    qb = q.reshape(B, nb, Q_BLOCK, N_DIFF_HEADS, 2, DIFF_HEAD_DIM).transpose(1, 0, 2, 3, 4, 5)
    k_pos = jnp.arange(S)

    def one_block(args):
        i, qi = args
        q_pos = i * Q_BLOCK + jnp.arange(Q_BLOCK)
        bias = jnp.transpose(rel_bias[t5_causal_bucket(q_pos, k_pos)], (2, 0, 1)).astype(jnp.float32)
        s = jnp.einsum('bqhcd,bkhcd->bhcqk', qi, k).astype(jnp.float32) * scale + bias[None, :, None]
        mask = k_pos[None, :] <= q_pos[:, None]
        s = jnp.where(mask, s, NEG_INF)
        p = jax.nn.softmax(s, axis=-1)
        a = (p[:, :, 0] - lam * p[:, :, 1]).astype(v.dtype)
        return jnp.einsum('bhqk,bkhe->bqhe', a, v)

    out = lax.map(one_block, (jnp.arange(nb), qb))
    return out.transpose(1, 0, 2, 3, 4).reshape(B, S, N_DIFF_HEADS, DIFF_V_DIM)


def causal_depthwise_conv(g, w, b):
    C = g.shape[-1]
    y = lax.conv_general_dilated(
        g, w.astype(g.dtype)[:, None, :], window_strides=(1,), padding=[(CONV_WIDTH - 1, 0)],
        dimension_numbers=('NWC', 'WIO', 'NWC'), feature_group_count=C)
    return y + b.astype(g.dtype)


def setup_inputs(seed: int = 0) -> dict:
    key = jax.random.key(seed)
    ks = jax.random.split(key, 20)
    f32 = jnp.float32
    nrm = lambda k, shape, s: jax.random.normal(k, shape, f32) * s
    return {
        "x": jax.random.normal(ks[0], (BATCH, SEQ, D_MODEL), f32),
        "norm_mix_g": 1.0 + nrm(ks[1], (DEPTH, D_MODEL), 0.02),
        "w_in": nrm(ks[2], (DEPTH, D_MODEL, IN_PROJ_WIDTH), D_MODEL ** -0.5),
        "pool_w": nrm(ks[3], (DEPTH, POOL_GROUPS, POOL_GROUP_DIM, POOL_GROUP_DIM), POOL_GROUP_DIM ** -0.5),
        "pool_scale": 1.0 + nrm(ks[4], (DEPTH, POOL_WIDTH), 0.1),
        "lambda_q1": nrm(ks[5], (DEPTH, DIFF_HEAD_DIM), 0.1),
        "lambda_k1": nrm(ks[6], (DEPTH, DIFF_HEAD_DIM), 0.1),
        "lambda_q2": nrm(ks[7], (DEPTH, DIFF_HEAD_DIM), 0.1),
        "lambda_k2": nrm(ks[8], (DEPTH, DIFF_HEAD_DIM), 0.1),
        "subln_g": 1.0 + nrm(ks[9], (DEPTH, DIFF_V_DIM), 0.02),
        "rel_bias": nrm(ks[10], (NUM_BUCKETS, N_DIFF_HEADS), 0.5),
        "w_out": nrm(ks[11], (DEPTH, D_MODEL, D_MODEL), D_MODEL ** -0.5),
        "norm_ffn_g": 1.0 + nrm(ks[12], (DEPTH, D_MODEL), 0.02),
        "ffn_w_in": nrm(ks[13], (DEPTH, D_MODEL, 2 * D_FF), D_MODEL ** -0.5),
        "ffn_conv_w": nrm(ks[14], (DEPTH, CONV_WIDTH, D_FF), CONV_WIDTH ** -0.5),
        "ffn_conv_b": nrm(ks[15], (DEPTH, D_FF), 0.01),
        "ffn_w_out": nrm(ks[16], (DEPTH, D_FF, D_MODEL), D_FF ** -0.5),
        "norm_final_g": 1.0 + nrm(ks[17], (D_MODEL,), 0.02),
    }


def reference(x, norm_mix_g, w_in, pool_w, pool_scale, lambda_q1, lambda_k1, lambda_q2,
              lambda_k2, subln_g, rel_bias, w_out, norm_ffn_g, ffn_w_in, ffn_conv_w,
              ffn_conv_b, ffn_w_out, norm_final_g):
    B, S, _ = x.shape
    f32 = jnp.float32
    for l in range(DEPTH):
        lam_init = lambda_init_fn(l)
        h = rms_norm(x, norm_mix_g[l], NORM_EPS)
        z = h @ w_in[l].astype(h.dtype)
        zp = z[..., :POOL_WIDTH]
        q = z[..., POOL_WIDTH:POOL_WIDTH + ATTN_WIDTH].reshape(B, S, N_DIFF_HEADS, 2, DIFF_HEAD_DIM)
        k = z[..., POOL_WIDTH + ATTN_WIDTH:POOL_WIDTH + 2 * ATTN_WIDTH].reshape(B, S, N_DIFF_HEADS, 2, DIFF_HEAD_DIM)
        v = z[..., POOL_WIDTH + 2 * ATTN_WIDTH:].reshape(B, S, N_DIFF_HEADS, DIFF_V_DIM)

        y_pool = pooling_mixer(zp, pool_w[l], pool_scale[l])

        lam = (jnp.exp(jnp.sum(lambda_q1[l].astype(f32) * lambda_k1[l].astype(f32)))
               - jnp.exp(jnp.sum(lambda_q2[l].astype(f32) * lambda_k2[l].astype(f32)))
               + lam_init)
        o = diff_attention(q, k, v, lam, rel_bias)
        o = rms_norm(o, subln_g[l], SUBLN_EPS) * (1.0 - lam_init)
        y_attn = o.reshape(B, S, ATTN_WIDTH)

        y = jnp.concatenate([y_pool, y_attn], axis=-1)
        x = x + y @ w_out[l].astype(y.dtype)

        h = rms_norm(x, norm_ffn_g[l], NORM_EPS)
        gu = h @ ffn_w_in[l].astype(h.dtype)
        g = causal_depthwise_conv(gu[..., :D_FF], ffn_conv_w[l], ffn_conv_b[l])
        u = gu[..., D_FF:]
        x = x + (jax.nn.silu(g) * u) @ ffn_w_out[l].astype(h.dtype)
    return rms_norm(x, norm_final_g, NORM_EPS)
```

```python
import functools
import math

import numpy as np
import jax
import jax.numpy as jnp
from jax import lax
from jax.experimental import pallas as pl
from jax.experimental.pallas import tpu as pltpu

D_MODEL = 1024
POOL_WIDTH = 512
POOL_GROUPS = 4
POOL_GROUP_DIM = 128
POOL_WINDOWS = (2, 4, 8, 16)
POOL_HALO = 16
ATTN_WIDTH = 512
N_HEADS = 4
HEAD_DIM = 64
V_DIM = 128
IN_PROJ_WIDTH = 2048
NUM_BUCKETS = 32
MAX_DISTANCE = 128
D_FF = 2816
NORM_EPS = 1e-6
SUBLN_EPS = 1e-5
NEG_INF = -1e30
LAMBDA_INIT = 0.8 - 0.6 * math.exp(-0.3 * 0)

ATTN_TILE = 256
IN_ROWS = 512
FFN_ROWS = 512
FFN_CHUNK = 256
CONV_HALO = 8
VMEM_LIMIT_BYTES = 56 * 1024 * 1024

F32 = jnp.float32
BF16 = jnp.bfloat16


def _bucket_tiles(tile):
    k = np.arange(tile)[:, None]
    q = np.arange(tile)[None, :]
    out = []
    for offset in (0, tile):
        n = q - k + offset
        max_exact = NUM_BUCKETS // 2
        nf = np.maximum(n, 1).astype(np.float64)
        large = max_exact + (np.log(nf / max_exact) / math.log(MAX_DISTANCE / max_exact)
                             * (NUM_BUCKETS - max_exact)).astype(np.int64)
        large = np.minimum(large, NUM_BUCKETS - 1)
        b = np.where(n < max_exact, n, large)
        out.append(np.where(n < 0, -1, b))
    return np.stack(out).astype(np.int32)


def _bias_kernel(rb_ref, bkt_ref, out_ref):
    h = pl.program_id(0)
    bkt = bkt_ref[...]

    def body(i, acc):
        return jnp.where(bkt == i, rb_ref[i, h], acc)

    out_ref[0] = lax.fori_loop(0, NUM_BUCKETS, body, jnp.full(bkt.shape, NEG_INF, F32))


def _bias_tiles(rel_bias):
    t = ATTN_TILE
    bkt = jnp.asarray(_bucket_tiles(t))
    return pl.pallas_call(
        _bias_kernel,
        grid=(N_HEADS,),
        in_specs=[pl.BlockSpec(memory_space=pltpu.SMEM),
                  pl.BlockSpec((2, t, t), lambda h: (0, 0, 0))],
        out_specs=pl.BlockSpec((1, 2, t, t), lambda h: (h, 0, 0, 0)),
        out_shape=jax.ShapeDtypeStruct((N_HEADS, 2, t, t), F32),
        name="bias_tiles",
    )(rel_bias, bkt)


def _in_proj_kernel(x_ref, g_ref, w_ref, pw_ref, ps_ref,
                    ypool_ref, qt_ref, k_ref, vt_ref, ext_ref):
    si = pl.program_id(1)
    tm = x_ref.shape[1]
    x = x_ref[0]
    ms = jnp.mean(x * x, axis=-1, keepdims=True)
    hn = (x * lax.rsqrt(ms + NORM_EPS) * g_ref[...]).astype(BF16)
    z = jnp.dot(hn, w_ref[...], preferred_element_type=F32)

    @pl.when(si == 0)
    def _():
        ext_ref[0:POOL_HALO, :] = jnp.zeros((POOL_HALO, POOL_WIDTH), F32)

    zp = z[:, :POOL_WIDTH]
    ext_ref[POOL_HALO:, :] = zp
    tpos = si * tm + lax.broadcasted_iota(jnp.int32, (tm, 1), 0) + 1
    for gi, w in enumerate(POOL_WINDOWS):
        lo, hi = gi * POOL_GROUP_DIM, (gi + 1) * POOL_GROUP_DIM
        tot = zp[:, lo:hi]
        for i in range(1, w):
            tot = tot + ext_ref[pl.ds(POOL_HALO - i, tm), lo:hi]
        cnt = jnp.minimum(tpos, w).astype(F32)
        pooled = tot / cnt - zp[:, lo:hi]
        yg = jnp.dot(pooled.astype(BF16), pw_ref[gi], preferred_element_type=F32)
        ypool_ref[0, :, lo:hi] = (yg * ps_ref[:, lo:hi]).astype(BF16)
    ext_ref[0:POOL_HALO, :] = zp[tm - POOL_HALO:, :]

    t = ATTN_TILE
    q0, k0, v0 = POOL_WIDTH, POOL_WIDTH + ATTN_WIDTH, POOL_WIDTH + 2 * ATTN_WIDTH
    scale = HEAD_DIM ** -0.5
    for h in range(N_HEADS):
        zq = z[:, q0 + h * V_DIM:q0 + (h + 1) * V_DIM] * scale
        qt_ref[0, h * V_DIM:(h + 1) * V_DIM, :] = zq.T.astype(BF16)
        k_ref[0, h] = z[:, k0 + h * V_DIM:k0 + (h + 1) * V_DIM].astype(BF16)
        vt = z[:, v0 + h * V_DIM:v0 + (h + 1) * V_DIM].T.astype(BF16)
        for j in range(tm // t):
            vt_ref[0, h, j] = vt[:, j * t:(j + 1) * t]


def _in_proj(x, g, w_in, pool_w, pool_scale):
    b, s, d = x.shape
    tm, t = IN_ROWS, ATTN_TILE
    return pl.pallas_call(
        _in_proj_kernel,
        grid=(b, s // tm),
        in_specs=[
            pl.BlockSpec((1, tm, d), lambda bi, si: (bi, si, 0)),
            pl.BlockSpec((1, d), lambda bi, si: (0, 0)),
            pl.BlockSpec((d, IN_PROJ_WIDTH), lambda bi, si: (0, 0)),
            pl.BlockSpec((POOL_GROUPS, POOL_GROUP_DIM, POOL_GROUP_DIM), lambda bi, si: (0, 0, 0)),
            pl.BlockSpec((1, POOL_WIDTH), lambda bi, si: (0, 0)),
        ],
        out_specs=[
            pl.BlockSpec((1, tm, POOL_WIDTH), lambda bi, si: (bi, si, 0)),
            pl.BlockSpec((1, ATTN_WIDTH, tm), lambda bi, si: (bi, 0, si)),
            pl.BlockSpec((1, N_HEADS, tm, V_DIM), lambda bi, si: (bi, 0, si, 0)),
            pl.BlockSpec((1, N_HEADS, tm // t, V_DIM, t), lambda bi, si: (bi, 0, si, 0, 0)),
        ],
        out_shape=[
            jax.ShapeDtypeStruct((b, s, POOL_WIDTH), BF16),
            jax.ShapeDtypeStruct((b, ATTN_WIDTH, s), BF16),
            jax.ShapeDtypeStruct((b, N_HEADS, s, V_DIM), BF16),
            jax.ShapeDtypeStruct((b, N_HEADS, s // t, V_DIM, t), BF16),
        ],
        scratch_shapes=[pltpu.VMEM((POOL_HALO + tm, POOL_WIDTH), F32)],
        compiler_params=pltpu.CompilerParams(
            dimension_semantics=("arbitrary", "arbitrary"),
            vmem_limit_bytes=VMEM_LIMIT_BYTES),
        name="in_proj",
    )(x, g, w_in, pool_w, pool_scale)


def _attn_kernel(rb_ref, qt_ref, k_ref, vt_ref, bias_ref, lamv_ref, sg_ref,
                 out_ref, m_ref, l_ref, acc_ref):
    h = pl.program_id(1)
    qi = pl.program_id(2)
    t = ATTN_TILE

    qt = qt_ref[0]
    row = lax.broadcasted_iota(jnp.int32, qt.shape, 0)
    zero = jnp.zeros_like(qt)
    qs = jnp.concatenate([jnp.where(row < HEAD_DIM, qt, zero),
                          jnp.where(row >= HEAD_DIM, qt, zero)], axis=1)

    def scores(j):
        kj = k_ref[0, 0, pl.ds(pl.multiple_of(j * t, t), t), :]
        return jnp.dot(kj, qs, preferred_element_type=F32)

    def both(tile):
        return jnp.concatenate([tile, tile], axis=1)

    s = scores(qi) + both(bias_ref[0, 0])
    m0 = jnp.max(s, axis=0, keepdims=True)
    p = jnp.exp(s - m0)
    m_ref[...] = m0
    l_ref[...] = jnp.sum(p, axis=0, keepdims=True)
    acc_ref[...] = jnp.dot(vt_ref[0, 0, qi], p.astype(BF16), preferred_element_type=F32)

    def update(j, s, off):
        m_old = m_ref[...]
        m_new = jnp.maximum(m_old, jnp.max(s, axis=0, keepdims=True) + off)
        alpha = jnp.exp(m_old - m_new)
        p = jnp.exp(s - (m_new - off))
        l_ref[...] = alpha * l_ref[...] + jnp.sum(p, axis=0, keepdims=True)
        pv = jnp.dot(vt_ref[0, 0, j], p.astype(BF16), preferred_element_type=F32)
        acc_ref[...] = alpha * acc_ref[...] + pv
        m_ref[...] = m_new

    @pl.when(qi >= 1)
    def _():
        j = qi - 1
        update(j, scores(j) + both(bias_ref[0, 1]), 0.0)

    far_bias = rb_ref[NUM_BUCKETS - 1, h]

    def far_body(j, carry):
        update(j, scores(j), far_bias)
        return carry

    lax.fori_loop(0, qi - 1, far_body, 0)

    lamv = lamv_ref[...]
    lam = (jnp.exp(jnp.sum(lamv[0:1] * lamv[1:2], axis=1, keepdims=True))
           - jnp.exp(jnp.sum(lamv[2:3] * lamv[3:4], axis=1, keepdims=True))
           + LAMBDA_INIT)
    o = acc_ref[...] / l_ref[...]
    o = (o[:, :t] - lam * o[:, t:]).T
    ms = jnp.mean(o * o, axis=-1, keepdims=True)
    y = o * lax.rsqrt(ms + SUBLN_EPS) * sg_ref[...] * (1.0 - LAMBDA_INIT)
    out_ref[0] = y.astype(BF16)


def _diff_attn(rel_bias, qt, k, vt, bias, lamv, subln_g):
    b, _, s = qt.shape
    t = ATTN_TILE
    nq = s // t
    return pl.pallas_call(
        _attn_kernel,
        grid=(b, N_HEADS, nq),
        in_specs=[
            pl.BlockSpec(memory_space=pltpu.SMEM),
            pl.BlockSpec((1, V_DIM, t), lambda bi, h, qi: (bi, h, qi)),
            pl.BlockSpec((1, 1, s, V_DIM), lambda bi, h, qi: (bi, h, 0, 0)),
            pl.BlockSpec((1, 1, nq, V_DIM, t), lambda bi, h, qi: (bi, h, 0, 0, 0)),
            pl.BlockSpec((1, 2, t, t), lambda bi, h, qi: (h, 0, 0, 0)),
            pl.BlockSpec((4, HEAD_DIM), lambda bi, h, qi: (0, 0)),
            pl.BlockSpec((1, V_DIM), lambda bi, h, qi: (0, 0)),
        ],
        out_specs=pl.BlockSpec((1, t, V_DIM), lambda bi, h, qi: (bi, qi, h)),
        out_shape=jax.ShapeDtypeStruct((b, s, ATTN_WIDTH), BF16),
        scratch_shapes=[pltpu.VMEM((1, 2 * t), F32),
                        pltpu.VMEM((1, 2 * t), F32),
                        pltpu.VMEM((V_DIM, 2 * t), F32)],
        compiler_params=pltpu.CompilerParams(
            dimension_semantics=("arbitrary", "arbitrary", "arbitrary"),
            vmem_limit_bytes=VMEM_LIMIT_BYTES),
        name="diff_attn",
    )(rel_bias, qt, k, vt, bias, lamv, subln_g)


def _ffn_kernel(x_ref, yp_ref, ya_ref, wo_ref, gf_ref, wg_ref, wu_ref, cw_ref, cb_ref,
                w2_ref, gl_ref, out_ref, carry_ref, gbuf_ref, act_ref):
    si = pl.program_id(1)
    tm = x_ref.shape[1]
    ck = FFN_CHUNK

    y = (jnp.dot(yp_ref[0], wo_ref[0:POOL_WIDTH, :], preferred_element_type=F32)
         + jnp.dot(ya_ref[0], wo_ref[POOL_WIDTH:, :], preferred_element_type=F32))
    x1 = x_ref[0] + y
    ms = jnp.mean(x1 * x1, axis=-1, keepdims=True)
    hn = (x1 * lax.rsqrt(ms + NORM_EPS) * gf_ref[...]).astype(BF16)

    @pl.when(si == 0)
    def _():
        carry_ref[...] = jnp.zeros(carry_ref.shape, F32)

    for c in range(D_FF // ck):
        lo, hi = c * ck, (c + 1) * ck
        g = jnp.dot(hn, wg_ref[:, lo:hi], preferred_element_type=F32)
        u = jnp.dot(hn, wu_ref[:, lo:hi], preferred_element_type=F32)
        gbuf_ref[0:CONV_HALO, :] = carry_ref[:, lo:hi]
        gbuf_ref[CONV_HALO:, :] = g
        g1 = gbuf_ref[pl.ds(CONV_HALO - 1, tm), :]
        g2 = gbuf_ref[pl.ds(CONV_HALO - 2, tm), :]
        carry_ref[:, lo:hi] = g[tm - CONV_HALO:, :]
        cv = (cw_ref[0:1, lo:hi] * g2 + cw_ref[1:2, lo:hi] * g1 + cw_ref[2:3, lo:hi] * g
              + cb_ref[:, lo:hi])
        a = cv * (1.0 / (1.0 + jnp.exp(-cv))) * u
        act_ref[:, lo:hi] = a.astype(BF16)

    x2 = x1 + jnp.dot(act_ref[...], w2_ref[...], preferred_element_type=F32)
    ms2 = jnp.mean(x2 * x2, axis=-1, keepdims=True)
    out_ref[0] = x2 * lax.rsqrt(ms2 + NORM_EPS) * gl_ref[...]


def _out_ffn(x, y_pool, y_attn, w_out, g_ffn, wg, wu, conv_w, conv_b, w2, g_final):
    b, s, d = x.shape
    tm = FFN_ROWS
    const = lambda bi, si: (0, 0)
    resident = functools.partial(pl.BlockSpec, index_map=const, pipeline_mode=pl.Buffered(1))
    return pl.pallas_call(
        _ffn_kernel,
        grid=(b, s // tm),
        in_specs=[
            pl.BlockSpec((1, tm, d), lambda bi, si: (bi, si, 0)),
            pl.BlockSpec((1, tm, POOL_WIDTH), lambda bi, si: (bi, si, 0)),
            pl.BlockSpec((1, tm, ATTN_WIDTH), lambda bi, si: (bi, si, 0)),
            resident((d, d)),
            resident((1, d)),
            resident((d, D_FF)),
            resident((d, D_FF)),
            resident((3, D_FF)),
            resident((1, D_FF)),
            resident((D_FF, d)),
            resident((1, d)),
        ],
        out_specs=pl.BlockSpec((1, tm, d), lambda bi, si: (bi, si, 0)),
        out_shape=jax.ShapeDtypeStruct((b, s, d), F32),
        scratch_shapes=[pltpu.VMEM((CONV_HALO, D_FF), F32),
                        pltpu.VMEM((CONV_HALO + tm, FFN_CHUNK), F32),
                        pltpu.VMEM((tm, D_FF), BF16)],
        compiler_params=pltpu.CompilerParams(
            dimension_semantics=("arbitrary", "arbitrary"),
            vmem_limit_bytes=VMEM_LIMIT_BYTES),
        name="out_ffn",
    )(x, y_pool, y_attn, w_out, g_ffn, wg, wu, conv_w, conv_b, w2, g_final)


def kernel(x, norm_mix_g, w_in, pool_w, pool_scale, lambda_q1, lambda_k1, lambda_q2, lambda_k2,
           subln_g, rel_bias, w_out, norm_ffn_g, ffn_w_in, ffn_conv_w, ffn_conv_b, ffn_w_out,
           norm_final_g):
    bias = _bias_tiles(rel_bias)
    y_pool, qt, k, vt = _in_proj(x, norm_mix_g, w_in[0].astype(BF16), pool_w[0].astype(BF16),
                                 pool_scale)
    lamv = jnp.concatenate([lambda_q1, lambda_k1, lambda_q2, lambda_k2], axis=0)
    y_attn = _diff_attn(rel_bias, qt, k, vt, bias, lamv, subln_g)
    wf = ffn_w_in[0]
    return _out_ffn(x, y_pool, y_attn, w_out[0].astype(BF16), norm_ffn_g,
                    wf[:, :D_FF].astype(BF16), wf[:, D_FF:].astype(BF16),
                    ffn_conv_w[0], ffn_conv_b, ffn_w_out[0].astype(BF16),
                    norm_final_g.reshape(1, D_MODEL))
```

```python
import functools
import math

import numpy as np
import jax
import jax.numpy as jnp
from jax import lax
from jax.experimental import pallas as pl
from jax.experimental.pallas import tpu as pltpu

D_MODEL = 1024
POOL_WIDTH = 512
POOL_GROUPS = 4
POOL_GROUP_DIM = 128
POOL_WINDOWS = (2, 4, 8, 16)
POOL_HALO = 16
ATTN_WIDTH = 512
N_HEADS = 4
HEAD_DIM = 64
V_DIM = 128
IN_PROJ_WIDTH = 2048
NUM_BUCKETS = 32
MAX_DISTANCE = 128
D_FF = 2816
NORM_EPS = 1e-6
SUBLN_EPS = 1e-5
NEG_INF = -1e30
LAMBDA_INIT = 0.8 - 0.6 * math.exp(-0.3 * 0)
LOG2E = math.log2(math.e)

ATTN_TILE = 256
QK_LOOKAHEAD = 2
IN_ROWS = 512
FFN_ROWS = 512
FFN_CHUNK = 256
CONV_HALO = 8
VMEM_LIMIT_BYTES = 56 * 1024 * 1024

F32 = jnp.float32
BF16 = jnp.bfloat16


def _bucket_tiles(tile):
    k = np.arange(tile)[:, None]
    q = np.arange(tile)[None, :]
    out = []
    for offset in (0, tile):
        n = q - k + offset
        max_exact = NUM_BUCKETS // 2
        nf = np.maximum(n, 1).astype(np.float64)
        large = max_exact + (np.log(nf / max_exact) / math.log(MAX_DISTANCE / max_exact)
                             * (NUM_BUCKETS - max_exact)).astype(np.int64)
        large = np.minimum(large, NUM_BUCKETS - 1)
        b = np.where(n < max_exact, n, large)
        out.append(np.where(n < 0, -1, b))
    return np.stack(out).astype(np.int32)


def _bias_kernel(rb_ref, bkt_ref, out_ref):
    h = pl.program_id(0)
    bkt = bkt_ref[...]

    def body(i, acc):
        return jnp.where(bkt == i, rb_ref[i, h] * LOG2E, acc)

    out_ref[0] = lax.fori_loop(0, NUM_BUCKETS, body, jnp.full(bkt.shape, NEG_INF, F32))


def _bias_tiles(rel_bias):
    t = ATTN_TILE
    bkt = jnp.asarray(_bucket_tiles(t))
    return pl.pallas_call(
        _bias_kernel,
        grid=(N_HEADS,),
        in_specs=[pl.BlockSpec(memory_space=pltpu.SMEM),
                  pl.BlockSpec((2, t, t), lambda h: (0, 0, 0))],
        out_specs=pl.BlockSpec((1, 2, t, t), lambda h: (h, 0, 0, 0)),
        out_shape=jax.ShapeDtypeStruct((N_HEADS, 2, t, t), F32),
        name="bias_tiles",
    )(rel_bias, bkt)


def _in_proj_kernel(x_ref, g_ref, w_ref, pw_ref, ps_ref,
                    ypool_ref, qt_ref, k_ref, vt_ref, ext_ref):
    si = pl.program_id(1)
    tm = x_ref.shape[1]
    x = x_ref[0]
    ms = jnp.mean(x * x, axis=-1, keepdims=True)
    hn = (x * lax.rsqrt(ms + NORM_EPS) * g_ref[...]).astype(BF16)
    z = jnp.dot(hn, w_ref[...], preferred_element_type=F32)

    @pl.when(si == 0)
    def _():
        ext_ref[0:POOL_HALO, :] = jnp.zeros((POOL_HALO, POOL_WIDTH), F32)

    zp = z[:, :POOL_WIDTH]
    ext_ref[POOL_HALO:, :] = zp
    tpos = si * tm + lax.broadcasted_iota(jnp.int32, (tm, 1), 0) + 1
    for gi, w in enumerate(POOL_WINDOWS):
        lo, hi = gi * POOL_GROUP_DIM, (gi + 1) * POOL_GROUP_DIM
        tot = zp[:, lo:hi]
        for i in range(1, w):
            tot = tot + ext_ref[pl.ds(POOL_HALO - i, tm), lo:hi]
        cnt = jnp.minimum(tpos, w).astype(F32)
        pooled = tot / cnt - zp[:, lo:hi]
        yg = jnp.dot(pooled.astype(BF16), pw_ref[gi], preferred_element_type=F32)
        ypool_ref[0, :, lo:hi] = (yg * ps_ref[:, lo:hi]).astype(BF16)
    ext_ref[0:POOL_HALO, :] = zp[tm - POOL_HALO:, :]

    t = ATTN_TILE
    q0, k0, v0 = POOL_WIDTH, POOL_WIDTH + ATTN_WIDTH, POOL_WIDTH + 2 * ATTN_WIDTH
    scale = LOG2E * HEAD_DIM ** -0.5
    for h in range(N_HEADS):
        zq = z[:, q0 + h * V_DIM:q0 + (h + 1) * V_DIM] * scale
        qt_ref[0, h * V_DIM:(h + 1) * V_DIM, :] = zq.T.astype(BF16)
        k_ref[0, h] = z[:, k0 + h * V_DIM:k0 + (h + 1) * V_DIM].astype(BF16)
        vt = z[:, v0 + h * V_DIM:v0 + (h + 1) * V_DIM].T.astype(BF16)
        for j in range(tm // t):
            vt_ref[0, h, j] = vt[:, j * t:(j + 1) * t]


def _in_proj(x, g, w_in, pool_w, pool_scale):
    b, s, d = x.shape
    tm, t = IN_ROWS, ATTN_TILE
    return pl.pallas_call(
        _in_proj_kernel,
        grid=(b, s // tm),
        in_specs=[
            pl.BlockSpec((1, tm, d), lambda bi, si: (bi, si, 0)),
            pl.BlockSpec((1, d), lambda bi, si: (0, 0)),
            pl.BlockSpec((d, IN_PROJ_WIDTH), lambda bi, si: (0, 0)),
            pl.BlockSpec((POOL_GROUPS, POOL_GROUP_DIM, POOL_GROUP_DIM), lambda bi, si: (0, 0, 0)),
            pl.BlockSpec((1, POOL_WIDTH), lambda bi, si: (0, 0)),
        ],
        out_specs=[
            pl.BlockSpec((1, tm, POOL_WIDTH), lambda bi, si: (bi, si, 0)),
            pl.BlockSpec((1, ATTN_WIDTH, tm), lambda bi, si: (bi, 0, si)),
            pl.BlockSpec((1, N_HEADS, tm, V_DIM), lambda bi, si: (bi, 0, si, 0)),
            pl.BlockSpec((1, N_HEADS, tm // t, V_DIM, t), lambda bi, si: (bi, 0, si, 0, 0)),
        ],
        out_shape=[
            jax.ShapeDtypeStruct((b, s, POOL_WIDTH), BF16),
            jax.ShapeDtypeStruct((b, ATTN_WIDTH, s), BF16),
            jax.ShapeDtypeStruct((b, N_HEADS, s, V_DIM), BF16),
            jax.ShapeDtypeStruct((b, N_HEADS, s // t, V_DIM, t), BF16),
        ],
        scratch_shapes=[pltpu.VMEM((POOL_HALO + tm, POOL_WIDTH), F32)],
        compiler_params=pltpu.CompilerParams(
            dimension_semantics=("arbitrary", "arbitrary"),
            vmem_limit_bytes=VMEM_LIMIT_BYTES),
        name="in_proj",
    )(x, g, w_in, pool_w, pool_scale)


def _attn_kernel(rb_ref, qt_ref, k_ref, vt_ref, bias_ref, lamv_ref, sg_ref, out_ref, *scratch):
    qi = pl.program_id(1)
    t = ATTN_TILE
    m_refs, l_refs, alpha_refs, acc_refs, qs_refs, s_refs, p_refs = (
        scratch[i * N_HEADS:(i + 1) * N_HEADS] for i in range(7))

    for h in range(N_HEADS):
        qt = qt_ref[0, h * V_DIM:(h + 1) * V_DIM, :]
        row = lax.broadcasted_iota(jnp.int32, qt.shape, 0)
        zero = jnp.zeros_like(qt)
        qs_refs[h][:, 0:t] = jnp.where(row < HEAD_DIM, qt, zero)
        qs_refs[h][:, t:2 * t] = jnp.where(row >= HEAD_DIM, qt, zero)

        m_refs[h][...] = jnp.full(m_refs[h].shape, -jnp.inf, F32)
        l_refs[h][...] = jnp.zeros(l_refs[h].shape, F32)
        acc_refs[h][...] = jnp.zeros(acc_refs[h].shape, F32)

    def qk(h, j):
        kj = k_ref[0, h, pl.ds(pl.multiple_of(j * t, t), t), :]
        s_refs[h][...] = jnp.dot(kj, qs_refs[h][...], preferred_element_type=F32)

    def softmax(h, bias_tile, off):
        s = s_refs[h][...]
        if bias_tile is not None:
            s = s + jnp.concatenate([bias_tile, bias_tile], axis=1)
        m_old = m_refs[h][...]
        m_new = jnp.maximum(m_old, jnp.max(s, axis=0, keepdims=True) + off)
        alpha = jnp.exp2(m_old - m_new)
        p = jnp.exp2(s - (m_new - off))
        l_refs[h][...] = alpha * l_refs[h][...] + jnp.sum(p, axis=0, keepdims=True)
        p_refs[h][...] = p.astype(BF16)
        alpha_refs[h][...] = alpha
        m_refs[h][...] = m_new

    def pv(h, j):
        r = jnp.dot(vt_ref[0, h, j], p_refs[h][...], preferred_element_type=F32)
        acc_refs[h][...] = alpha_refs[h][...] * acc_refs[h][...] + r

    def sweep(j, bias_idx, off, lookahead):
        for h in range(N_HEADS):
            nh = h + QK_LOOKAHEAD
            if nh < N_HEADS:
                qk(nh, j)
            elif lookahead:
                qk(nh - N_HEADS, j + 1)
            softmax(h, None if bias_idx is None else bias_ref[h, bias_idx],
                    0.0 if off is None else off[h])
            pv(h, j)

    for h in range(QK_LOOKAHEAD):
        qk(h, 0)

    far_bias = [rb_ref[NUM_BUCKETS - 1, h] * LOG2E for h in range(N_HEADS)]

    def far_body(j, carry):
        sweep(j, None, far_bias, True)
        return carry

    lax.fori_loop(0, qi - 1, far_body, 0)

    @pl.when(qi >= 1)
    def _():
        sweep(qi - 1, 1, None, True)

    sweep(qi, 0, None, False)

    lamv = lamv_ref[...]
    lam = (jnp.exp(jnp.sum(lamv[0:1] * lamv[1:2], axis=1, keepdims=True))
           - jnp.exp(jnp.sum(lamv[2:3] * lamv[3:4], axis=1, keepdims=True))
           + LAMBDA_INIT)
    for h in range(N_HEADS):
        o = acc_refs[h][...] / l_refs[h][...]
        o = (o[:, :t] - lam * o[:, t:]).T
        ms = jnp.mean(o * o, axis=-1, keepdims=True)
        y = o * lax.rsqrt(ms + SUBLN_EPS) * sg_ref[...] * (1.0 - LAMBDA_INIT)
        out_ref[0, :, h * V_DIM:(h + 1) * V_DIM] = y.astype(BF16)


def _diff_attn(rel_bias, qt, k, vt, bias, lamv, subln_g):
    b, _, s = qt.shape
    t = ATTN_TILE
    nq = s // t
    per_head = lambda shape, dtype: [pltpu.VMEM(shape, dtype) for _ in range(N_HEADS)]
    return pl.pallas_call(
        _attn_kernel,
        grid=(b, nq),
        in_specs=[
            pl.BlockSpec(memory_space=pltpu.SMEM),
            pl.BlockSpec((1, ATTN_WIDTH, t), lambda bi, qi: (bi, 0, qi)),
            pl.BlockSpec((1, N_HEADS, s, V_DIM), lambda bi, qi: (bi, 0, 0, 0)),
            pl.BlockSpec((1, N_HEADS, nq, V_DIM, t), lambda bi, qi: (bi, 0, 0, 0, 0)),
            pl.BlockSpec((N_HEADS, 2, t, t), lambda bi, qi: (0, 0, 0, 0)),
            pl.BlockSpec((4, HEAD_DIM), lambda bi, qi: (0, 0)),
            pl.BlockSpec((1, V_DIM), lambda bi, qi: (0, 0)),
        ],
        out_specs=pl.BlockSpec((1, t, ATTN_WIDTH), lambda bi, qi: (bi, qi, 0)),
        out_shape=jax.ShapeDtypeStruct((b, s, ATTN_WIDTH), BF16),
        scratch_shapes=(per_head((1, 2 * t), F32) + per_head((1, 2 * t), F32)
                        + per_head((1, 2 * t), F32)
                        + per_head((V_DIM, 2 * t), F32) + per_head((V_DIM, 2 * t), BF16)
                        + per_head((t, 2 * t), F32) + per_head((t, 2 * t), BF16)),
        compiler_params=pltpu.CompilerParams(
            dimension_semantics=("arbitrary", "arbitrary"),
            vmem_limit_bytes=VMEM_LIMIT_BYTES),
        name="diff_attn",
    )(rel_bias, qt, k, vt, bias, lamv, subln_g)


def _ffn_kernel(x_ref, yp_ref, ya_ref, wo_ref, gf_ref, wg_ref, wu_ref, cw_ref, cb_ref,
                w2_ref, gl_ref, out_ref, carry_ref, gbuf_ref, act_ref):
    si = pl.program_id(1)
    tm = x_ref.shape[1]
    ck = FFN_CHUNK

    y = (jnp.dot(yp_ref[0], wo_ref[0:POOL_WIDTH, :], preferred_element_type=F32)
         + jnp.dot(ya_ref[0], wo_ref[POOL_WIDTH:, :], preferred_element_type=F32))
    x1 = x_ref[0] + y
    ms = jnp.mean(x1 * x1, axis=-1, keepdims=True)
    hn = (x1 * lax.rsqrt(ms + NORM_EPS) * gf_ref[...]).astype(BF16)

    @pl.when(si == 0)
    def _():
        carry_ref[...] = jnp.zeros(carry_ref.shape, F32)

    for c in range(D_FF // ck):
        lo, hi = c * ck, (c + 1) * ck
        g = jnp.dot(hn, wg_ref[:, lo:hi], preferred_element_type=F32)
        u = jnp.dot(hn, wu_ref[:, lo:hi], preferred_element_type=F32)
        gbuf_ref[0:CONV_HALO, :] = carry_ref[:, lo:hi]
        gbuf_ref[CONV_HALO:, :] = g
        g1 = gbuf_ref[pl.ds(CONV_HALO - 1, tm), :]
        g2 = gbuf_ref[pl.ds(CONV_HALO - 2, tm), :]
        carry_ref[:, lo:hi] = g[tm - CONV_HALO:, :]
        cv = (cw_ref[0:1, lo:hi] * g2 + cw_ref[1:2, lo:hi] * g1 + cw_ref[2:3, lo:hi] * g
              + cb_ref[:, lo:hi])
        a = cv * (1.0 / (1.0 + jnp.exp(-cv))) * u
        act_ref[:, lo:hi] = a.astype(BF16)

    x2 = x1 + jnp.dot(act_ref[...], w2_ref[...], preferred_element_type=F32)
    ms2 = jnp.mean(x2 * x2, axis=-1, keepdims=True)
    out_ref[0] = x2 * lax.rsqrt(ms2 + NORM_EPS) * gl_ref[...]


def _out_ffn(x, y_pool, y_attn, w_out, g_ffn, wg, wu, conv_w, conv_b, w2, g_final):
    b, s, d = x.shape
    tm = FFN_ROWS
    const = lambda bi, si: (0, 0)
    resident = functools.partial(pl.BlockSpec, index_map=const, pipeline_mode=pl.Buffered(1))
    return pl.pallas_call(
        _ffn_kernel,
        grid=(b, s // tm),
        in_specs=[
            pl.BlockSpec((1, tm, d), lambda bi, si: (bi, si, 0)),
            pl.BlockSpec((1, tm, POOL_WIDTH), lambda bi, si: (bi, si, 0)),
            pl.BlockSpec((1, tm, ATTN_WIDTH), lambda bi, si: (bi, si, 0)),
            resident((d, d)),
            resident((1, d)),
            resident((d, D_FF)),
            resident((d, D_FF)),
            resident((3, D_FF)),
            resident((1, D_FF)),
            resident((D_FF, d)),
            resident((1, d)),
        ],
        out_specs=pl.BlockSpec((1, tm, d), lambda bi, si: (bi, si, 0)),
        out_shape=jax.ShapeDtypeStruct((b, s, d), F32),
        scratch_shapes=[pltpu.VMEM((CONV_HALO, D_FF), F32),
                        pltpu.VMEM((CONV_HALO + tm, FFN_CHUNK), F32),
                        pltpu.VMEM((tm, D_FF), BF16)],
        compiler_params=pltpu.CompilerParams(
            dimension_semantics=("arbitrary", "arbitrary"),
            vmem_limit_bytes=VMEM_LIMIT_BYTES),
        name="out_ffn",
    )(x, y_pool, y_attn, w_out, g_ffn, wg, wu, conv_w, conv_b, w2, g_final)


def kernel(x, norm_mix_g, w_in, pool_w, pool_scale, lambda_q1, lambda_k1, lambda_q2, lambda_k2,
           subln_g, rel_bias, w_out, norm_ffn_g, ffn_w_in, ffn_conv_w, ffn_conv_b, ffn_w_out,
           norm_final_g):
    bias = _bias_tiles(rel_bias)
    y_pool, qt, k, vt = _in_proj(x, norm_mix_g, w_in[0].astype(BF16), pool_w[0].astype(BF16),
                                 pool_scale)
    lamv = jnp.concatenate([lambda_q1, lambda_k1, lambda_q2, lambda_k2], axis=0)
    y_attn = _diff_attn(rel_bias, qt, k, vt, bias, lamv, subln_g)
    wf = ffn_w_in[0]
    return _out_ffn(x, y_pool, y_attn, w_out[0].astype(BF16), norm_ffn_g,
                    wf[:, :D_FF].astype(BF16), wf[:, D_FF:].astype(BF16),
                    ffn_conv_w[0], ffn_conv_b, ffn_w_out[0].astype(BF16),
                    norm_final_g.reshape(1, D_MODEL))
```

```python
import functools
import math

import numpy as np
import jax
import jax.numpy as jnp
from jax import lax
from jax.experimental import pallas as pl
from jax.experimental.pallas import tpu as pltpu

D_MODEL = 1024
POOL_WIDTH = 512
POOL_GROUPS = 4
POOL_GROUP_DIM = 128
POOL_WINDOWS = (2, 4, 8, 16)
POOL_HALO = 16
ATTN_WIDTH = 512
N_HEADS = 4
HEAD_DIM = 64
V_DIM = 128
IN_PROJ_WIDTH = 2048
NUM_BUCKETS = 32
MAX_DISTANCE = 128
D_FF = 2816
NORM_EPS = 1e-6
SUBLN_EPS = 1e-5
NEG_INF = -1e30
LAMBDA_INIT = 0.8 - 0.6 * math.exp(-0.3 * 0)
LOG2E = math.log2(math.e)

ATTN_TILE = 256
BIAS_TILE = 256
QK_LOOKAHEAD = 2
IN_ROWS = 512
FFN_ROWS = 512
FFN_CHUNK = 256
CONV_HALO = 8
VMEM_LIMIT_BYTES = 56 * 1024 * 1024

F32 = jnp.float32
BF16 = jnp.bfloat16


def _bucket_tiles(tile):
    k = np.arange(tile)[:, None]
    q = np.arange(tile)[None, :]
    out = []
    for offset in (0, tile):
        n = q - k + offset
        max_exact = NUM_BUCKETS // 2
        nf = np.maximum(n, 1).astype(np.float64)
        large = max_exact + (np.log(nf / max_exact) / math.log(MAX_DISTANCE / max_exact)
                             * (NUM_BUCKETS - max_exact)).astype(np.int64)
        large = np.minimum(large, NUM_BUCKETS - 1)
        b = np.where(n < max_exact, n, large)
        out.append(np.where(n < 0, -1, b))
    return np.stack(out).astype(np.int32)


def _bias_kernel(rb_ref, bkt_ref, out_ref):
    h = pl.program_id(0)
    bkt = bkt_ref[...]

    def body(i, acc):
        return jnp.where(bkt == i, rb_ref[i, h] * LOG2E, acc)

    out_ref[0] = lax.fori_loop(0, NUM_BUCKETS, body, jnp.full(bkt.shape, NEG_INF, F32))


def _bias_tiles(rel_bias):
    t = BIAS_TILE
    bkt = jnp.asarray(_bucket_tiles(t))
    return pl.pallas_call(
        _bias_kernel,
        grid=(N_HEADS,),
        in_specs=[pl.BlockSpec(memory_space=pltpu.SMEM),
                  pl.BlockSpec((2, t, t), lambda h: (0, 0, 0))],
        out_specs=pl.BlockSpec((1, 2, t, t), lambda h: (h, 0, 0, 0)),
        out_shape=jax.ShapeDtypeStruct((N_HEADS, 2, t, t), F32),
        name="bias_tiles",
    )(rel_bias, bkt)


def _in_proj_kernel(x_ref, g_ref, w_ref, pw_ref, ps_ref,
                    ypool_ref, qt_ref, k_ref, vt_ref, ext_ref):
    si = pl.program_id(1)
    tm = x_ref.shape[1]
    x = x_ref[0]
    ms = jnp.mean(x * x, axis=-1, keepdims=True)
    hn = (x * lax.rsqrt(ms + NORM_EPS) * g_ref[...]).astype(BF16)
    z = jnp.dot(hn, w_ref[...], preferred_element_type=F32)

    @pl.when(si == 0)
    def _():
        ext_ref[0:POOL_HALO, :] = jnp.zeros((POOL_HALO, POOL_WIDTH), F32)

    zp = z[:, :POOL_WIDTH]
    ext_ref[POOL_HALO:, :] = zp
    tpos = si * tm + lax.broadcasted_iota(jnp.int32, (tm, 1), 0) + 1
    for gi, w in enumerate(POOL_WINDOWS):
        lo, hi = gi * POOL_GROUP_DIM, (gi + 1) * POOL_GROUP_DIM
        tot = zp[:, lo:hi]
        for i in range(1, w):
            tot = tot + ext_ref[pl.ds(POOL_HALO - i, tm), lo:hi]
        cnt = jnp.minimum(tpos, w).astype(F32)
        pooled = tot / cnt - zp[:, lo:hi]
        yg = jnp.dot(pooled.astype(BF16), pw_ref[gi], preferred_element_type=F32)
        ypool_ref[0, :, lo:hi] = (yg * ps_ref[:, lo:hi]).astype(BF16)
    ext_ref[0:POOL_HALO, :] = zp[tm - POOL_HALO:, :]

    t = ATTN_TILE
    q0, k0, v0 = POOL_WIDTH, POOL_WIDTH + ATTN_WIDTH, POOL_WIDTH + 2 * ATTN_WIDTH
    scale = LOG2E * HEAD_DIM ** -0.5
    for h in range(N_HEADS):
        zq = z[:, q0 + h * V_DIM:q0 + (h + 1) * V_DIM] * scale
        qt_ref[0, h * V_DIM:(h + 1) * V_DIM, :] = zq.T.astype(BF16)
        k_ref[0, h] = z[:, k0 + h * V_DIM:k0 + (h + 1) * V_DIM].astype(BF16)
        vt = z[:, v0 + h * V_DIM:v0 + (h + 1) * V_DIM].T.astype(BF16)
        for j in range(tm // t):
            vt_ref[0, h, j] = vt[:, j * t:(j + 1) * t]


def _in_proj(x, g, w_in, pool_w, pool_scale):
    b, s, d = x.shape
    tm, t = IN_ROWS, ATTN_TILE
    return pl.pallas_call(
        _in_proj_kernel,
        grid=(b, s // tm),
        in_specs=[
            pl.BlockSpec((1, tm, d), lambda bi, si: (bi, si, 0)),
            pl.BlockSpec((1, d), lambda bi, si: (0, 0)),
            pl.BlockSpec((d, IN_PROJ_WIDTH), lambda bi, si: (0, 0)),
            pl.BlockSpec((POOL_GROUPS, POOL_GROUP_DIM, POOL_GROUP_DIM), lambda bi, si: (0, 0, 0)),
            pl.BlockSpec((1, POOL_WIDTH), lambda bi, si: (0, 0)),
        ],
        out_specs=[
            pl.BlockSpec((1, tm, POOL_WIDTH), lambda bi, si: (bi, si, 0)),
            pl.BlockSpec((1, ATTN_WIDTH, tm), lambda bi, si: (bi, 0, si)),
            pl.BlockSpec((1, N_HEADS, tm, V_DIM), lambda bi, si: (bi, 0, si, 0)),
            pl.BlockSpec((1, N_HEADS, tm // t, V_DIM, t), lambda bi, si: (bi, 0, si, 0, 0)),
        ],
        out_shape=[
            jax.ShapeDtypeStruct((b, s, POOL_WIDTH), BF16),
            jax.ShapeDtypeStruct((b, ATTN_WIDTH, s), BF16),
            jax.ShapeDtypeStruct((b, N_HEADS, s, V_DIM), BF16),
            jax.ShapeDtypeStruct((b, N_HEADS, s // t, V_DIM, t), BF16),
        ],
        scratch_shapes=[pltpu.VMEM((POOL_HALO + tm, POOL_WIDTH), F32)],
        compiler_params=pltpu.CompilerParams(
            dimension_semantics=("arbitrary", "arbitrary"),
            vmem_limit_bytes=VMEM_LIMIT_BYTES),
        name="in_proj",
    )(x, g, w_in, pool_w, pool_scale)


def _attn_kernel(rb_ref, qt_ref, k_ref, vt_ref, bias_ref, lamv_ref, sg_ref, out_ref, *scratch):
    qi = pl.program_id(1)
    t = ATTN_TILE
    m_refs, l_refs, smax_refs, acc_refs, qs_refs, s_refs = (
        scratch[i * N_HEADS:(i + 1) * N_HEADS] for i in range(6))

    for h in range(N_HEADS):
        qt = qt_ref[0, h * V_DIM:(h + 1) * V_DIM, :]
        row = lax.broadcasted_iota(jnp.int32, qt.shape, 0)
        zero = jnp.zeros_like(qt)
        qs_refs[h][:, 0:t] = jnp.where(row < HEAD_DIM, qt, zero)
        qs_refs[h][:, t:2 * t] = jnp.where(row >= HEAD_DIM, qt, zero)

        m_refs[h][...] = jnp.full(m_refs[h].shape, -jnp.inf, F32)
        l_refs[h][...] = jnp.zeros(l_refs[h].shape, F32)
        acc_refs[h][...] = jnp.zeros(acc_refs[h].shape, F32)

    far_bias = [rb_ref[NUM_BUCKETS - 1, h] * LOG2E for h in range(N_HEADS)]

    def qk(h, j):
        kj = k_ref[0, h, pl.ds(pl.multiple_of(j * t, t), t), :]
        s = jnp.dot(kj, qs_refs[h][...], preferred_element_type=F32)
        s_refs[h][...] = s
        smax_refs[h][...] = jnp.max(s, axis=0, keepdims=True)

    def tile_bias(h, diag):
        n = t // BIAS_TILE
        pieces = {0: bias_ref[h, 0], 1: bias_ref[h, 1]}
        far = jnp.full((BIAS_TILE, BIAS_TILE), far_bias[h], F32)
        masked = jnp.full((BIAS_TILE, BIAS_TILE), NEG_INF, F32)
        rows = []
        for kb in range(n):
            d = [qb - kb + (0 if diag else n) for qb in range(n)]
            rows.append(jnp.concatenate([masked if x < 0 else pieces.get(x, far) for x in d], axis=1))
        return jnp.concatenate(rows, axis=0)

    def softmax_pv(h, j, diag, off):
        s = s_refs[h][...]
        if diag is None:
            smax = smax_refs[h][...]
        else:
            b = tile_bias(h, diag)
            s = s + jnp.concatenate([b, b], axis=1)
            smax = jnp.max(s, axis=0, keepdims=True)
        m_old = m_refs[h][...]
        m_new = jnp.maximum(m_old, smax + off)
        alpha = jnp.exp2(m_old - m_new)
        p = jnp.exp2(s - (m_new - off))
        l_refs[h][...] = alpha * l_refs[h][...] + jnp.sum(p, axis=0, keepdims=True)
        m_refs[h][...] = m_new
        r = jnp.dot(vt_ref[0, h, j], p.astype(BF16), preferred_element_type=F32)
        acc_refs[h][...] = alpha * acc_refs[h][...] + r

    def sweep(j, diag, off, lookahead):
        for h in range(N_HEADS):
            nh = h + QK_LOOKAHEAD
            if nh < N_HEADS:
                qk(nh, j)
            elif lookahead:
                qk(nh - N_HEADS, j + 1)
            softmax_pv(h, j, diag, 0.0 if off is None else off[h])

    for h in range(QK_LOOKAHEAD):
        qk(h, 0)

    def far_body(j, carry):
        sweep(j, None, far_bias, True)
        return carry

    lax.fori_loop(0, qi - 1, far_body, 0)

    @pl.when(qi >= 1)
    def _():
        sweep(qi - 1, False, None, True)

    sweep(qi, True, None, False)

    lamv = lamv_ref[...]
    lam = (jnp.exp(jnp.sum(lamv[0:1] * lamv[1:2], axis=1, keepdims=True))
           - jnp.exp(jnp.sum(lamv[2:3] * lamv[3:4], axis=1, keepdims=True))
           + LAMBDA_INIT)
    for h in range(N_HEADS):
        o = acc_refs[h][...] / l_refs[h][...]
        o = (o[:, :t] - lam * o[:, t:]).T
        ms = jnp.mean(o * o, axis=-1, keepdims=True)
        y = o * lax.rsqrt(ms + SUBLN_EPS) * sg_ref[...] * (1.0 - LAMBDA_INIT)
        out_ref[0, :, h * V_DIM:(h + 1) * V_DIM] = y.astype(BF16)


def _diff_attn(rel_bias, qt, k, vt, bias, lamv, subln_g):
    b, _, s = qt.shape
    t = ATTN_TILE
    nq = s // t
    per_head = lambda shape, dtype: [pltpu.VMEM(shape, dtype) for _ in range(N_HEADS)]
    return pl.pallas_call(
        _attn_kernel,
        grid=(b, nq),
        in_specs=[
            pl.BlockSpec(memory_space=pltpu.SMEM),
            pl.BlockSpec((1, ATTN_WIDTH, t), lambda bi, qi: (bi, 0, qi)),
            pl.BlockSpec((1, N_HEADS, s, V_DIM), lambda bi, qi: (bi, 0, 0, 0)),
            pl.BlockSpec((1, N_HEADS, nq, V_DIM, t), lambda bi, qi: (bi, 0, 0, 0, 0)),
            pl.BlockSpec((N_HEADS, 2, BIAS_TILE, BIAS_TILE), lambda bi, qi: (0, 0, 0, 0)),
            pl.BlockSpec((4, HEAD_DIM), lambda bi, qi: (0, 0)),
            pl.BlockSpec((1, V_DIM), lambda bi, qi: (0, 0)),
        ],
        out_specs=pl.BlockSpec((1, t, ATTN_WIDTH), lambda bi, qi: (bi, qi, 0)),
        out_shape=jax.ShapeDtypeStruct((b, s, ATTN_WIDTH), BF16),
        scratch_shapes=(per_head((1, 2 * t), F32) + per_head((1, 2 * t), F32)
                        + per_head((1, 2 * t), F32)
                        + per_head((V_DIM, 2 * t), F32) + per_head((V_DIM, 2 * t), BF16)
                        + per_head((t, 2 * t), F32)),
        compiler_params=pltpu.CompilerParams(
            dimension_semantics=("arbitrary", "arbitrary"),
            vmem_limit_bytes=VMEM_LIMIT_BYTES),
        name="diff_attn",
    )(rel_bias, qt, k, vt, bias, lamv, subln_g)


def _ffn_kernel(x_ref, yp_ref, ya_ref, wo_ref, gf_ref, wg_ref, wu_ref, cw_ref, cb_ref,
                w2_ref, gl_ref, out_ref, carry_ref, gbuf_ref, act_ref):
    si = pl.program_id(1)
    tm = x_ref.shape[1]
    ck = FFN_CHUNK

    y = (jnp.dot(yp_ref[0], wo_ref[0:POOL_WIDTH, :], preferred_element_type=F32)
         + jnp.dot(ya_ref[0], wo_ref[POOL_WIDTH:, :], preferred_element_type=F32))
    x1 = x_ref[0] + y
    ms = jnp.mean(x1 * x1, axis=-1, keepdims=True)
    hn = (x1 * lax.rsqrt(ms + NORM_EPS) * gf_ref[...]).astype(BF16)

    @pl.when(si == 0)
    def _():
        carry_ref[...] = jnp.zeros(carry_ref.shape, F32)

    for c in range(D_FF // ck):
        lo, hi = c * ck, (c + 1) * ck
        g = jnp.dot(hn, wg_ref[:, lo:hi], preferred_element_type=F32)
        u = jnp.dot(hn, wu_ref[:, lo:hi], preferred_element_type=F32)
        gbuf_ref[0:CONV_HALO, :] = carry_ref[:, lo:hi]
        gbuf_ref[CONV_HALO:, :] = g
        g1 = gbuf_ref[pl.ds(CONV_HALO - 1, tm), :]
        g2 = gbuf_ref[pl.ds(CONV_HALO - 2, tm), :]
        carry_ref[:, lo:hi] = g[tm - CONV_HALO:, :]
        cv = (cw_ref[0:1, lo:hi] * g2 + cw_ref[1:2, lo:hi] * g1 + cw_ref[2:3, lo:hi] * g
              + cb_ref[:, lo:hi])
        a = cv * (1.0 / (1.0 + jnp.exp(-cv))) * u
        act_ref[:, lo:hi] = a.astype(BF16)

    x2 = x1 + jnp.dot(act_ref[...], w2_ref[...], preferred_element_type=F32)
    ms2 = jnp.mean(x2 * x2, axis=-1, keepdims=True)
    out_ref[0] = x2 * lax.rsqrt(ms2 + NORM_EPS) * gl_ref[...]


def _out_ffn(x, y_pool, y_attn, w_out, g_ffn, wg, wu, conv_w, conv_b, w2, g_final):
    b, s, d = x.shape
    tm = FFN_ROWS
    const = lambda bi, si: (0, 0)
    resident = functools.partial(pl.BlockSpec, index_map=const, pipeline_mode=pl.Buffered(1))
    return pl.pallas_call(
        _ffn_kernel,
        grid=(b, s // tm),
        in_specs=[
            pl.BlockSpec((1, tm, d), lambda bi, si: (bi, si, 0)),
            pl.BlockSpec((1, tm, POOL_WIDTH), lambda bi, si: (bi, si, 0)),
            pl.BlockSpec((1, tm, ATTN_WIDTH), lambda bi, si: (bi, si, 0)),
            resident((d, d)),
            resident((1, d)),
            resident((d, D_FF)),
            resident((d, D_FF)),
            resident((3, D_FF)),
            resident((1, D_FF)),
            resident((D_FF, d)),
            resident((1, d)),
        ],
        out_specs=pl.BlockSpec((1, tm, d), lambda bi, si: (bi, si, 0)),
        out_shape=jax.ShapeDtypeStruct((b, s, d), F32),
        scratch_shapes=[pltpu.VMEM((CONV_HALO, D_FF), F32),
                        pltpu.VMEM((CONV_HALO + tm, FFN_CHUNK), F32),
                        pltpu.VMEM((tm, D_FF), BF16)],
        compiler_params=pltpu.CompilerParams(
            dimension_semantics=("arbitrary", "arbitrary"),
            vmem_limit_bytes=VMEM_LIMIT_BYTES),
        name="out_ffn",
    )(x, y_pool, y_attn, w_out, g_ffn, wg, wu, conv_w, conv_b, w2, g_final)


def kernel(x, norm_mix_g, w_in, pool_w, pool_scale, lambda_q1, lambda_k1, lambda_q2, lambda_k2,
           subln_g, rel_bias, w_out, norm_ffn_g, ffn_w_in, ffn_conv_w, ffn_conv_b, ffn_w_out,
           norm_final_g):
    bias = _bias_tiles(rel_bias)
    y_pool, qt, k, vt = _in_proj(x, norm_mix_g, w_in[0].astype(BF16), pool_w[0].astype(BF16),
                                 pool_scale)
    lamv = jnp.concatenate([lambda_q1, lambda_k1, lambda_q2, lambda_k2], axis=0)
    y_attn = _diff_attn(rel_bias, qt, k, vt, bias, lamv, subln_g)
    wf = ffn_w_in[0]
    return _out_ffn(x, y_pool, y_attn, w_out[0].astype(BF16), norm_ffn_g,
                    wf[:, :D_FF].astype(BF16), wf[:, D_FF:].astype(BF16),
                    ffn_conv_w[0], ffn_conv_b, ffn_w_out[0].astype(BF16),
                    norm_final_g.reshape(1, D_MODEL))
```

```python
import functools
import math

import numpy as np
import jax
import jax.numpy as jnp
from jax import lax
from jax.experimental import pallas as pl
from jax.experimental.pallas import tpu as pltpu

D_MODEL = 1024
POOL_WIDTH = 512
POOL_GROUPS = 4
POOL_GROUP_DIM = 128
POOL_WINDOWS = (2, 4, 8, 16)
POOL_HALO = 16
ATTN_WIDTH = 512
N_HEADS = 4
HEAD_DIM = 64
V_DIM = 128
IN_PROJ_WIDTH = 2048
NUM_BUCKETS = 32
MAX_DISTANCE = 128
D_FF = 2816
NORM_EPS = 1e-6
SUBLN_EPS = 1e-5
NEG_INF = -1e30
LAMBDA_INIT = 0.8 - 0.6 * math.exp(-0.3 * 0)
LOG2E = math.log2(math.e)

ATTN_TILE = 256
BIAS_TILE = 256
QK_LOOKAHEAD = 2
IN_ROWS = 512
FFN_ROWS = 512
FFN_CHUNK = 256
CONV_HALO = 8
VMEM_LIMIT_BYTES = 56 * 1024 * 1024

F32 = jnp.float32
BF16 = jnp.bfloat16


def _bucket_tiles(tile):
    k = np.arange(tile)[:, None]
    q = np.arange(tile)[None, :]
    out = []
    for offset in (0, tile):
        n = q - k + offset
        max_exact = NUM_BUCKETS // 2
        nf = np.maximum(n, 1).astype(np.float64)
        large = max_exact + (np.log(nf / max_exact) / math.log(MAX_DISTANCE / max_exact)
                             * (NUM_BUCKETS - max_exact)).astype(np.int64)
        large = np.minimum(large, NUM_BUCKETS - 1)
        b = np.where(n < max_exact, n, large)
        out.append(np.where(n < 0, -1, b))
    return np.stack(out).astype(np.int32)


def _bias_kernel(rb_ref, bkt_ref, out_ref):
    h = pl.program_id(0)
    bkt = bkt_ref[...]

    def body(i, acc):
        return jnp.where(bkt == i, rb_ref[i, h] * LOG2E, acc)

    out_ref[0] = lax.fori_loop(0, NUM_BUCKETS, body, jnp.full(bkt.shape, NEG_INF, F32))


def _bias_tiles(rel_bias):
    t = BIAS_TILE
    bkt = jnp.asarray(_bucket_tiles(t))
    return pl.pallas_call(
        _bias_kernel,
        grid=(N_HEADS,),
        in_specs=[pl.BlockSpec(memory_space=pltpu.SMEM),
                  pl.BlockSpec((2, t, t), lambda h: (0, 0, 0))],
        out_specs=pl.BlockSpec((1, 2, t, t), lambda h: (h, 0, 0, 0)),
        out_shape=jax.ShapeDtypeStruct((N_HEADS, 2, t, t), F32),
        name="bias_tiles",
    )(rel_bias, bkt)


def _in_proj_kernel(x_ref, g_ref, w_ref, pw_ref, ps_ref,
                    ypool_ref, qt_ref, k_ref, vt_ref, ext_ref):
    si = pl.program_id(1)
    tm = x_ref.shape[1]

    @pl.when(si == 0)
    def _():
        ext_ref[0:POOL_HALO, :] = jnp.zeros((POOL_HALO, POOL_WIDTH), F32)

    x = x_ref[0]
    ms = jnp.mean(x * x, axis=-1, keepdims=True)
    hn = (x * lax.rsqrt(ms + NORM_EPS) * g_ref[...]).astype(BF16)
    zp = jnp.dot(hn, w_ref[:, :POOL_WIDTH], preferred_element_type=F32)
    z = jnp.dot(hn, w_ref[:, POOL_WIDTH:], preferred_element_type=F32)

    ext_ref[POOL_HALO:, :] = zp
    tpos = si * tm + lax.broadcasted_iota(jnp.int32, (tm, 1), 0) + 1
    for gi, w in enumerate(POOL_WINDOWS):
        lo, hi = gi * POOL_GROUP_DIM, (gi + 1) * POOL_GROUP_DIM
        tot = ext_ref[:, lo:hi]
        step = 1
        while step < w:
            tot = tot + pltpu.roll(tot, step, axis=0)
            step *= 2
        tot = tot[POOL_HALO:, :]
        cnt = jnp.minimum(tpos, w).astype(F32)
        pooled = tot / cnt - zp[:, lo:hi]
        yg = jnp.dot(pooled.astype(BF16), pw_ref[gi], preferred_element_type=F32)
        ypool_ref[0, :, lo:hi] = (yg * ps_ref[:, lo:hi]).astype(BF16)
    ext_ref[0:POOL_HALO, :] = zp[tm - POOL_HALO:, :]

    t = ATTN_TILE
    q0, k0, v0 = 0, ATTN_WIDTH, 2 * ATTN_WIDTH
    scale = LOG2E * HEAD_DIM ** -0.5
    for h in range(N_HEADS):
        zq = z[:, q0 + h * V_DIM:q0 + (h + 1) * V_DIM] * scale
        qt_ref[0, h * V_DIM:(h + 1) * V_DIM, :] = zq.T.astype(BF16)
        k_ref[0, h] = z[:, k0 + h * V_DIM:k0 + (h + 1) * V_DIM].astype(BF16)
        vt = z[:, v0 + h * V_DIM:v0 + (h + 1) * V_DIM].T.astype(BF16)
        for j in range(tm // t):
            vt_ref[0, h, j] = vt[:, j * t:(j + 1) * t]


def _in_proj(x, g, w_in, pool_w, pool_scale):
    b, s, d = x.shape
    tm, t = IN_ROWS, ATTN_TILE
    return pl.pallas_call(
        _in_proj_kernel,
        grid=(b, s // tm),
        in_specs=[
            pl.BlockSpec((1, tm, d), lambda bi, si: (bi, si, 0)),
            pl.BlockSpec((1, d), lambda bi, si: (0, 0)),
            pl.BlockSpec((d, IN_PROJ_WIDTH), lambda bi, si: (0, 0)),
            pl.BlockSpec((POOL_GROUPS, POOL_GROUP_DIM, POOL_GROUP_DIM), lambda bi, si: (0, 0, 0)),
            pl.BlockSpec((1, POOL_WIDTH), lambda bi, si: (0, 0)),
        ],
        out_specs=[
            pl.BlockSpec((1, tm, POOL_WIDTH), lambda bi, si: (bi, si, 0)),
            pl.BlockSpec((1, ATTN_WIDTH, tm), lambda bi, si: (bi, 0, si)),
            pl.BlockSpec((1, N_HEADS, tm, V_DIM), lambda bi, si: (bi, 0, si, 0)),
            pl.BlockSpec((1, N_HEADS, tm // t, V_DIM, t), lambda bi, si: (bi, 0, si, 0, 0)),
        ],
        out_shape=[
            jax.ShapeDtypeStruct((b, s, POOL_WIDTH), BF16),
            jax.ShapeDtypeStruct((b, ATTN_WIDTH, s), BF16),
            jax.ShapeDtypeStruct((b, N_HEADS, s, V_DIM), BF16),
            jax.ShapeDtypeStruct((b, N_HEADS, s // t, V_DIM, t), BF16),
        ],
        scratch_shapes=[pltpu.VMEM((POOL_HALO + tm, POOL_WIDTH), F32)],
        compiler_params=pltpu.CompilerParams(
            dimension_semantics=("arbitrary", "arbitrary"),
            vmem_limit_bytes=VMEM_LIMIT_BYTES),
        name="in_proj",
    )(x, g, w_in, pool_w, pool_scale)


def _attn_kernel(rb_ref, qt_ref, k_ref, vt_ref, bias_ref, lamv_ref, sg_ref, out_ref, *scratch):
    qi = pl.program_id(1)
    t = ATTN_TILE
    m_refs, l_refs, smax_refs, acc_refs, qs_refs, s_refs = (
        scratch[i * N_HEADS:(i + 1) * N_HEADS] for i in range(6))

    for h in range(N_HEADS):
        qt = qt_ref[0, h * V_DIM:(h + 1) * V_DIM, :]
        row = lax.broadcasted_iota(jnp.int32, qt.shape, 0)
        zero = jnp.zeros_like(qt)
        qs_refs[h][:, 0:t] = jnp.where(row < HEAD_DIM, qt, zero)
        qs_refs[h][:, t:2 * t] = jnp.where(row >= HEAD_DIM, qt, zero)

        m_refs[h][...] = jnp.full(m_refs[h].shape, -jnp.inf, F32)
        l_refs[h][...] = jnp.zeros(l_refs[h].shape, F32)
        acc_refs[h][...] = jnp.zeros(acc_refs[h].shape, F32)

    far_bias = [rb_ref[NUM_BUCKETS - 1, h] * LOG2E for h in range(N_HEADS)]

    def qk(h, j, kind=None):
        kj = k_ref[0, h, pl.ds(pl.multiple_of(j * t, t), t), :]
        s = jnp.dot(kj, qs_refs[h][...], preferred_element_type=F32)
        if kind is not None:
            s = with_bias(s, h, kind)
        s_refs[h][...] = s
        smax_refs[h][...] = jnp.max(s, axis=0, keepdims=True)

    def with_bias(s, h, diag):
        b = tile_bias(h, diag)
        return s + jnp.concatenate([b, b], axis=1)

    def tile_bias(h, diag):
        n = t // BIAS_TILE
        pieces = {0: bias_ref[h, 0], 1: bias_ref[h, 1]}
        far = jnp.full((BIAS_TILE, BIAS_TILE), far_bias[h], F32)
        masked = jnp.full((BIAS_TILE, BIAS_TILE), NEG_INF, F32)
        rows = []
        for kb in range(n):
            d = [qb - kb + (0 if diag else n) for qb in range(n)]
            rows.append(jnp.concatenate([masked if x < 0 else pieces.get(x, far) for x in d], axis=1))
        return jnp.concatenate(rows, axis=0)

    def softmax_pv(h, j, pending_bias, off):
        s = s_refs[h][...]
        if pending_bias is None:
            smax = smax_refs[h][...]
        else:
            s = with_bias(s, h, pending_bias)
            smax = jnp.max(s, axis=0, keepdims=True)
        m_old = m_refs[h][...]
        m_new = jnp.maximum(m_old, smax + off)
        alpha = jnp.exp2(m_old - m_new)
        p = jnp.exp2(s - (m_new - off))
        l_refs[h][...] = alpha * l_refs[h][...] + jnp.sum(p, axis=0, keepdims=True)
        m_refs[h][...] = m_new
        r = jnp.dot(vt_ref[0, h, j], p.astype(BF16), preferred_element_type=F32)
        acc_refs[h][...] = alpha * acc_refs[h][...] + r

    def run(tiles, next_j):
        stages = [(h, j, kind) for (j, kind) in tiles for h in range(N_HEADS)]
        for i, (h, j, kind) in enumerate(stages):
            ahead = i + QK_LOOKAHEAD
            if ahead < len(stages):
                qk(*stages[ahead])
            elif next_j is not None:
                qk(ahead - len(stages), next_j)
            pending = kind if i < QK_LOOKAHEAD else None
            softmax_pv(h, j, pending, far_bias[h] if kind is None else 0.0)

    for h in range(QK_LOOKAHEAD):
        qk(h, 0)

    n_far = qi - 1

    def pair_body(i, carry):
        j = 2 * i
        run([(j, None), (j + 1, None)], j + 2)
        return carry

    lax.fori_loop(0, jnp.maximum(n_far, 0) // 2, pair_body, 0)

    @pl.when(qi == 0)
    def _():
        run([(qi, True)], None)

    @pl.when((qi >= 1) & (n_far % 2 == 0))
    def _():
        run([(qi - 1, False), (qi, True)], None)

    @pl.when((qi >= 1) & (n_far % 2 == 1))
    def _():
        run([(qi - 2, None), (qi - 1, False), (qi, True)], None)

    lamv = lamv_ref[...]
    lam = (jnp.exp(jnp.sum(lamv[0:1] * lamv[1:2], axis=1, keepdims=True))
           - jnp.exp(jnp.sum(lamv[2:3] * lamv[3:4], axis=1, keepdims=True))
           + LAMBDA_INIT)
    for h in range(N_HEADS):
        o = acc_refs[h][...] / l_refs[h][...]
        o = (o[:, :t] - lam * o[:, t:]).T
        ms = jnp.mean(o * o, axis=-1, keepdims=True)
        y = o * lax.rsqrt(ms + SUBLN_EPS) * sg_ref[...] * (1.0 - LAMBDA_INIT)
        out_ref[0, :, h * V_DIM:(h + 1) * V_DIM] = y.astype(BF16)


def _diff_attn(rel_bias, qt, k, vt, bias, lamv, subln_g):
    b, _, s = qt.shape
    t = ATTN_TILE
    nq = s // t
    per_head = lambda shape, dtype: [pltpu.VMEM(shape, dtype) for _ in range(N_HEADS)]
    return pl.pallas_call(
        _attn_kernel,
        grid=(b, nq),
        in_specs=[
            pl.BlockSpec(memory_space=pltpu.SMEM),
            pl.BlockSpec((1, ATTN_WIDTH, t), lambda bi, qi: (bi, 0, qi)),
            pl.BlockSpec((1, N_HEADS, s, V_DIM), lambda bi, qi: (bi, 0, 0, 0)),
            pl.BlockSpec((1, N_HEADS, nq, V_DIM, t), lambda bi, qi: (bi, 0, 0, 0, 0)),
            pl.BlockSpec((N_HEADS, 2, BIAS_TILE, BIAS_TILE), lambda bi, qi: (0, 0, 0, 0)),
            pl.BlockSpec((4, HEAD_DIM), lambda bi, qi: (0, 0)),
            pl.BlockSpec((1, V_DIM), lambda bi, qi: (0, 0)),
        ],
        out_specs=pl.BlockSpec((1, t, ATTN_WIDTH), lambda bi, qi: (bi, qi, 0)),
        out_shape=jax.ShapeDtypeStruct((b, s, ATTN_WIDTH), BF16),
        scratch_shapes=(per_head((1, 2 * t), F32) + per_head((1, 2 * t), F32)
                        + per_head((1, 2 * t), F32)
                        + per_head((V_DIM, 2 * t), F32) + per_head((V_DIM, 2 * t), BF16)
                        + per_head((t, 2 * t), F32)),
        compiler_params=pltpu.CompilerParams(
            dimension_semantics=("arbitrary", "arbitrary"),
            vmem_limit_bytes=VMEM_LIMIT_BYTES),
        name="diff_attn",
    )(rel_bias, qt, k, vt, bias, lamv, subln_g)


def _ffn_kernel(x_ref, yp_ref, ya_ref, wo_ref, gf_ref, wg_ref, wu_ref, cw_ref, cb_ref,
                w2_ref, gl_ref, out_ref, carry_ref, gbuf_ref, act_ref):
    si = pl.program_id(1)
    tm = x_ref.shape[1]
    ck = FFN_CHUNK

    y = (jnp.dot(yp_ref[0], wo_ref[0:POOL_WIDTH, :], preferred_element_type=F32)
         + jnp.dot(ya_ref[0], wo_ref[POOL_WIDTH:, :], preferred_element_type=F32))
    x1 = x_ref[0] + y
    ms = jnp.mean(x1 * x1, axis=-1, keepdims=True)
    hn = (x1 * lax.rsqrt(ms + NORM_EPS) * gf_ref[...]).astype(BF16)

    @pl.when(si == 0)
    def _():
        carry_ref[...] = jnp.zeros(carry_ref.shape, F32)

    for c in range(D_FF // ck):
        lo, hi = c * ck, (c + 1) * ck
        g = jnp.dot(hn, wg_ref[:, lo:hi], preferred_element_type=F32)
        u = jnp.dot(hn, wu_ref[:, lo:hi], preferred_element_type=F32)
        gbuf_ref[0:CONV_HALO, :] = carry_ref[:, lo:hi]
        gbuf_ref[CONV_HALO:, :] = g
        g1 = gbuf_ref[pl.ds(CONV_HALO - 1, tm), :]
        g2 = gbuf_ref[pl.ds(CONV_HALO - 2, tm), :]
        carry_ref[:, lo:hi] = g[tm - CONV_HALO:, :]
        cv = (cw_ref[0:1, lo:hi] * g2 + cw_ref[1:2, lo:hi] * g1 + cw_ref[2:3, lo:hi] * g
              + cb_ref[:, lo:hi])
        a = cv * (1.0 / (1.0 + jnp.exp(-cv))) * u
        act_ref[:, lo:hi] = a.astype(BF16)

    x2 = x1 + jnp.dot(act_ref[...], w2_ref[...], preferred_element_type=F32)
    ms2 = jnp.mean(x2 * x2, axis=-1, keepdims=True)
    out_ref[0] = x2 * lax.rsqrt(ms2 + NORM_EPS) * gl_ref[...]


def _out_ffn(x, y_pool, y_attn, w_out, g_ffn, wg, wu, conv_w, conv_b, w2, g_final):
    b, s, d = x.shape
    tm = FFN_ROWS
    const = lambda bi, si: (0, 0)
    resident = functools.partial(pl.BlockSpec, index_map=const, pipeline_mode=pl.Buffered(1))
    return pl.pallas_call(
        _ffn_kernel,
        grid=(b, s // tm),
        in_specs=[
            pl.BlockSpec((1, tm, d), lambda bi, si: (bi, si, 0)),
            pl.BlockSpec((1, tm, POOL_WIDTH), lambda bi, si: (bi, si, 0)),
            pl.BlockSpec((1, tm, ATTN_WIDTH), lambda bi, si: (bi, si, 0)),
            resident((d, d)),
            resident((1, d)),
            resident((d, D_FF)),
            resident((d, D_FF)),
            resident((3, D_FF)),
            resident((1, D_FF)),
            resident((D_FF, d)),
            resident((1, d)),
        ],
        out_specs=pl.BlockSpec((1, tm, d), lambda bi, si: (bi, si, 0)),
        out_shape=jax.ShapeDtypeStruct((b, s, d), F32),
        scratch_shapes=[pltpu.VMEM((CONV_HALO, D_FF), F32),
                        pltpu.VMEM((CONV_HALO + tm, FFN_CHUNK), F32),
                        pltpu.VMEM((tm, D_FF), BF16)],
        compiler_params=pltpu.CompilerParams(
            dimension_semantics=("arbitrary", "arbitrary"),
            vmem_limit_bytes=VMEM_LIMIT_BYTES),
        name="out_ffn",
    )(x, y_pool, y_attn, w_out, g_ffn, wg, wu, conv_w, conv_b, w2, g_final)


def kernel(x, norm_mix_g, w_in, pool_w, pool_scale, lambda_q1, lambda_k1, lambda_q2, lambda_k2,
           subln_g, rel_bias, w_out, norm_ffn_g, ffn_w_in, ffn_conv_w, ffn_conv_b, ffn_w_out,
           norm_final_g):
    bias = _bias_tiles(rel_bias)
    y_pool, qt, k, vt = _in_proj(x, norm_mix_g, w_in[0].astype(BF16), pool_w[0].astype(BF16),
                                 pool_scale)
    lamv = jnp.concatenate([lambda_q1, lambda_k1, lambda_q2, lambda_k2], axis=0)
    y_attn = _diff_attn(rel_bias, qt, k, vt, bias, lamv, subln_g)
    wf = ffn_w_in[0]
    return _out_ffn(x, y_pool, y_attn, w_out[0].astype(BF16), norm_ffn_g,
                    wf[:, :D_FF].astype(BF16), wf[:, D_FF:].astype(BF16),
                    ffn_conv_w[0], ffn_conv_b, ffn_w_out[0].astype(BF16),
                    norm_final_g.reshape(1, D_MODEL))
```

```python
import functools
import math

import numpy as np
import jax
import jax.numpy as jnp
from jax import lax
from jax.experimental import pallas as pl
from jax.experimental.pallas import tpu as pltpu

D_MODEL = 1024
POOL_WIDTH = 512
POOL_GROUPS = 4
POOL_GROUP_DIM = 128
POOL_WINDOWS = (2, 4, 8, 16)
POOL_HALO = 16
ATTN_WIDTH = 512
N_HEADS = 4
HEAD_DIM = 64
V_DIM = 128
IN_PROJ_WIDTH = 2048
NUM_BUCKETS = 32
MAX_DISTANCE = 128
D_FF = 2816
NORM_EPS = 1e-6
SUBLN_EPS = 1e-5
NEG_INF = -1e30
LAMBDA_INIT = 0.8 - 0.6 * math.exp(-0.3 * 0)
LOG2E = math.log2(math.e)

ATTN_TILE = 256
BIAS_TILE = 256
QK_LOOKAHEAD = 2
IN_ROWS = 1024
FFN_ROWS = 1024
FFN_CHUNK = 256
CONV_HALO = 8
VMEM_LIMIT_BYTES = 56 * 1024 * 1024

F32 = jnp.float32
BF16 = jnp.bfloat16


def _bucket_tiles(tile):
    k = np.arange(tile)[:, None]
    q = np.arange(tile)[None, :]
    out = []
    for offset in (0, tile):
        n = q - k + offset
        max_exact = NUM_BUCKETS // 2
        nf = np.maximum(n, 1).astype(np.float64)
        large = max_exact + (np.log(nf / max_exact) / math.log(MAX_DISTANCE / max_exact)
                             * (NUM_BUCKETS - max_exact)).astype(np.int64)
        large = np.minimum(large, NUM_BUCKETS - 1)
        b = np.where(n < max_exact, n, large)
        out.append(np.where(n < 0, -1, b))
    return np.stack(out).astype(np.int32)


def _bias_kernel(rb_ref, bkt_ref, out_ref):
    h = pl.program_id(0)
    bkt = bkt_ref[...]

    def body(i, acc):
        return jnp.where(bkt == i, rb_ref[i, h] * LOG2E, acc)

    out_ref[0] = lax.fori_loop(0, NUM_BUCKETS, body, jnp.full(bkt.shape, NEG_INF, F32))


def _bias_tiles(rel_bias):
    t = BIAS_TILE
    bkt = jnp.asarray(_bucket_tiles(t))
    return pl.pallas_call(
        _bias_kernel,
        grid=(N_HEADS,),
        in_specs=[pl.BlockSpec(memory_space=pltpu.SMEM),
                  pl.BlockSpec((2, t, t), lambda h: (0, 0, 0))],
        out_specs=pl.BlockSpec((1, 2, t, t), lambda h: (h, 0, 0, 0)),
        out_shape=jax.ShapeDtypeStruct((N_HEADS, 2, t, t), F32),
        name="bias_tiles",
    )(rel_bias, bkt)


def _in_proj_kernel(x_ref, g_ref, w_ref, pw_ref, ps_ref,
                    ypool_ref, qt_ref, k_ref, vt_ref, ext_ref):
    si = pl.program_id(1)
    tm = x_ref.shape[1]

    @pl.when(si == 0)
    def _():
        ext_ref[0:POOL_HALO, :] = jnp.zeros((POOL_HALO, POOL_WIDTH), F32)

    x = x_ref[0]
    ms = jnp.mean(x * x, axis=-1, keepdims=True)
    hn = (x * lax.rsqrt(ms + NORM_EPS) * g_ref[...]).astype(BF16)
    zp = jnp.dot(hn, w_ref[:, :POOL_WIDTH], preferred_element_type=F32)
    z = jnp.dot(hn, w_ref[:, POOL_WIDTH:], preferred_element_type=F32)

    ext_ref[POOL_HALO:, :] = zp
    tpos = si * tm + lax.broadcasted_iota(jnp.int32, (tm, 1), 0) + 1
    for gi, w in enumerate(POOL_WINDOWS):
        lo, hi = gi * POOL_GROUP_DIM, (gi + 1) * POOL_GROUP_DIM
        tot = ext_ref[:, lo:hi]
        step = 1
        while step < w:
            tot = tot + pltpu.roll(tot, step, axis=0)
            step *= 2
        tot = tot[POOL_HALO:, :]
        cnt = jnp.minimum(tpos, w).astype(F32)
        pooled = tot / cnt - zp[:, lo:hi]
        yg = jnp.dot(pooled.astype(BF16), pw_ref[gi], preferred_element_type=F32)
        ypool_ref[0, :, lo:hi] = (yg * ps_ref[:, lo:hi]).astype(BF16)
    ext_ref[0:POOL_HALO, :] = zp[tm - POOL_HALO:, :]

    t = ATTN_TILE
    q0, k0, v0 = 0, ATTN_WIDTH, 2 * ATTN_WIDTH
    scale = LOG2E * HEAD_DIM ** -0.5
    for h in range(N_HEADS):
        zq = z[:, q0 + h * V_DIM:q0 + (h + 1) * V_DIM] * scale
        qt_ref[0, h * V_DIM:(h + 1) * V_DIM, :] = zq.T.astype(BF16)
        k_ref[0, h] = z[:, k0 + h * V_DIM:k0 + (h + 1) * V_DIM].astype(BF16)
        vt = z[:, v0 + h * V_DIM:v0 + (h + 1) * V_DIM].T.astype(BF16)
        for j in range(tm // t):
            vt_ref[0, h, j] = vt[:, j * t:(j + 1) * t]


def _in_proj(x, g, w_in, pool_w, pool_scale):
    b, s, d = x.shape
    tm, t = IN_ROWS, ATTN_TILE
    return pl.pallas_call(
        _in_proj_kernel,
        grid=(b, s // tm),
        in_specs=[
            pl.BlockSpec((1, tm, d), lambda bi, si: (bi, si, 0)),
            pl.BlockSpec((1, d), lambda bi, si: (0, 0)),
            pl.BlockSpec((d, IN_PROJ_WIDTH), lambda bi, si: (0, 0)),
            pl.BlockSpec((POOL_GROUPS, POOL_GROUP_DIM, POOL_GROUP_DIM), lambda bi, si: (0, 0, 0)),
            pl.BlockSpec((1, POOL_WIDTH), lambda bi, si: (0, 0)),
        ],
        out_specs=[
            pl.BlockSpec((1, tm, POOL_WIDTH), lambda bi, si: (bi, si, 0)),
            pl.BlockSpec((1, ATTN_WIDTH, tm), lambda bi, si: (bi, 0, si)),
            pl.BlockSpec((1, N_HEADS, tm, V_DIM), lambda bi, si: (bi, 0, si, 0)),
            pl.BlockSpec((1, N_HEADS, tm // t, V_DIM, t), lambda bi, si: (bi, 0, si, 0, 0)),
        ],
        out_shape=[
            jax.ShapeDtypeStruct((b, s, POOL_WIDTH), BF16),
            jax.ShapeDtypeStruct((b, ATTN_WIDTH, s), BF16),
            jax.ShapeDtypeStruct((b, N_HEADS, s, V_DIM), BF16),
            jax.ShapeDtypeStruct((b, N_HEADS, s // t, V_DIM, t), BF16),
        ],
        scratch_shapes=[pltpu.VMEM((POOL_HALO + tm, POOL_WIDTH), F32)],
        compiler_params=pltpu.CompilerParams(
            dimension_semantics=("arbitrary", "arbitrary"),
            vmem_limit_bytes=VMEM_LIMIT_BYTES),
        name="in_proj",
    )(x, g, w_in, pool_w, pool_scale)


def _attn_kernel(rb_ref, qt_ref, k_ref, vt_ref, bias_ref, lamv_ref, sg_ref, out_ref, *scratch):
    qi = pl.program_id(1)
    t = ATTN_TILE
    m_refs, l_refs, smax_refs, acc_refs, qs_refs, s_refs = (
        scratch[i * N_HEADS:(i + 1) * N_HEADS] for i in range(6))

    for h in range(N_HEADS):
        qt = qt_ref[0, h * V_DIM:(h + 1) * V_DIM, :]
        row = lax.broadcasted_iota(jnp.int32, qt.shape, 0)
        zero = jnp.zeros_like(qt)
        qs_refs[h][:, 0:t] = jnp.where(row < HEAD_DIM, qt, zero)
        qs_refs[h][:, t:2 * t] = jnp.where(row >= HEAD_DIM, qt, zero)

        m_refs[h][...] = jnp.full(m_refs[h].shape, -jnp.inf, F32)
        l_refs[h][...] = jnp.zeros(l_refs[h].shape, F32)
        acc_refs[h][...] = jnp.zeros(acc_refs[h].shape, F32)

    far_bias = [rb_ref[NUM_BUCKETS - 1, h] * LOG2E for h in range(N_HEADS)]

    def qk(h, j, kind=None):
        kj = k_ref[0, h, pl.ds(pl.multiple_of(j * t, t), t), :]
        s = jnp.dot(kj, qs_refs[h][...], preferred_element_type=F32)
        if kind is not None:
            s = with_bias(s, h, kind)
        s_refs[h][...] = s
        smax_refs[h][...] = jnp.max(s, axis=0, keepdims=True)

    def with_bias(s, h, diag):
        b = tile_bias(h, diag)
        return s + jnp.concatenate([b, b], axis=1)

    def tile_bias(h, diag):
        n = t // BIAS_TILE
        pieces = {0: bias_ref[h, 0], 1: bias_ref[h, 1]}
        far = jnp.full((BIAS_TILE, BIAS_TILE), far_bias[h], F32)
        masked = jnp.full((BIAS_TILE, BIAS_TILE), NEG_INF, F32)
        rows = []
        for kb in range(n):
            d = [qb - kb + (0 if diag else n) for qb in range(n)]
            rows.append(jnp.concatenate([masked if x < 0 else pieces.get(x, far) for x in d], axis=1))
        return jnp.concatenate(rows, axis=0)

    def softmax_pv(h, j, pending_bias, off):
        s = s_refs[h][...]
        if pending_bias is None:
            smax = smax_refs[h][...]
        else:
            s = with_bias(s, h, pending_bias)
            smax = jnp.max(s, axis=0, keepdims=True)
        m_old = m_refs[h][...]
        m_new = jnp.maximum(m_old, smax + off)
        alpha = jnp.exp2(m_old - m_new)
        p = jnp.exp2(s - (m_new - off))
        l_refs[h][...] = alpha * l_refs[h][...] + jnp.sum(p, axis=0, keepdims=True)
        m_refs[h][...] = m_new
        r = jnp.dot(vt_ref[0, h, j], p.astype(BF16), preferred_element_type=F32)
        acc_refs[h][...] = alpha * acc_refs[h][...] + r

    def run(tiles, next_j):
        stages = [(h, j, kind) for (j, kind) in tiles for h in range(N_HEADS)]
        for i, (h, j, kind) in enumerate(stages):
            ahead = i + QK_LOOKAHEAD
            if ahead < len(stages):
                qk(*stages[ahead])
            elif next_j is not None:
                qk(ahead - len(stages), next_j)
            pending = kind if i < QK_LOOKAHEAD else None
            softmax_pv(h, j, pending, far_bias[h] if kind is None else 0.0)

    for h in range(QK_LOOKAHEAD):
        qk(h, 0)

    n_far = qi - 1

    def pair_body(i, carry):
        j = 2 * i
        run([(j, None), (j + 1, None)], j + 2)
        return carry

    lax.fori_loop(0, jnp.maximum(n_far, 0) // 2, pair_body, 0)

    @pl.when(qi == 0)
    def _():
        run([(qi, True)], None)

    @pl.when((qi >= 1) & (n_far % 2 == 0))
    def _():
        run([(qi - 1, False), (qi, True)], None)

    @pl.when((qi >= 1) & (n_far % 2 == 1))
    def _():
        run([(qi - 2, None), (qi - 1, False), (qi, True)], None)

    lamv = lamv_ref[...]
    lam = (jnp.exp(jnp.sum(lamv[0:1] * lamv[1:2], axis=1, keepdims=True))
           - jnp.exp(jnp.sum(lamv[2:3] * lamv[3:4], axis=1, keepdims=True))
           + LAMBDA_INIT)
    for h in range(N_HEADS):
        o = acc_refs[h][...] / l_refs[h][...]
        o = (o[:, :t] - lam * o[:, t:]).T
        ms = jnp.mean(o * o, axis=-1, keepdims=True)
        y = o * lax.rsqrt(ms + SUBLN_EPS) * sg_ref[...] * (1.0 - LAMBDA_INIT)
        out_ref[0, :, h * V_DIM:(h + 1) * V_DIM] = y.astype(BF16)


def _diff_attn(rel_bias, qt, k, vt, bias, lamv, subln_g):
    b, _, s = qt.shape
    t = ATTN_TILE
    nq = s // t
    per_head = lambda shape, dtype: [pltpu.VMEM(shape, dtype) for _ in range(N_HEADS)]
    return pl.pallas_call(
        _attn_kernel,
        grid=(b, nq),
        in_specs=[
            pl.BlockSpec(memory_space=pltpu.SMEM),
            pl.BlockSpec((1, ATTN_WIDTH, t), lambda bi, qi: (bi, 0, qi)),
            pl.BlockSpec((1, N_HEADS, s, V_DIM), lambda bi, qi: (bi, 0, 0, 0)),
            pl.BlockSpec((1, N_HEADS, nq, V_DIM, t), lambda bi, qi: (bi, 0, 0, 0, 0)),
            pl.BlockSpec((N_HEADS, 2, BIAS_TILE, BIAS_TILE), lambda bi, qi: (0, 0, 0, 0)),
            pl.BlockSpec((4, HEAD_DIM), lambda bi, qi: (0, 0)),
            pl.BlockSpec((1, V_DIM), lambda bi, qi: (0, 0)),
        ],
        out_specs=pl.BlockSpec((1, t, ATTN_WIDTH), lambda bi, qi: (bi, qi, 0)),
        out_shape=jax.ShapeDtypeStruct((b, s, ATTN_WIDTH), BF16),
        scratch_shapes=(per_head((1, 2 * t), F32) + per_head((1, 2 * t), F32)
                        + per_head((1, 2 * t), F32)
                        + per_head((V_DIM, 2 * t), F32) + per_head((V_DIM, 2 * t), BF16)
                        + per_head((t, 2 * t), F32)),
        compiler_params=pltpu.CompilerParams(
            dimension_semantics=("arbitrary", "arbitrary"),
            vmem_limit_bytes=VMEM_LIMIT_BYTES),
        name="diff_attn",
    )(rel_bias, qt, k, vt, bias, lamv, subln_g)


def _ffn_kernel(x_ref, yp_ref, ya_ref, wo_ref, gf_ref, wg_ref, wu_ref, cw_ref, cb_ref,
                w2_ref, gl_ref, out_ref, carry_ref, gbuf_ref, act_ref):
    si = pl.program_id(1)
    tm = x_ref.shape[1]
    ck = FFN_CHUNK

    y = (jnp.dot(yp_ref[0], wo_ref[0:POOL_WIDTH, :], preferred_element_type=F32)
         + jnp.dot(ya_ref[0], wo_ref[POOL_WIDTH:, :], preferred_element_type=F32))
    x1 = x_ref[0] + y
    ms = jnp.mean(x1 * x1, axis=-1, keepdims=True)
    hn = (x1 * lax.rsqrt(ms + NORM_EPS) * gf_ref[...]).astype(BF16)

    @pl.when(si == 0)
    def _():
        carry_ref[...] = jnp.zeros(carry_ref.shape, F32)

    for c in range(D_FF // ck):
        lo, hi = c * ck, (c + 1) * ck
        g = jnp.dot(hn, wg_ref[:, lo:hi], preferred_element_type=F32)
        u = jnp.dot(hn, wu_ref[:, lo:hi], preferred_element_type=F32)
        gbuf_ref[0:CONV_HALO, :] = carry_ref[:, lo:hi]
        gbuf_ref[CONV_HALO:, :] = g
        g1 = gbuf_ref[pl.ds(CONV_HALO - 1, tm), :]
        g2 = gbuf_ref[pl.ds(CONV_HALO - 2, tm), :]
        carry_ref[:, lo:hi] = g[tm - CONV_HALO:, :]
        cv = (cw_ref[0:1, lo:hi] * g2 + cw_ref[1:2, lo:hi] * g1 + cw_ref[2:3, lo:hi] * g
              + cb_ref[:, lo:hi])
        a = cv * (1.0 / (1.0 + jnp.exp(-cv))) * u
        act_ref[:, lo:hi] = a.astype(BF16)

    x2 = x1 + jnp.dot(act_ref[...], w2_ref[...], preferred_element_type=F32)
    ms2 = jnp.mean(x2 * x2, axis=-1, keepdims=True)
    out_ref[0] = x2 * lax.rsqrt(ms2 + NORM_EPS) * gl_ref[...]


def _out_ffn(x, y_pool, y_attn, w_out, g_ffn, wg, wu, conv_w, conv_b, w2, g_final):
    b, s, d = x.shape
    tm = FFN_ROWS
    const = lambda bi, si: (0, 0)
    resident = functools.partial(pl.BlockSpec, index_map=const, pipeline_mode=pl.Buffered(1))
    return pl.pallas_call(
        _ffn_kernel,
        grid=(b, s // tm),
        in_specs=[
            pl.BlockSpec((1, tm, d), lambda bi, si: (bi, si, 0)),
            pl.BlockSpec((1, tm, POOL_WIDTH), lambda bi, si: (bi, si, 0)),
            pl.BlockSpec((1, tm, ATTN_WIDTH), lambda bi, si: (bi, si, 0)),
            resident((d, d)),
            resident((1, d)),
            resident((d, D_FF)),
            resident((d, D_FF)),
            resident((3, D_FF)),
            resident((1, D_FF)),
            resident((D_FF, d)),
            resident((1, d)),
        ],
        out_specs=pl.BlockSpec((1, tm, d), lambda bi, si: (bi, si, 0)),
        out_shape=jax.ShapeDtypeStruct((b, s, d), F32),
        scratch_shapes=[pltpu.VMEM((CONV_HALO, D_FF), F32),
                        pltpu.VMEM((CONV_HALO + tm, FFN_CHUNK), F32),
                        pltpu.VMEM((tm, D_FF), BF16)],
        compiler_params=pltpu.CompilerParams(
            dimension_semantics=("arbitrary", "arbitrary"),
            vmem_limit_bytes=VMEM_LIMIT_BYTES),
        name="out_ffn",
    )(x, y_pool, y_attn, w_out, g_ffn, wg, wu, conv_w, conv_b, w2, g_final)


def kernel(x, norm_mix_g, w_in, pool_w, pool_scale, lambda_q1, lambda_k1, lambda_q2, lambda_k2,
           subln_g, rel_bias, w_out, norm_ffn_g, ffn_w_in, ffn_conv_w, ffn_conv_b, ffn_w_out,
           norm_final_g):
    bias = _bias_tiles(rel_bias)
    y_pool, qt, k, vt = _in_proj(x, norm_mix_g, w_in[0].astype(BF16), pool_w[0].astype(BF16),
                                 pool_scale)
    lamv = jnp.concatenate([lambda_q1, lambda_k1, lambda_q2, lambda_k2], axis=0)
    y_attn = _diff_attn(rel_bias, qt, k, vt, bias, lamv, subln_g)
    wf = ffn_w_in[0]
    return _out_ffn(x, y_pool, y_attn, w_out[0].astype(BF16), norm_ffn_g,
                    wf[:, :D_FF].astype(BF16), wf[:, D_FF:].astype(BF16),
                    ffn_conv_w[0], ffn_conv_b, ffn_w_out[0].astype(BF16),
                    norm_final_g.reshape(1, D_MODEL))
```

```python
import functools
import math

import numpy as np
import jax
import jax.numpy as jnp
from jax import lax
from jax.experimental import pallas as pl
from jax.experimental.pallas import tpu as pltpu

D_MODEL = 1024
POOL_WIDTH = 512
POOL_GROUPS = 4
POOL_GROUP_DIM = 128
POOL_WINDOWS = (2, 4, 8, 16)
POOL_HALO = 16
ATTN_WIDTH = 512
N_HEADS = 4
HEAD_DIM = 64
V_DIM = 128
IN_PROJ_WIDTH = 2048
NUM_BUCKETS = 32
MAX_DISTANCE = 128
D_FF = 2816
NORM_EPS = 1e-6
SUBLN_EPS = 1e-5
NEG_INF = -1e30
LAMBDA_INIT = 0.8 - 0.6 * math.exp(-0.3 * 0)
LOG2E = math.log2(math.e)

ATTN_TILE = 256
BIAS_TILE = 128
QK_LOOKAHEAD = 2
IN_ROWS = 1024
FFN_ROWS = 1024
FFN_CHUNK = 256
CONV_HALO = 8
VMEM_LIMIT_BYTES = 56 * 1024 * 1024

F32 = jnp.float32
BF16 = jnp.bfloat16


def _bucket_tiles(tile):
    k = np.arange(tile)[:, None]
    q = np.arange(tile)[None, :]
    out = []
    for offset in (0, tile):
        n = q - k + offset
        max_exact = NUM_BUCKETS // 2
        nf = np.maximum(n, 1).astype(np.float64)
        large = max_exact + (np.log(nf / max_exact) / math.log(MAX_DISTANCE / max_exact)
                             * (NUM_BUCKETS - max_exact)).astype(np.int64)
        large = np.minimum(large, NUM_BUCKETS - 1)
        b = np.where(n < max_exact, n, large)
        out.append(np.where(n < 0, -1, b))
    return np.stack(out).astype(np.int32)


def _bias_kernel(rb_ref, bkt_ref, out_ref):
    h = pl.program_id(0)
    bkt = bkt_ref[...]

    def body(i, acc):
        return jnp.where(bkt == i, rb_ref[i, h] * LOG2E, acc)

    out_ref[0] = lax.fori_loop(0, NUM_BUCKETS, body, jnp.full(bkt.shape, NEG_INF, F32))


def _bias_tiles(rel_bias):
    t = BIAS_TILE
    bkt = jnp.asarray(_bucket_tiles(t))
    return pl.pallas_call(
        _bias_kernel,
        grid=(N_HEADS,),
        in_specs=[pl.BlockSpec(memory_space=pltpu.SMEM),
                  pl.BlockSpec((2, t, t), lambda h: (0, 0, 0))],
        out_specs=pl.BlockSpec((1, 2, t, t), lambda h: (h, 0, 0, 0)),
        out_shape=jax.ShapeDtypeStruct((N_HEADS, 2, t, t), F32),
        name="bias_tiles",
    )(rel_bias, bkt)


def _in_proj_kernel(x_ref, g_ref, w_ref, pw_ref, ps_ref,
                    ypool_ref, qt_ref, k_ref, vt_ref, ext_ref):
    si = pl.program_id(1)
    tm = x_ref.shape[1]

    @pl.when(si == 0)
    def _():
        ext_ref[0:POOL_HALO, :] = jnp.zeros((POOL_HALO, POOL_WIDTH), F32)

    x = x_ref[0]
    ms = jnp.mean(x * x, axis=-1, keepdims=True)
    hn = (x * lax.rsqrt(ms + NORM_EPS) * g_ref[...]).astype(BF16)
    zp = jnp.dot(hn, w_ref[:, :POOL_WIDTH], preferred_element_type=F32)
    z = jnp.dot(hn, w_ref[:, POOL_WIDTH:], preferred_element_type=F32)

    ext_ref[POOL_HALO:, :] = zp
    tpos = si * tm + lax.broadcasted_iota(jnp.int32, (tm, 1), 0) + 1
    for gi, w in enumerate(POOL_WINDOWS):
        lo, hi = gi * POOL_GROUP_DIM, (gi + 1) * POOL_GROUP_DIM
        tot = ext_ref[:, lo:hi]
        step = 1
        while step < w:
            tot = tot + pltpu.roll(tot, step, axis=0)
            step *= 2
        tot = tot[POOL_HALO:, :]
        cnt = jnp.minimum(tpos, w).astype(F32)
        pooled = tot / cnt - zp[:, lo:hi]
        yg = jnp.dot(pooled.astype(BF16), pw_ref[gi], preferred_element_type=F32)
        ypool_ref[0, :, lo:hi] = (yg * ps_ref[:, lo:hi]).astype(BF16)
    ext_ref[0:POOL_HALO, :] = zp[tm - POOL_HALO:, :]

    t = ATTN_TILE
    q0, k0, v0 = 0, ATTN_WIDTH, 2 * ATTN_WIDTH
    scale = LOG2E * HEAD_DIM ** -0.5
    for h in range(N_HEADS):
        zq = z[:, q0 + h * V_DIM:q0 + (h + 1) * V_DIM] * scale
        qt_ref[0, h * V_DIM:(h + 1) * V_DIM, :] = zq.T.astype(BF16)
        k_ref[0, h] = z[:, k0 + h * V_DIM:k0 + (h + 1) * V_DIM].astype(BF16)
        vt = z[:, v0 + h * V_DIM:v0 + (h + 1) * V_DIM].T.astype(BF16)
        for j in range(tm // t):
            vt_ref[0, h, j] = vt[:, j * t:(j + 1) * t]


def _in_proj(x, g, w_in, pool_w, pool_scale):
    b, s, d = x.shape
    tm, t = IN_ROWS, ATTN_TILE
    return pl.pallas_call(
        _in_proj_kernel,
        grid=(b, s // tm),
        in_specs=[
            pl.BlockSpec((1, tm, d), lambda bi, si: (bi, si, 0)),
            pl.BlockSpec((1, d), lambda bi, si: (0, 0)),
            pl.BlockSpec((d, IN_PROJ_WIDTH), lambda bi, si: (0, 0)),
            pl.BlockSpec((POOL_GROUPS, POOL_GROUP_DIM, POOL_GROUP_DIM), lambda bi, si: (0, 0, 0)),
            pl.BlockSpec((1, POOL_WIDTH), lambda bi, si: (0, 0)),
        ],
        out_specs=[
            pl.BlockSpec((1, tm, POOL_WIDTH), lambda bi, si: (bi, si, 0)),
            pl.BlockSpec((1, ATTN_WIDTH, tm), lambda bi, si: (bi, 0, si)),
            pl.BlockSpec((1, N_HEADS, tm, V_DIM), lambda bi, si: (bi, 0, si, 0)),
            pl.BlockSpec((1, N_HEADS, tm // t, V_DIM, t), lambda bi, si: (bi, 0, si, 0, 0)),
        ],
        out_shape=[
            jax.ShapeDtypeStruct((b, s, POOL_WIDTH), BF16),
            jax.ShapeDtypeStruct((b, ATTN_WIDTH, s), BF16),
            jax.ShapeDtypeStruct((b, N_HEADS, s, V_DIM), BF16),
            jax.ShapeDtypeStruct((b, N_HEADS, s // t, V_DIM, t), BF16),
        ],
        scratch_shapes=[pltpu.VMEM((POOL_HALO + tm, POOL_WIDTH), F32)],
        compiler_params=pltpu.CompilerParams(
            dimension_semantics=("arbitrary", "arbitrary"),
            vmem_limit_bytes=VMEM_LIMIT_BYTES),
        name="in_proj",
    )(x, g, w_in, pool_w, pool_scale)


def _attn_kernel(rb_ref, qt_ref, k_ref, vt_ref, bias_ref, lamv_ref, sg_ref, out_ref, *scratch):
    qi = pl.program_id(1)
    t = ATTN_TILE
    m_refs, l_refs, smax_refs, acc_refs, qs_refs, s_refs = (
        scratch[i * N_HEADS:(i + 1) * N_HEADS] for i in range(6))

    for h in range(N_HEADS):
        qt = qt_ref[0, h * V_DIM:(h + 1) * V_DIM, :]
        row = lax.broadcasted_iota(jnp.int32, qt.shape, 0)
        zero = jnp.zeros_like(qt)
        qs_refs[h][:, 0:t] = jnp.where(row < HEAD_DIM, qt, zero)
        qs_refs[h][:, t:2 * t] = jnp.where(row >= HEAD_DIM, qt, zero)

        m_refs[h][...] = jnp.full(m_refs[h].shape, -jnp.inf, F32)
        l_refs[h][...] = jnp.zeros(l_refs[h].shape, F32)
        acc_refs[h][...] = jnp.zeros(acc_refs[h].shape, F32)

    far_bias = [rb_ref[NUM_BUCKETS - 1, h] * LOG2E for h in range(N_HEADS)]

    lamv = lamv_ref[...]
    lam = (jnp.exp(jnp.sum(lamv[0:1] * lamv[1:2], axis=1, keepdims=True))
           - jnp.exp(jnp.sum(lamv[2:3] * lamv[3:4], axis=1, keepdims=True))
           + LAMBDA_INIT)

    def qk(h, j, kind=None):
        kj = k_ref[0, h, pl.ds(pl.multiple_of(j * t, t), t), :]
        s = jnp.dot(kj, qs_refs[h][...], preferred_element_type=F32)
        if kind is not None:
            s = with_bias(s, h, kind)
        s_refs[h][...] = s
        smax_refs[h][...] = jnp.max(s, axis=0, keepdims=True)

    def with_bias(s, h, diag):
        b = tile_bias(h, diag)
        return s + jnp.concatenate([b, b], axis=1)

    def tile_bias(h, diag):
        n = t // BIAS_TILE
        pieces = {0: bias_ref[h, 0], 1: bias_ref[h, 1]}
        far = jnp.full((BIAS_TILE, BIAS_TILE), far_bias[h], F32)
        masked = jnp.full((BIAS_TILE, BIAS_TILE), NEG_INF, F32)
        rows = []
        for kb in range(n):
            d = [qb - kb + (0 if diag else n) for qb in range(n)]
            rows.append(jnp.concatenate([masked if x < 0 else pieces.get(x, far) for x in d], axis=1))
        return jnp.concatenate(rows, axis=0)

    def softmax_pv(h, j, pending_bias, off):
        s = s_refs[h][...]
        if pending_bias is None:
            smax = smax_refs[h][...]
        else:
            s = with_bias(s, h, pending_bias)
            smax = jnp.max(s, axis=0, keepdims=True)
        m_old = m_refs[h][...]
        m_new = jnp.maximum(m_old, smax + off)
        alpha = jnp.exp2(m_old - m_new)
        p = jnp.exp2(s - (m_new - off))
        l_refs[h][...] = alpha * l_refs[h][...] + jnp.sum(p, axis=0, keepdims=True)
        m_refs[h][...] = m_new
        r = jnp.dot(vt_ref[0, h, j], p.astype(BF16), preferred_element_type=F32)
        acc_refs[h][...] = alpha * acc_refs[h][...] + r

    def run(tiles, next_j):
        stages = [(h, j, kind) for (j, kind) in tiles for h in range(N_HEADS)]
        for i, (h, j, kind) in enumerate(stages):
            ahead = i + QK_LOOKAHEAD
            if ahead < len(stages):
                qk(*stages[ahead])
            elif next_j is not None:
                qk(ahead - len(stages), next_j)
            pending = kind if i < QK_LOOKAHEAD else None
            softmax_pv(h, j, pending, far_bias[h] if kind is None else 0.0)
            if kind is True:
                finalize(h)

    def finalize(h):
        o = acc_refs[h][...] * (1.0 / l_refs[h][...])
        o = o[:, :t] - lam * o[:, t:]
        ms = jnp.mean(o * o, axis=0, keepdims=True)
        y = (o * lax.rsqrt(ms + SUBLN_EPS)).T * (sg_ref[...] * (1.0 - LAMBDA_INIT))
        out_ref[0, :, h * V_DIM:(h + 1) * V_DIM] = y.astype(BF16)

    for h in range(QK_LOOKAHEAD):
        qk(h, 0)

    n_far = qi - 1

    def pair_body(i, carry):
        j = 2 * i
        run([(j, None), (j + 1, None)], j + 2)
        return carry

    lax.fori_loop(0, jnp.maximum(n_far, 0) // 2, pair_body, 0)

    @pl.when(qi == 0)
    def _():
        run([(qi, True)], None)

    @pl.when((qi >= 1) & (n_far % 2 == 0))
    def _():
        run([(qi - 1, False), (qi, True)], None)

    @pl.when((qi >= 1) & (n_far % 2 == 1))
    def _():
        run([(qi - 2, None), (qi - 1, False), (qi, True)], None)


def _diff_attn(rel_bias, qt, k, vt, bias, lamv, subln_g):
    b, _, s = qt.shape
    t = ATTN_TILE
    nq = s // t
    per_head = lambda shape, dtype: [pltpu.VMEM(shape, dtype) for _ in range(N_HEADS)]
    return pl.pallas_call(
        _attn_kernel,
        grid=(b, nq),
        in_specs=[
            pl.BlockSpec(memory_space=pltpu.SMEM),
            pl.BlockSpec((1, ATTN_WIDTH, t), lambda bi, qi: (bi, 0, qi)),
            pl.BlockSpec((1, N_HEADS, s, V_DIM), lambda bi, qi: (bi, 0, 0, 0)),
            pl.BlockSpec((1, N_HEADS, nq, V_DIM, t), lambda bi, qi: (bi, 0, 0, 0, 0)),
            pl.BlockSpec((N_HEADS, 2, BIAS_TILE, BIAS_TILE), lambda bi, qi: (0, 0, 0, 0)),
            pl.BlockSpec((4, HEAD_DIM), lambda bi, qi: (0, 0)),
            pl.BlockSpec((1, V_DIM), lambda bi, qi: (0, 0)),
        ],
        out_specs=pl.BlockSpec((1, t, ATTN_WIDTH), lambda bi, qi: (bi, qi, 0)),
        out_shape=jax.ShapeDtypeStruct((b, s, ATTN_WIDTH), BF16),
        scratch_shapes=(per_head((1, 2 * t), F32) + per_head((1, 2 * t), F32)
                        + per_head((1, 2 * t), F32)
                        + per_head((V_DIM, 2 * t), F32) + per_head((V_DIM, 2 * t), BF16)
                        + per_head((t, 2 * t), F32)),
        compiler_params=pltpu.CompilerParams(
            dimension_semantics=("arbitrary", "arbitrary"),
            vmem_limit_bytes=VMEM_LIMIT_BYTES),
        name="diff_attn",
    )(rel_bias, qt, k, vt, bias, lamv, subln_g)


def _ffn_kernel(x_ref, yp_ref, ya_ref, wo_ref, gf_ref, wg_ref, wu_ref, cw_ref, cb_ref,
                w2_ref, gl_ref, out_ref, carry_ref, gbuf_ref, act_ref):
    si = pl.program_id(1)
    tm = x_ref.shape[1]
    ck = FFN_CHUNK

    y = (jnp.dot(yp_ref[0], wo_ref[0:POOL_WIDTH, :], preferred_element_type=F32)
         + jnp.dot(ya_ref[0], wo_ref[POOL_WIDTH:, :], preferred_element_type=F32))
    x1 = x_ref[0] + y
    ms = jnp.mean(x1 * x1, axis=-1, keepdims=True)
    hn = (x1 * lax.rsqrt(ms + NORM_EPS) * gf_ref[...]).astype(BF16)

    @pl.when(si == 0)
    def _():
        carry_ref[...] = jnp.zeros(carry_ref.shape, F32)

    for c in range(D_FF // ck):
        lo, hi = c * ck, (c + 1) * ck
        g = jnp.dot(hn, wg_ref[:, lo:hi], preferred_element_type=F32)
        u = jnp.dot(hn, wu_ref[:, lo:hi], preferred_element_type=F32)
        gbuf_ref[0:CONV_HALO, :] = carry_ref[:, lo:hi]
        gbuf_ref[CONV_HALO:, :] = g
        g1 = gbuf_ref[pl.ds(CONV_HALO - 1, tm), :]
        g2 = gbuf_ref[pl.ds(CONV_HALO - 2, tm), :]
        carry_ref[:, lo:hi] = g[tm - CONV_HALO:, :]
        cv = (cw_ref[0:1, lo:hi] * g2 + cw_ref[1:2, lo:hi] * g1 + cw_ref[2:3, lo:hi] * g
              + cb_ref[:, lo:hi])
        a = cv * (1.0 / (1.0 + jnp.exp(-cv))) * u
        act_ref[:, lo:hi] = a.astype(BF16)

    x2 = x1 + jnp.dot(act_ref[...], w2_ref[...], preferred_element_type=F32)
    ms2 = jnp.mean(x2 * x2, axis=-1, keepdims=True)
    out_ref[0] = x2 * lax.rsqrt(ms2 + NORM_EPS) * gl_ref[...]


def _out_ffn(x, y_pool, y_attn, w_out, g_ffn, wg, wu, conv_w, conv_b, w2, g_final):
    b, s, d = x.shape
    tm = FFN_ROWS
    const = lambda bi, si: (0, 0)
    resident = functools.partial(pl.BlockSpec, index_map=const, pipeline_mode=pl.Buffered(1))
    return pl.pallas_call(
        _ffn_kernel,
        grid=(b, s // tm),
        in_specs=[
            pl.BlockSpec((1, tm, d), lambda bi, si: (bi, si, 0)),
            pl.BlockSpec((1, tm, POOL_WIDTH), lambda bi, si: (bi, si, 0)),
            pl.BlockSpec((1, tm, ATTN_WIDTH), lambda bi, si: (bi, si, 0)),
            resident((d, d)),
            resident((1, d)),
            resident((d, D_FF)),
            resident((d, D_FF)),
            resident((3, D_FF)),
            resident((1, D_FF)),
            resident((D_FF, d)),
            resident((1, d)),
        ],
        out_specs=pl.BlockSpec((1, tm, d), lambda bi, si: (bi, si, 0)),
        out_shape=jax.ShapeDtypeStruct((b, s, d), F32),
        scratch_shapes=[pltpu.VMEM((CONV_HALO, D_FF), F32),
                        pltpu.VMEM((CONV_HALO + tm, FFN_CHUNK), F32),
                        pltpu.VMEM((tm, D_FF), BF16)],
        compiler_params=pltpu.CompilerParams(
            dimension_semantics=("arbitrary", "arbitrary"),
            vmem_limit_bytes=VMEM_LIMIT_BYTES),
        name="out_ffn",
    )(x, y_pool, y_attn, w_out, g_ffn, wg, wu, conv_w, conv_b, w2, g_final)


def kernel(x, norm_mix_g, w_in, pool_w, pool_scale, lambda_q1, lambda_k1, lambda_q2, lambda_k2,
           subln_g, rel_bias, w_out, norm_ffn_g, ffn_w_in, ffn_conv_w, ffn_conv_b, ffn_w_out,
           norm_final_g):
    bias = _bias_tiles(rel_bias)
    y_pool, qt, k, vt = _in_proj(x, norm_mix_g, w_in[0].astype(BF16), pool_w[0].astype(BF16),
                                 pool_scale)
    lamv = jnp.concatenate([lambda_q1, lambda_k1, lambda_q2, lambda_k2], axis=0)
    y_attn = _diff_attn(rel_bias, qt, k, vt, bias, lamv, subln_g)
    wf = ffn_w_in[0]
    return _out_ffn(x, y_pool, y_attn, w_out[0].astype(BF16), norm_ffn_g,
                    wf[:, :D_FF].astype(BF16), wf[:, D_FF:].astype(BF16),
                    ffn_conv_w[0], ffn_conv_b, ffn_w_out[0].astype(BF16),
                    norm_final_g.reshape(1, D_MODEL))
```

```python
import functools
import math

import numpy as np
import jax
import jax.numpy as jnp
from jax import lax
from jax.experimental import pallas as pl
from jax.experimental.pallas import tpu as pltpu

D_MODEL = 1024
POOL_WIDTH = 512
POOL_GROUPS = 4
POOL_GROUP_DIM = 128
POOL_WINDOWS = (2, 4, 8, 16)
POOL_HALO = 16
ATTN_WIDTH = 512
N_HEADS = 4
HEAD_DIM = 64
V_DIM = 128
IN_PROJ_WIDTH = 2048
NUM_BUCKETS = 32
MAX_DISTANCE = 128
D_FF = 2816
NORM_EPS = 1e-6
SUBLN_EPS = 1e-5
NEG_INF = -1e30
LAMBDA_INIT = 0.8 - 0.6 * math.exp(-0.3 * 0)
LOG2E = math.log2(math.e)

ATTN_TILE = 256
BIAS_TILE = 128
QK_LOOKAHEAD = 2
IN_ROWS = 1024
FFN_ROWS = 1024
FFN_CHUNK = 256
CONV_HALO = 8
WEIGHT_CAST_ROWS = 128
VMEM_LIMIT_BYTES = 56 * 1024 * 1024

F32 = jnp.float32
BF16 = jnp.bfloat16


def _bucket_tiles(tile):
    k = np.arange(tile)[:, None]
    q = np.arange(tile)[None, :]
    out = []
    for offset in (0, tile):
        n = q - k + offset
        max_exact = NUM_BUCKETS // 2
        nf = np.maximum(n, 1).astype(np.float64)
        large = max_exact + (np.log(nf / max_exact) / math.log(MAX_DISTANCE / max_exact)
                             * (NUM_BUCKETS - max_exact)).astype(np.int64)
        large = np.minimum(large, NUM_BUCKETS - 1)
        b = np.where(n < max_exact, n, large)
        out.append(np.where(n < 0, -1, b))
    return np.stack(out).astype(np.int32)


def _bias_kernel(rb_ref, bkt_ref, out_ref):
    h = pl.program_id(0)
    bkt = bkt_ref[...]

    def body(i, acc):
        return jnp.where(bkt == i, rb_ref[i, h] * LOG2E, acc)

    out_ref[0] = lax.fori_loop(0, NUM_BUCKETS, body, jnp.full(bkt.shape, NEG_INF, F32))


def _bias_tiles(rel_bias):
    t = BIAS_TILE
    bkt = jnp.asarray(_bucket_tiles(t))
    return pl.pallas_call(
        _bias_kernel,
        grid=(N_HEADS,),
        in_specs=[pl.BlockSpec(memory_space=pltpu.SMEM),
                  pl.BlockSpec((2, t, t), lambda h: (0, 0, 0))],
        out_specs=pl.BlockSpec((1, 2, t, t), lambda h: (h, 0, 0, 0)),
        out_shape=jax.ShapeDtypeStruct((N_HEADS, 2, t, t), F32),
        name="bias_tiles",
    )(rel_bias, bkt)


def _in_proj_kernel(x_ref, g_ref, w32_ref, pw_ref, ps_ref,
                    ypool_ref, qt_ref, k_ref, vt_ref, ext_ref, w_ref):
    si = pl.program_id(1)
    tm = x_ref.shape[1]

    @pl.when((pl.program_id(0) == 0) & (si == 0))
    def _():
        for r in range(0, D_MODEL, WEIGHT_CAST_ROWS):
            w_ref[r:r + WEIGHT_CAST_ROWS, :] = w32_ref[r:r + WEIGHT_CAST_ROWS, :].astype(BF16)

    @pl.when(si == 0)
    def _():
        ext_ref[0:POOL_HALO, :] = jnp.zeros((POOL_HALO, POOL_WIDTH), F32)

    x = x_ref[0]
    ms = jnp.mean(x * x, axis=-1, keepdims=True)
    hn = (x * lax.rsqrt(ms + NORM_EPS) * g_ref[...]).astype(BF16)
    zp = jnp.dot(hn, w_ref[:, :POOL_WIDTH], preferred_element_type=F32)
    z = jnp.dot(hn, w_ref[:, POOL_WIDTH:], preferred_element_type=F32)

    ext_ref[POOL_HALO:, :] = zp
    tpos = si * tm + lax.broadcasted_iota(jnp.int32, (tm, 1), 0) + 1
    for gi, w in enumerate(POOL_WINDOWS):
        lo, hi = gi * POOL_GROUP_DIM, (gi + 1) * POOL_GROUP_DIM
        tot = ext_ref[:, lo:hi]
        step = 1
        while step < w:
            tot = tot + pltpu.roll(tot, step, axis=0)
            step *= 2
        tot = tot[POOL_HALO:, :]
        cnt = jnp.minimum(tpos, w).astype(F32)
        pooled = tot / cnt - zp[:, lo:hi]
        yg = jnp.dot(pooled.astype(BF16), pw_ref[gi].astype(BF16), preferred_element_type=F32)
        ypool_ref[0, :, lo:hi] = (yg * ps_ref[:, lo:hi]).astype(BF16)
    ext_ref[0:POOL_HALO, :] = zp[tm - POOL_HALO:, :]

    t = ATTN_TILE
    q0, k0, v0 = 0, ATTN_WIDTH, 2 * ATTN_WIDTH
    scale = LOG2E * HEAD_DIM ** -0.5
    for h in range(N_HEADS):
        zq = z[:, q0 + h * V_DIM:q0 + (h + 1) * V_DIM] * scale
        qt_ref[0, h * V_DIM:(h + 1) * V_DIM, :] = zq.T.astype(BF16)
        k_ref[0, h] = z[:, k0 + h * V_DIM:k0 + (h + 1) * V_DIM].astype(BF16)
        vt = z[:, v0 + h * V_DIM:v0 + (h + 1) * V_DIM].T.astype(BF16)
        for j in range(tm // t):
            vt_ref[0, h, j] = vt[:, j * t:(j + 1) * t]


def _in_proj(x, g, w_in, pool_w, pool_scale):
    b, s, d = x.shape
    tm, t = IN_ROWS, ATTN_TILE
    return pl.pallas_call(
        _in_proj_kernel,
        grid=(b, s // tm),
        in_specs=[
            pl.BlockSpec((1, tm, d), lambda bi, si: (bi, si, 0)),
            pl.BlockSpec((1, d), lambda bi, si: (0, 0)),
            pl.BlockSpec((d, IN_PROJ_WIDTH), lambda bi, si: (0, 0), pipeline_mode=pl.Buffered(1)),
            pl.BlockSpec((POOL_GROUPS, POOL_GROUP_DIM, POOL_GROUP_DIM), lambda bi, si: (0, 0, 0)),
            pl.BlockSpec((1, POOL_WIDTH), lambda bi, si: (0, 0)),
        ],
        out_specs=[
            pl.BlockSpec((1, tm, POOL_WIDTH), lambda bi, si: (bi, si, 0)),
            pl.BlockSpec((1, ATTN_WIDTH, tm), lambda bi, si: (bi, 0, si)),
            pl.BlockSpec((1, N_HEADS, tm, V_DIM), lambda bi, si: (bi, 0, si, 0)),
            pl.BlockSpec((1, N_HEADS, tm // t, V_DIM, t), lambda bi, si: (bi, 0, si, 0, 0)),
        ],
        out_shape=[
            jax.ShapeDtypeStruct((b, s, POOL_WIDTH), BF16),
            jax.ShapeDtypeStruct((b, ATTN_WIDTH, s), BF16),
            jax.ShapeDtypeStruct((b, N_HEADS, s, V_DIM), BF16),
            jax.ShapeDtypeStruct((b, N_HEADS, s // t, V_DIM, t), BF16),
        ],
        scratch_shapes=[pltpu.VMEM((POOL_HALO + tm, POOL_WIDTH), F32),
                        pltpu.VMEM((d, IN_PROJ_WIDTH), BF16)],
        compiler_params=pltpu.CompilerParams(
            dimension_semantics=("arbitrary", "arbitrary"),
            vmem_limit_bytes=VMEM_LIMIT_BYTES),
        name="in_proj",
    )(x, g, w_in, pool_w, pool_scale)


def _attn_kernel(rb_ref, qt_ref, qtn_ref, k_ref, vt_ref, bias_ref, lamv_ref, sg_ref, out_ref,
                 *scratch):
    qi = pl.program_id(1)
    t = ATTN_TILE
    m_refs, l_refs, smax_refs, acc_refs, qs_refs, s_refs = (
        scratch[i * N_HEADS:(i + 1) * N_HEADS] for i in range(6))
    qsn_refs = scratch[6 * N_HEADS:6 * N_HEADS + QK_LOOKAHEAD]

    def load_q(h, src_ref, dst_refs):
        qt = src_ref[0, h * V_DIM:(h + 1) * V_DIM, :]
        row = lax.broadcasted_iota(jnp.int32, qt.shape, 0)
        zero = jnp.zeros_like(qt)
        dst_refs[h][:, 0:t] = jnp.where(row < HEAD_DIM, qt, zero)
        dst_refs[h][:, t:2 * t] = jnp.where(row >= HEAD_DIM, qt, zero)

    far_bias = [rb_ref[NUM_BUCKETS - 1, h] * LOG2E for h in range(N_HEADS)]

    lamv = lamv_ref[...]
    lam = (jnp.exp(jnp.sum(lamv[0:1] * lamv[1:2], axis=1, keepdims=True))
           - jnp.exp(jnp.sum(lamv[2:3] * lamv[3:4], axis=1, keepdims=True))
           + LAMBDA_INIT)

    def qk(h, j, kind=None, q_refs=qs_refs):
        kj = k_ref[0, h, pl.ds(pl.multiple_of(j * t, t), t), :]
        s = jnp.dot(kj, q_refs[h][...], preferred_element_type=F32)
        if kind is not None:
            s = with_bias(s, h, kind)
        s_refs[h][...] = s
        smax_refs[h][...] = jnp.max(s, axis=0, keepdims=True)

    def with_bias(s, h, diag):
        b = tile_bias(h, diag)
        return s + jnp.concatenate([b, b], axis=1)

    def tile_bias(h, diag):
        n = t // BIAS_TILE
        pieces = {0: bias_ref[h, 0], 1: bias_ref[h, 1]}
        far = jnp.full((BIAS_TILE, BIAS_TILE), far_bias[h], F32)
        masked = jnp.full((BIAS_TILE, BIAS_TILE), NEG_INF, F32)
        rows = []
        for kb in range(n):
            d = [qb - kb + (0 if diag else n) for qb in range(n)]
            rows.append(jnp.concatenate([masked if x < 0 else pieces.get(x, far) for x in d], axis=1))
        return jnp.concatenate(rows, axis=0)

    def softmax_pv(h, j, pending_bias, off):
        s = s_refs[h][...]
        if pending_bias is None:
            smax = smax_refs[h][...]
        else:
            s = with_bias(s, h, pending_bias)
            smax = jnp.max(s, axis=0, keepdims=True)
        m_old = m_refs[h][...]
        m_new = jnp.maximum(m_old, smax + off)
        alpha = jnp.exp2(m_old - m_new)
        p = jnp.exp2(s - (m_new - off))
        l_refs[h][...] = alpha * l_refs[h][...] + jnp.sum(p, axis=0, keepdims=True)
        m_refs[h][...] = m_new
        r = jnp.dot(vt_ref[0, h, j], p.astype(BF16), preferred_element_type=F32)
        acc_refs[h][...] = alpha * acc_refs[h][...] + r

    def run(tiles, next_j):
        stages = [(h, j, kind) for (j, kind) in tiles for h in range(N_HEADS)]
        for i, (h, j, kind) in enumerate(stages):
            ahead = i + QK_LOOKAHEAD
            if ahead < len(stages):
                qk(*stages[ahead])
            elif next_j is not None:
                qk(ahead - len(stages), next_j)
            else:
                load_q(ahead - len(stages), qtn_ref, qsn_refs)
                qk(ahead - len(stages), 0, q_refs=qsn_refs)
            pending = kind if i < QK_LOOKAHEAD else None
            softmax_pv(h, j, pending, far_bias[h] if kind is None else 0.0)
            if kind is True:
                finalize(h)

    def finalize(h):
        o = acc_refs[h][...] * (1.0 / l_refs[h][...])
        o = o[:, :t] - lam * o[:, t:]
        ms = jnp.mean(o * o, axis=0, keepdims=True)
        y = (o * lax.rsqrt(ms + SUBLN_EPS)).T * (sg_ref[...] * (1.0 - LAMBDA_INIT))
        out_ref[0, :, h * V_DIM:(h + 1) * V_DIM] = y.astype(BF16)

    @pl.when(qi == 0)
    def _():
        for h in range(QK_LOOKAHEAD):
            load_q(h, qt_ref, qsn_refs)
            qk(h, 0, q_refs=qsn_refs)

    for h in range(QK_LOOKAHEAD):
        qs_refs[h][...] = qsn_refs[h][...]
    for h in range(QK_LOOKAHEAD, N_HEADS):
        load_q(h, qt_ref, qs_refs)
    for h in range(N_HEADS):
        m_refs[h][...] = jnp.full(m_refs[h].shape, -jnp.inf, F32)
        l_refs[h][...] = jnp.zeros(l_refs[h].shape, F32)
        acc_refs[h][...] = jnp.zeros(acc_refs[h].shape, F32)

    n_far = qi - 1

    def pair_body(i, carry):
        j = 2 * i
        run([(j, None), (j + 1, None)], j + 2)
        return carry

    lax.fori_loop(0, jnp.maximum(n_far, 0) // 2, pair_body, 0)

    @pl.when(qi == 0)
    def _():
        run([(qi, True)], None)

    @pl.when((qi >= 1) & (n_far % 2 == 0))
    def _():
        run([(qi - 1, False), (qi, True)], None)

    @pl.when((qi >= 1) & (n_far % 2 == 1))
    def _():
        run([(qi - 2, None), (qi - 1, False), (qi, True)], None)


def _diff_attn(rel_bias, qt, k, vt, bias, lamv, subln_g):
    b, _, s = qt.shape
    t = ATTN_TILE
    nq = s // t
    per_head = lambda shape, dtype: [pltpu.VMEM(shape, dtype) for _ in range(N_HEADS)]
    return pl.pallas_call(
        _attn_kernel,
        grid=(b, nq),
        in_specs=[
            pl.BlockSpec(memory_space=pltpu.SMEM),
            pl.BlockSpec((1, ATTN_WIDTH, t), lambda bi, qi: (bi, 0, qi)),
            pl.BlockSpec((1, QK_LOOKAHEAD * V_DIM, t),
                         lambda bi, qi: (bi, 0, jnp.minimum(qi + 1, nq - 1))),
            pl.BlockSpec((1, N_HEADS, s, V_DIM), lambda bi, qi: (bi, 0, 0, 0)),
            pl.BlockSpec((1, N_HEADS, nq, V_DIM, t), lambda bi, qi: (bi, 0, 0, 0, 0)),
            pl.BlockSpec((N_HEADS, 2, BIAS_TILE, BIAS_TILE), lambda bi, qi: (0, 0, 0, 0)),
            pl.BlockSpec((4, HEAD_DIM), lambda bi, qi: (0, 0)),
            pl.BlockSpec((1, V_DIM), lambda bi, qi: (0, 0)),
        ],
        out_specs=pl.BlockSpec((1, t, ATTN_WIDTH), lambda bi, qi: (bi, qi, 0)),
        out_shape=jax.ShapeDtypeStruct((b, s, ATTN_WIDTH), BF16),
        scratch_shapes=(per_head((1, 2 * t), F32) + per_head((1, 2 * t), F32)
                        + per_head((1, 2 * t), F32)
                        + per_head((V_DIM, 2 * t), F32) + per_head((V_DIM, 2 * t), BF16)
                        + per_head((t, 2 * t), F32)
                        + [pltpu.VMEM((V_DIM, 2 * t), BF16) for _ in range(QK_LOOKAHEAD)]),
        compiler_params=pltpu.CompilerParams(
            dimension_semantics=("arbitrary", "arbitrary"),
            vmem_limit_bytes=VMEM_LIMIT_BYTES),
        name="diff_attn",
    )(rel_bias, qt, qt, k, vt, bias, lamv, subln_g)


def _ffn_kernel(x_ref, yp_ref, ya_ref, wo_ref, gf_ref, wg_ref, wu_ref, cw_ref, cb_ref,
                w2_ref, gl_ref, out_ref, carry_ref, gbuf_ref, act_ref):
    si = pl.program_id(1)
    tm = x_ref.shape[1]
    ck = FFN_CHUNK

    y = (jnp.dot(yp_ref[0], wo_ref[0:POOL_WIDTH, :], preferred_element_type=F32)
         + jnp.dot(ya_ref[0], wo_ref[POOL_WIDTH:, :], preferred_element_type=F32))
    x1 = x_ref[0] + y
    ms = jnp.mean(x1 * x1, axis=-1, keepdims=True)
    hn = (x1 * lax.rsqrt(ms + NORM_EPS) * gf_ref[...]).astype(BF16)

    @pl.when(si == 0)
    def _():
        carry_ref[...] = jnp.zeros(carry_ref.shape, F32)

    for c in range(D_FF // ck):
        lo, hi = c * ck, (c + 1) * ck
        g = jnp.dot(hn, wg_ref[:, lo:hi], preferred_element_type=F32)
        u = jnp.dot(hn, wu_ref[:, lo:hi], preferred_element_type=F32)
        gbuf_ref[0:CONV_HALO, :] = carry_ref[:, lo:hi]
        gbuf_ref[CONV_HALO:, :] = g
        g1 = gbuf_ref[pl.ds(CONV_HALO - 1, tm), :]
        g2 = gbuf_ref[pl.ds(CONV_HALO - 2, tm), :]
        carry_ref[:, lo:hi] = g[tm - CONV_HALO:, :]
        cv = (cw_ref[0:1, lo:hi] * g2 + cw_ref[1:2, lo:hi] * g1 + cw_ref[2:3, lo:hi] * g
              + cb_ref[:, lo:hi])
        a = cv * (1.0 / (1.0 + jnp.exp(-cv))) * u
        act_ref[:, lo:hi] = a.astype(BF16)

    x2 = x1 + jnp.dot(act_ref[...], w2_ref[...], preferred_element_type=F32)
    ms2 = jnp.mean(x2 * x2, axis=-1, keepdims=True)
    out_ref[0] = x2 * lax.rsqrt(ms2 + NORM_EPS) * gl_ref[...]


def _out_ffn(x, y_pool, y_attn, w_out, g_ffn, wg, wu, conv_w, conv_b, w2, g_final):
    b, s, d = x.shape
    tm = FFN_ROWS
    const = lambda bi, si: (0, 0)
    resident = functools.partial(pl.BlockSpec, index_map=const, pipeline_mode=pl.Buffered(1))
    return pl.pallas_call(
        _ffn_kernel,
        grid=(b, s // tm),
        in_specs=[
            pl.BlockSpec((1, tm, d), lambda bi, si: (bi, si, 0)),
            pl.BlockSpec((1, tm, POOL_WIDTH), lambda bi, si: (bi, si, 0)),
            pl.BlockSpec((1, tm, ATTN_WIDTH), lambda bi, si: (bi, si, 0)),
            resident((d, d)),
            resident((1, d)),
            resident((d, D_FF)),
            resident((d, D_FF)),
            resident((3, D_FF)),
            resident((1, D_FF)),
            resident((D_FF, d)),
            resident((1, d)),
        ],
        out_specs=pl.BlockSpec((1, tm, d), lambda bi, si: (bi, si, 0)),
        out_shape=jax.ShapeDtypeStruct((b, s, d), F32),
        scratch_shapes=[pltpu.VMEM((CONV_HALO, D_FF), F32),
                        pltpu.VMEM((CONV_HALO + tm, FFN_CHUNK), F32),
                        pltpu.VMEM((tm, D_FF), BF16)],
        compiler_params=pltpu.CompilerParams(
            dimension_semantics=("arbitrary", "arbitrary"),
            vmem_limit_bytes=VMEM_LIMIT_BYTES),
        name="out_ffn",
    )(x, y_pool, y_attn, w_out, g_ffn, wg, wu, conv_w, conv_b, w2, g_final)


def kernel(x, norm_mix_g, w_in, pool_w, pool_scale, lambda_q1, lambda_k1, lambda_q2, lambda_k2,
           subln_g, rel_bias, w_out, norm_ffn_g, ffn_w_in, ffn_conv_w, ffn_conv_b, ffn_w_out,
           norm_final_g):
    bias = _bias_tiles(rel_bias)
    y_pool, qt, k, vt = _in_proj(x, norm_mix_g, w_in[0], pool_w[0], pool_scale)
    lamv = jnp.concatenate([lambda_q1, lambda_k1, lambda_q2, lambda_k2], axis=0)
    y_attn = _diff_attn(rel_bias, qt, k, vt, bias, lamv, subln_g)
    wf = ffn_w_in[0]
    return _out_ffn(x, y_pool, y_attn, w_out[0].astype(BF16), norm_ffn_g,
                    wf[:, :D_FF].astype(BF16), wf[:, D_FF:].astype(BF16),
                    ffn_conv_w[0], ffn_conv_b, ffn_w_out[0].astype(BF16),
                    norm_final_g.reshape(1, D_MODEL))
```

```python
import functools
import math

import numpy as np
import jax
import jax.numpy as jnp
from jax import lax
from jax.experimental import pallas as pl
from jax.experimental.pallas import tpu as pltpu

D_MODEL = 1024
POOL_WIDTH = 512
POOL_GROUPS = 4
POOL_GROUP_DIM = 128
POOL_WINDOWS = (2, 4, 8, 16)
POOL_HALO = 16
ATTN_WIDTH = 512
N_HEADS = 4
HEAD_DIM = 64
V_DIM = 128
IN_PROJ_WIDTH = 2048
NUM_BUCKETS = 32
MAX_DISTANCE = 128
D_FF = 2816
NORM_EPS = 1e-6
SUBLN_EPS = 1e-5
NEG_INF = -1e30
LAMBDA_INIT = 0.8 - 0.6 * math.exp(-0.3 * 0)
LOG2E = math.log2(math.e)

ATTN_TILE = 256
BIAS_TILE = 128
QK_LOOKAHEAD = 2
FAR_PER_TRIP = 4
IN_ROWS = 1024
FFN_ROWS = 1024
FFN_CHUNK = 256
CONV_HALO = 8
WEIGHT_CAST_ROWS = 128
VMEM_LIMIT_BYTES = 56 * 1024 * 1024

F32 = jnp.float32
BF16 = jnp.bfloat16


def _bucket_tiles(tile):
    k = np.arange(tile)[:, None]
    q = np.arange(tile)[None, :]
    out = []
    for offset in (0, tile):
        n = q - k + offset
        max_exact = NUM_BUCKETS // 2
        nf = np.maximum(n, 1).astype(np.float64)
        large = max_exact + (np.log(nf / max_exact) / math.log(MAX_DISTANCE / max_exact)
                             * (NUM_BUCKETS - max_exact)).astype(np.int64)
        large = np.minimum(large, NUM_BUCKETS - 1)
        b = np.where(n < max_exact, n, large)
        out.append(np.where(n < 0, -1, b))
    return np.stack(out).astype(np.int32)


def _bias_kernel(rb_ref, bkt_ref, out_ref):
    h = pl.program_id(0)
    bkt = bkt_ref[...]

    def body(i, acc):
        return jnp.where(bkt == i, rb_ref[i, h] * LOG2E, acc)

    out_ref[0] = lax.fori_loop(0, NUM_BUCKETS, body, jnp.full(bkt.shape, NEG_INF, F32))


def _bias_tiles(rel_bias):
    t = BIAS_TILE
    bkt = jnp.asarray(_bucket_tiles(t))
    return pl.pallas_call(
        _bias_kernel,
        grid=(N_HEADS,),
        in_specs=[pl.BlockSpec(memory_space=pltpu.SMEM),
                  pl.BlockSpec((2, t, t), lambda h: (0, 0, 0))],
        out_specs=pl.BlockSpec((1, 2, t, t), lambda h: (h, 0, 0, 0)),
        out_shape=jax.ShapeDtypeStruct((N_HEADS, 2, t, t), F32),
        name="bias_tiles",
    )(rel_bias, bkt)


def _in_proj_kernel(x_ref, g_ref, w32_ref, pw_ref, ps_ref,
                    ypool_ref, qt_ref, k_ref, vt_ref, ext_ref, w_ref):
    si = pl.program_id(1)
    tm = x_ref.shape[1]

    @pl.when((pl.program_id(0) == 0) & (si == 0))
    def _():
        for r in range(0, D_MODEL, WEIGHT_CAST_ROWS):
            w_ref[r:r + WEIGHT_CAST_ROWS, :] = w32_ref[r:r + WEIGHT_CAST_ROWS, :].astype(BF16)

    @pl.when(si == 0)
    def _():
        ext_ref[0:POOL_HALO, :] = jnp.zeros((POOL_HALO, POOL_WIDTH), F32)

    x = x_ref[0]
    ms = jnp.mean(x * x, axis=-1, keepdims=True)
    hn = (x * lax.rsqrt(ms + NORM_EPS) * g_ref[...]).astype(BF16)
    zp = jnp.dot(hn, w_ref[:, :POOL_WIDTH], preferred_element_type=F32)
    z = jnp.dot(hn, w_ref[:, POOL_WIDTH:], preferred_element_type=F32)

    ext_ref[POOL_HALO:, :] = zp
    tpos = si * tm + lax.broadcasted_iota(jnp.int32, (tm, 1), 0) + 1
    for gi, w in enumerate(POOL_WINDOWS):
        lo, hi = gi * POOL_GROUP_DIM, (gi + 1) * POOL_GROUP_DIM
        tot = ext_ref[:, lo:hi]
        step = 1
        while step < w:
            tot = tot + pltpu.roll(tot, step, axis=0)
            step *= 2
        tot = tot[POOL_HALO:, :]
        cnt = jnp.minimum(tpos, w).astype(F32)
        pooled = tot / cnt - zp[:, lo:hi]
        yg = jnp.dot(pooled.astype(BF16), pw_ref[gi].astype(BF16), preferred_element_type=F32)
        ypool_ref[0, :, lo:hi] = (yg * ps_ref[:, lo:hi]).astype(BF16)
    ext_ref[0:POOL_HALO, :] = zp[tm - POOL_HALO:, :]

    t = ATTN_TILE
    q0, k0, v0 = 0, ATTN_WIDTH, 2 * ATTN_WIDTH
    scale = LOG2E * HEAD_DIM ** -0.5
    for h in range(N_HEADS):
        zq = z[:, q0 + h * V_DIM:q0 + (h + 1) * V_DIM] * scale
        qt_ref[0, h * V_DIM:(h + 1) * V_DIM, :] = zq.T.astype(BF16)
        k_ref[0, h] = z[:, k0 + h * V_DIM:k0 + (h + 1) * V_DIM].astype(BF16)
        vt = z[:, v0 + h * V_DIM:v0 + (h + 1) * V_DIM].T.astype(BF16)
        for j in range(tm // t):
            vt_ref[0, h, j] = vt[:, j * t:(j + 1) * t]


def _in_proj(x, g, w_in, pool_w, pool_scale):
    b, s, d = x.shape
    tm, t = IN_ROWS, ATTN_TILE
    return pl.pallas_call(
        _in_proj_kernel,
        grid=(b, s // tm),
        in_specs=[
            pl.BlockSpec((1, tm, d), lambda bi, si: (bi, si, 0)),
            pl.BlockSpec((1, d), lambda bi, si: (0, 0)),
            pl.BlockSpec((d, IN_PROJ_WIDTH), lambda bi, si: (0, 0), pipeline_mode=pl.Buffered(1)),
            pl.BlockSpec((POOL_GROUPS, POOL_GROUP_DIM, POOL_GROUP_DIM), lambda bi, si: (0, 0, 0)),
            pl.BlockSpec((1, POOL_WIDTH), lambda bi, si: (0, 0)),
        ],
        out_specs=[
            pl.BlockSpec((1, tm, POOL_WIDTH), lambda bi, si: (bi, si, 0)),
            pl.BlockSpec((1, ATTN_WIDTH, tm), lambda bi, si: (bi, 0, si)),
            pl.BlockSpec((1, N_HEADS, tm, V_DIM), lambda bi, si: (bi, 0, si, 0)),
            pl.BlockSpec((1, N_HEADS, tm // t, V_DIM, t), lambda bi, si: (bi, 0, si, 0, 0)),
        ],
        out_shape=[
            jax.ShapeDtypeStruct((b, s, POOL_WIDTH), BF16),
            jax.ShapeDtypeStruct((b, ATTN_WIDTH, s), BF16),
            jax.ShapeDtypeStruct((b, N_HEADS, s, V_DIM), BF16),
            jax.ShapeDtypeStruct((b, N_HEADS, s // t, V_DIM, t), BF16),
        ],
        scratch_shapes=[pltpu.VMEM((POOL_HALO + tm, POOL_WIDTH), F32),
                        pltpu.VMEM((d, IN_PROJ_WIDTH), BF16)],
        compiler_params=pltpu.CompilerParams(
            dimension_semantics=("arbitrary", "arbitrary"),
            vmem_limit_bytes=VMEM_LIMIT_BYTES),
        name="in_proj",
    )(x, g, w_in, pool_w, pool_scale)


def _attn_kernel(rb_ref, qt_ref, qtn_ref, k_ref, vt_ref, bias_ref, lamv_ref, sg_ref, out_ref,
                 *scratch):
    qi = pl.program_id(1)
    t = ATTN_TILE
    m_refs, l_refs, smax_refs, acc_refs, qs_refs, s_refs = (
        scratch[i * N_HEADS:(i + 1) * N_HEADS] for i in range(6))
    qsn_refs = scratch[6 * N_HEADS:6 * N_HEADS + QK_LOOKAHEAD]

    def load_q(h, src_ref, dst_refs):
        qt = src_ref[0, h * V_DIM:(h + 1) * V_DIM, :]
        row = lax.broadcasted_iota(jnp.int32, qt.shape, 0)
        zero = jnp.zeros_like(qt)
        dst_refs[h][:, 0:t] = jnp.where(row < HEAD_DIM, qt, zero)
        dst_refs[h][:, t:2 * t] = jnp.where(row >= HEAD_DIM, qt, zero)

    far_bias = [rb_ref[NUM_BUCKETS - 1, h] * LOG2E for h in range(N_HEADS)]

    lamv = lamv_ref[...]
    lam = (jnp.exp(jnp.sum(lamv[0:1] * lamv[1:2], axis=1, keepdims=True))
           - jnp.exp(jnp.sum(lamv[2:3] * lamv[3:4], axis=1, keepdims=True))
           + LAMBDA_INIT)

    def qk(h, j, kind=None, q_refs=qs_refs):
        kj = k_ref[0, h, pl.ds(pl.multiple_of(j * t, t), t), :]
        s = jnp.dot(kj, q_refs[h][...], preferred_element_type=F32)
        if kind is not None:
            s = with_bias(s, h, kind)
        s_refs[h][...] = s
        smax_refs[h][...] = jnp.max(s, axis=0, keepdims=True)

    def with_bias(s, h, diag):
        b = tile_bias(h, diag)
        return s + jnp.concatenate([b, b], axis=1)

    def tile_bias(h, diag):
        n = t // BIAS_TILE
        pieces = {0: bias_ref[h, 0], 1: bias_ref[h, 1]}
        far = jnp.full((BIAS_TILE, BIAS_TILE), far_bias[h], F32)
        masked = jnp.full((BIAS_TILE, BIAS_TILE), NEG_INF, F32)
        rows = []
        for kb in range(n):
            d = [qb - kb + (0 if diag else n) for qb in range(n)]
            rows.append(jnp.concatenate([masked if x < 0 else pieces.get(x, far) for x in d], axis=1))
        return jnp.concatenate(rows, axis=0)

    def softmax_pv(h, j, pending_bias, off):
        s = s_refs[h][...]
        if pending_bias is None:
            smax = smax_refs[h][...]
        else:
            s = with_bias(s, h, pending_bias)
            smax = jnp.max(s, axis=0, keepdims=True)
        m_old = m_refs[h][...]
        m_new = jnp.maximum(m_old, smax + off)
        alpha = jnp.exp2(m_old - m_new)
        p = jnp.exp2(s - (m_new - off))
        l_refs[h][...] = alpha * l_refs[h][...] + jnp.sum(p, axis=0, keepdims=True)
        m_refs[h][...] = m_new
        r = jnp.dot(vt_ref[0, h, j], p.astype(BF16), preferred_element_type=F32)
        acc_refs[h][...] = alpha * acc_refs[h][...] + r

    def run(tiles, next_j):
        stages = [(h, j, kind) for (j, kind) in tiles for h in range(N_HEADS)]
        for i, (h, j, kind) in enumerate(stages):
            ahead = i + QK_LOOKAHEAD
            if ahead < len(stages):
                qk(*stages[ahead])
            elif next_j is not None:
                qk(ahead - len(stages), next_j)
            else:
                load_q(ahead - len(stages), qtn_ref, qsn_refs)
                qk(ahead - len(stages), 0, q_refs=qsn_refs)
            pending = kind if i < QK_LOOKAHEAD else None
            softmax_pv(h, j, pending, far_bias[h] if kind is None else 0.0)
            if kind is True:
                finalize(h)

    def finalize(h):
        o = acc_refs[h][...] * (1.0 / l_refs[h][...])
        o = o[:, :t] - lam * o[:, t:]
        ms = jnp.mean(o * o, axis=0, keepdims=True)
        y = (o * lax.rsqrt(ms + SUBLN_EPS)).T * (sg_ref[...] * (1.0 - LAMBDA_INIT))
        out_ref[0, :, h * V_DIM:(h + 1) * V_DIM] = y.astype(BF16)

    @pl.when(qi == 0)
    def _():
        for h in range(QK_LOOKAHEAD):
            load_q(h, qt_ref, qsn_refs)
            qk(h, 0, q_refs=qsn_refs)

    for h in range(QK_LOOKAHEAD):
        qs_refs[h][...] = qsn_refs[h][...]
    for h in range(QK_LOOKAHEAD, N_HEADS):
        load_q(h, qt_ref, qs_refs)
    for h in range(N_HEADS):
        m_refs[h][...] = jnp.full(m_refs[h].shape, -jnp.inf, F32)
        l_refs[h][...] = jnp.zeros(l_refs[h].shape, F32)
        acc_refs[h][...] = jnp.zeros(acc_refs[h].shape, F32)

    n_far = qi - 1

    def far_body(i, carry):
        j = FAR_PER_TRIP * i
        run([(j + d, None) for d in range(FAR_PER_TRIP)], j + FAR_PER_TRIP)
        return carry

    lax.fori_loop(0, jnp.maximum(n_far, 0) // FAR_PER_TRIP, far_body, 0)

    @pl.when(qi == 0)
    def _():
        run([(qi, True)], None)

    for rem in range(FAR_PER_TRIP):
        @pl.when((qi >= 1) & (n_far % FAR_PER_TRIP == rem))
        def _():
            run([(qi - 1 - d, None) for d in range(rem, 0, -1)] + [(qi - 1, False), (qi, True)], None)


def _diff_attn(rel_bias, qt, k, vt, bias, lamv, subln_g):
    b, _, s = qt.shape
    t = ATTN_TILE
    nq = s // t
    per_head = lambda shape, dtype: [pltpu.VMEM(shape, dtype) for _ in range(N_HEADS)]
    return pl.pallas_call(
        _attn_kernel,
        grid=(b, nq),
        in_specs=[
            pl.BlockSpec(memory_space=pltpu.SMEM),
            pl.BlockSpec((1, ATTN_WIDTH, t), lambda bi, qi: (bi, 0, qi)),
            pl.BlockSpec((1, QK_LOOKAHEAD * V_DIM, t),
                         lambda bi, qi: (bi, 0, jnp.minimum(qi + 1, nq - 1))),
            pl.BlockSpec((1, N_HEADS, s, V_DIM), lambda bi, qi: (bi, 0, 0, 0)),
            pl.BlockSpec((1, N_HEADS, nq, V_DIM, t), lambda bi, qi: (bi, 0, 0, 0, 0)),
            pl.BlockSpec((N_HEADS, 2, BIAS_TILE, BIAS_TILE), lambda bi, qi: (0, 0, 0, 0)),
            pl.BlockSpec((4, HEAD_DIM), lambda bi, qi: (0, 0)),
            pl.BlockSpec((1, V_DIM), lambda bi, qi: (0, 0)),
        ],
        out_specs=pl.BlockSpec((1, t, ATTN_WIDTH), lambda bi, qi: (bi, qi, 0)),
        out_shape=jax.ShapeDtypeStruct((b, s, ATTN_WIDTH), BF16),
        scratch_shapes=(per_head((1, 2 * t), F32) + per_head((1, 2 * t), F32)
                        + per_head((1, 2 * t), F32)
                        + per_head((V_DIM, 2 * t), F32) + per_head((V_DIM, 2 * t), BF16)
                        + per_head((t, 2 * t), F32)
                        + [pltpu.VMEM((V_DIM, 2 * t), BF16) for _ in range(QK_LOOKAHEAD)]),
        compiler_params=pltpu.CompilerParams(
            dimension_semantics=("arbitrary", "arbitrary"),
            vmem_limit_bytes=VMEM_LIMIT_BYTES),
        name="diff_attn",
    )(rel_bias, qt, qt, k, vt, bias, lamv, subln_g)


def _ffn_kernel(x_ref, yp_ref, ya_ref, wo_ref, gf_ref, wg_ref, wu_ref, cw_ref, cb_ref,
                w2_ref, gl_ref, out_ref, carry_ref, gbuf_ref, act_ref):
    si = pl.program_id(1)
    tm = x_ref.shape[1]
    ck = FFN_CHUNK

    y = (jnp.dot(yp_ref[0], wo_ref[0:POOL_WIDTH, :], preferred_element_type=F32)
         + jnp.dot(ya_ref[0], wo_ref[POOL_WIDTH:, :], preferred_element_type=F32))
    x1 = x_ref[0] + y
    ms = jnp.mean(x1 * x1, axis=-1, keepdims=True)
    hn = (x1 * lax.rsqrt(ms + NORM_EPS) * gf_ref[...]).astype(BF16)

    @pl.when(si == 0)
    def _():
        carry_ref[...] = jnp.zeros(carry_ref.shape, F32)

    for c in range(D_FF // ck):
        lo, hi = c * ck, (c + 1) * ck
        g = jnp.dot(hn, wg_ref[:, lo:hi], preferred_element_type=F32)
        u = jnp.dot(hn, wu_ref[:, lo:hi], preferred_element_type=F32)
        gbuf_ref[0:CONV_HALO, :] = carry_ref[:, lo:hi]
        gbuf_ref[CONV_HALO:, :] = g
        g1 = gbuf_ref[pl.ds(CONV_HALO - 1, tm), :]
        g2 = gbuf_ref[pl.ds(CONV_HALO - 2, tm), :]
        carry_ref[:, lo:hi] = g[tm - CONV_HALO:, :]
        cv = (cw_ref[0:1, lo:hi] * g2 + cw_ref[1:2, lo:hi] * g1 + cw_ref[2:3, lo:hi] * g
              + cb_ref[:, lo:hi])
        a = cv * (1.0 / (1.0 + jnp.exp(-cv))) * u
        act_ref[:, lo:hi] = a.astype(BF16)

    x2 = x1 + jnp.dot(act_ref[...], w2_ref[...], preferred_element_type=F32)
    ms2 = jnp.mean(x2 * x2, axis=-1, keepdims=True)
    out_ref[0] = x2 * lax.rsqrt(ms2 + NORM_EPS) * gl_ref[...]


def _out_ffn(x, y_pool, y_attn, w_out, g_ffn, wg, wu, conv_w, conv_b, w2, g_final):
    b, s, d = x.shape
    tm = FFN_ROWS
    const = lambda bi, si: (0, 0)
    resident = functools.partial(pl.BlockSpec, index_map=const, pipeline_mode=pl.Buffered(1))
    return pl.pallas_call(
        _ffn_kernel,
        grid=(b, s // tm),
        in_specs=[
            pl.BlockSpec((1, tm, d), lambda bi, si: (bi, si, 0)),
            pl.BlockSpec((1, tm, POOL_WIDTH), lambda bi, si: (bi, si, 0)),
            pl.BlockSpec((1, tm, ATTN_WIDTH), lambda bi, si: (bi, si, 0)),
            resident((d, d)),
            resident((1, d)),
            resident((d, D_FF)),
            resident((d, D_FF)),
            resident((3, D_FF)),
            resident((1, D_FF)),
            resident((D_FF, d)),
            resident((1, d)),
        ],
        out_specs=pl.BlockSpec((1, tm, d), lambda bi, si: (bi, si, 0)),
        out_shape=jax.ShapeDtypeStruct((b, s, d), F32),
        scratch_shapes=[pltpu.VMEM((CONV_HALO, D_FF), F32),
                        pltpu.VMEM((CONV_HALO + tm, FFN_CHUNK), F32),
                        pltpu.VMEM((tm, D_FF), BF16)],
        compiler_params=pltpu.CompilerParams(
            dimension_semantics=("arbitrary", "arbitrary"),
            vmem_limit_bytes=VMEM_LIMIT_BYTES),
        name="out_ffn",
    )(x, y_pool, y_attn, w_out, g_ffn, wg, wu, conv_w, conv_b, w2, g_final)


def kernel(x, norm_mix_g, w_in, pool_w, pool_scale, lambda_q1, lambda_k1, lambda_q2, lambda_k2,
           subln_g, rel_bias, w_out, norm_ffn_g, ffn_w_in, ffn_conv_w, ffn_conv_b, ffn_w_out,
           norm_final_g):
    bias = _bias_tiles(rel_bias)
    y_pool, qt, k, vt = _in_proj(x, norm_mix_g, w_in[0], pool_w[0], pool_scale)
    lamv = jnp.concatenate([lambda_q1, lambda_k1, lambda_q2, lambda_k2], axis=0)
    y_attn = _diff_attn(rel_bias, qt, k, vt, bias, lamv, subln_g)
    wf = ffn_w_in[0]
    return _out_ffn(x, y_pool, y_attn, w_out[0].astype(BF16), norm_ffn_g,
                    wf[:, :D_FF].astype(BF16), wf[:, D_FF:].astype(BF16),
                    ffn_conv_w[0], ffn_conv_b, ffn_w_out[0].astype(BF16),
                    norm_final_g.reshape(1, D_MODEL))
```

```python
import functools
import math

import numpy as np
import jax
import jax.numpy as jnp
from jax import lax
from jax.experimental import pallas as pl
from jax.experimental.pallas import tpu as pltpu

D_MODEL = 1024
POOL_WIDTH = 512
POOL_GROUPS = 4
POOL_GROUP_DIM = 128
POOL_WINDOWS = (2, 4, 8, 16)
POOL_HALO = 16
ATTN_WIDTH = 512
N_HEADS = 4
HEAD_DIM = 64
V_DIM = 128
IN_PROJ_WIDTH = 2048
NUM_BUCKETS = 32
MAX_DISTANCE = 128
D_FF = 2816
NORM_EPS = 1e-6
SUBLN_EPS = 1e-5
NEG_INF = -1e30
LAMBDA_INIT = 0.8 - 0.6 * math.exp(-0.3 * 0)
LOG2E = math.log2(math.e)

ATTN_TILE = 256
BIAS_TILE = 128
QK_LOOKAHEAD = 2
FAR_PER_TRIP = 4
IN_ROWS = 1024
FFN_ROWS = 1024
FFN_CHUNK = 256
FFN_ROW_BLOCKS = 4
CONV_HALO = 8
WEIGHT_CAST_ROWS = 128
VMEM_LIMIT_BYTES = 56 * 1024 * 1024

F32 = jnp.float32
BF16 = jnp.bfloat16


def _bucket_tiles(tile):
    k = np.arange(tile)[:, None]
    q = np.arange(tile)[None, :]
    out = []
    for offset in (0, tile):
        n = q - k + offset
        max_exact = NUM_BUCKETS // 2
        nf = np.maximum(n, 1).astype(np.float64)
        large = max_exact + (np.log(nf / max_exact) / math.log(MAX_DISTANCE / max_exact)
                             * (NUM_BUCKETS - max_exact)).astype(np.int64)
        large = np.minimum(large, NUM_BUCKETS - 1)
        b = np.where(n < max_exact, n, large)
        out.append(np.where(n < 0, -1, b))
    return np.stack(out).astype(np.int32)


def _bias_kernel(rb_ref, bkt_ref, out_ref):
    h = pl.program_id(0)
    bkt = bkt_ref[...]

    def body(i, acc):
        return jnp.where(bkt == i, rb_ref[i, h] * LOG2E, acc)

    out_ref[0] = lax.fori_loop(0, NUM_BUCKETS, body, jnp.full(bkt.shape, NEG_INF, F32))


def _bias_tiles(rel_bias):
    t = BIAS_TILE
    bkt = jnp.asarray(_bucket_tiles(t))
    return pl.pallas_call(
        _bias_kernel,
        grid=(N_HEADS,),
        in_specs=[pl.BlockSpec(memory_space=pltpu.SMEM),
                  pl.BlockSpec((2, t, t), lambda h: (0, 0, 0))],
        out_specs=pl.BlockSpec((1, 2, t, t), lambda h: (h, 0, 0, 0)),
        out_shape=jax.ShapeDtypeStruct((N_HEADS, 2, t, t), F32),
        name="bias_tiles",
    )(rel_bias, bkt)


def _in_proj_kernel(x_ref, g_ref, w32_ref, pw_ref, ps_ref,
                    ypool_ref, qt_ref, k_ref, vt_ref, ext_ref, w_ref):
    si = pl.program_id(1)
    tm = x_ref.shape[1]

    @pl.when((pl.program_id(0) == 0) & (si == 0))
    def _():
        for r in range(0, D_MODEL, WEIGHT_CAST_ROWS):
            w_ref[r:r + WEIGHT_CAST_ROWS, :] = w32_ref[r:r + WEIGHT_CAST_ROWS, :].astype(BF16)

    @pl.when(si == 0)
    def _():
        ext_ref[0:POOL_HALO, :] = jnp.zeros((POOL_HALO, POOL_WIDTH), F32)

    x = x_ref[0]
    ms = jnp.mean(x * x, axis=-1, keepdims=True)
    hn = (x * lax.rsqrt(ms + NORM_EPS) * g_ref[...]).astype(BF16)
    zp = jnp.dot(hn, w_ref[:, :POOL_WIDTH], preferred_element_type=F32)
    z = jnp.dot(hn, w_ref[:, POOL_WIDTH:], preferred_element_type=F32)

    ext_ref[POOL_HALO:, :] = zp
    tpos = si * tm + lax.broadcasted_iota(jnp.int32, (tm, 1), 0) + 1
    for gi, w in enumerate(POOL_WINDOWS):
        lo, hi = gi * POOL_GROUP_DIM, (gi + 1) * POOL_GROUP_DIM
        tot = ext_ref[:, lo:hi]
        step = 1
        while step < w:
            tot = tot + pltpu.roll(tot, step, axis=0)
            step *= 2
        tot = tot[POOL_HALO:, :]
        cnt = jnp.minimum(tpos, w).astype(F32)
        pooled = tot / cnt - zp[:, lo:hi]
        yg = jnp.dot(pooled.astype(BF16), pw_ref[gi].astype(BF16), preferred_element_type=F32)
        ypool_ref[0, :, lo:hi] = (yg * ps_ref[:, lo:hi]).astype(BF16)
    ext_ref[0:POOL_HALO, :] = zp[tm - POOL_HALO:, :]

    t = ATTN_TILE
    q0, k0, v0 = 0, ATTN_WIDTH, 2 * ATTN_WIDTH
    scale = LOG2E * HEAD_DIM ** -0.5
    for h in range(N_HEADS):
        zq = z[:, q0 + h * V_DIM:q0 + (h + 1) * V_DIM] * scale
        qt_ref[0, h * V_DIM:(h + 1) * V_DIM, :] = zq.T.astype(BF16)
        k_ref[0, h] = z[:, k0 + h * V_DIM:k0 + (h + 1) * V_DIM].astype(BF16)
        vt = z[:, v0 + h * V_DIM:v0 + (h + 1) * V_DIM].T.astype(BF16)
        for j in range(tm // t):
            vt_ref[0, h, j] = vt[:, j * t:(j + 1) * t]


def _in_proj(x, g, w_in, pool_w, pool_scale):
    b, s, d = x.shape
    tm, t = IN_ROWS, ATTN_TILE
    return pl.pallas_call(
        _in_proj_kernel,
        grid=(b, s // tm),
        in_specs=[
            pl.BlockSpec((1, tm, d), lambda bi, si: (bi, si, 0)),
            pl.BlockSpec((1, d), lambda bi, si: (0, 0)),
            pl.BlockSpec((d, IN_PROJ_WIDTH), lambda bi, si: (0, 0), pipeline_mode=pl.Buffered(1)),
            pl.BlockSpec((POOL_GROUPS, POOL_GROUP_DIM, POOL_GROUP_DIM), lambda bi, si: (0, 0, 0)),
            pl.BlockSpec((1, POOL_WIDTH), lambda bi, si: (0, 0)),
        ],
        out_specs=[
            pl.BlockSpec((1, tm, POOL_WIDTH), lambda bi, si: (bi, si, 0)),
            pl.BlockSpec((1, ATTN_WIDTH, tm), lambda bi, si: (bi, 0, si)),
            pl.BlockSpec((1, N_HEADS, tm, V_DIM), lambda bi, si: (bi, 0, si, 0)),
            pl.BlockSpec((1, N_HEADS, tm // t, V_DIM, t), lambda bi, si: (bi, 0, si, 0, 0)),
        ],
        out_shape=[
            jax.ShapeDtypeStruct((b, s, POOL_WIDTH), BF16),
            jax.ShapeDtypeStruct((b, ATTN_WIDTH, s), BF16),
            jax.ShapeDtypeStruct((b, N_HEADS, s, V_DIM), BF16),
            jax.ShapeDtypeStruct((b, N_HEADS, s // t, V_DIM, t), BF16),
        ],
        scratch_shapes=[pltpu.VMEM((POOL_HALO + tm, POOL_WIDTH), F32),
                        pltpu.VMEM((d, IN_PROJ_WIDTH), BF16)],
        compiler_params=pltpu.CompilerParams(
            dimension_semantics=("arbitrary", "arbitrary"),
            vmem_limit_bytes=VMEM_LIMIT_BYTES),
        name="in_proj",
    )(x, g, w_in, pool_w, pool_scale)


def _attn_kernel(rb_ref, qt_ref, qtn_ref, k_ref, vt_ref, bias_ref, lamv_ref, sg_ref, out_ref,
                 *scratch):
    qi = pl.program_id(1)
    t = ATTN_TILE
    m_refs, l_refs, smax_refs, acc_refs, qs_refs, s_refs = (
        scratch[i * N_HEADS:(i + 1) * N_HEADS] for i in range(6))
    qsn_refs = scratch[6 * N_HEADS:6 * N_HEADS + QK_LOOKAHEAD]

    def load_q(h, src_ref, dst_refs):
        qt = src_ref[0, h * V_DIM:(h + 1) * V_DIM, :]
        row = lax.broadcasted_iota(jnp.int32, qt.shape, 0)
        zero = jnp.zeros_like(qt)
        dst_refs[h][:, 0:t] = jnp.where(row < HEAD_DIM, qt, zero)
        dst_refs[h][:, t:2 * t] = jnp.where(row >= HEAD_DIM, qt, zero)

    far_bias = [rb_ref[NUM_BUCKETS - 1, h] * LOG2E for h in range(N_HEADS)]

    lamv = lamv_ref[...]
    lam = (jnp.exp(jnp.sum(lamv[0:1] * lamv[1:2], axis=1, keepdims=True))
           - jnp.exp(jnp.sum(lamv[2:3] * lamv[3:4], axis=1, keepdims=True))
           + LAMBDA_INIT)

    def qk(h, j, kind=None, q_refs=qs_refs):
        kj = k_ref[0, h, pl.ds(pl.multiple_of(j * t, t), t), :]
        s = jnp.dot(kj, q_refs[h][...], preferred_element_type=F32)
        if kind is not None:
            s = with_bias(s, h, kind)
        s_refs[h][...] = s
        smax_refs[h][...] = jnp.max(s, axis=0, keepdims=True)

    def with_bias(s, h, diag):
        b = tile_bias(h, diag)
        return s + jnp.concatenate([b, b], axis=1)

    def tile_bias(h, diag):
        n = t // BIAS_TILE
        pieces = {0: bias_ref[h, 0], 1: bias_ref[h, 1]}
        far = jnp.full((BIAS_TILE, BIAS_TILE), far_bias[h], F32)
        masked = jnp.full((BIAS_TILE, BIAS_TILE), NEG_INF, F32)
        rows = []
        for kb in range(n):
            d = [qb - kb + (0 if diag else n) for qb in range(n)]
            rows.append(jnp.concatenate([masked if x < 0 else pieces.get(x, far) for x in d], axis=1))
        return jnp.concatenate(rows, axis=0)

    def softmax_pv(h, j, pending_bias, off):
        s = s_refs[h][...]
        if pending_bias is None:
            smax = smax_refs[h][...]
        else:
            s = with_bias(s, h, pending_bias)
            smax = jnp.max(s, axis=0, keepdims=True)
        m_old = m_refs[h][...]
        m_new = jnp.maximum(m_old, smax + off)
        alpha = jnp.exp2(m_old - m_new)
        p = jnp.exp2(s - (m_new - off))
        l_refs[h][...] = alpha * l_refs[h][...] + jnp.sum(p, axis=0, keepdims=True)
        m_refs[h][...] = m_new
        r = jnp.dot(vt_ref[0, h, j], p.astype(BF16), preferred_element_type=F32)
        acc_refs[h][...] = alpha * acc_refs[h][...] + r

    def run(tiles, next_j):
        stages = [(h, j, kind) for (j, kind) in tiles for h in range(N_HEADS)]
        for i, (h, j, kind) in enumerate(stages):
            ahead = i + QK_LOOKAHEAD
            if ahead < len(stages):
                qk(*stages[ahead])
            elif next_j is not None:
                qk(ahead - len(stages), next_j)
            else:
                load_q(ahead - len(stages), qtn_ref, qsn_refs)
                qk(ahead - len(stages), 0, q_refs=qsn_refs)
            pending = kind if i < QK_LOOKAHEAD else None
            softmax_pv(h, j, pending, far_bias[h] if kind is None else 0.0)
            if kind is True:
                finalize(h)

    def finalize(h):
        o = acc_refs[h][...] * (1.0 / l_refs[h][...])
        o = o[:, :t] - lam * o[:, t:]
        ms = jnp.mean(o * o, axis=0, keepdims=True)
        y = (o * lax.rsqrt(ms + SUBLN_EPS)).T * (sg_ref[...] * (1.0 - LAMBDA_INIT))
        out_ref[0, :, h * V_DIM:(h + 1) * V_DIM] = y.astype(BF16)

    @pl.when(qi == 0)
    def _():
        for h in range(QK_LOOKAHEAD):
            load_q(h, qt_ref, qsn_refs)
            qk(h, 0, q_refs=qsn_refs)

    for h in range(QK_LOOKAHEAD):
        qs_refs[h][...] = qsn_refs[h][...]
    for h in range(QK_LOOKAHEAD, N_HEADS):
        load_q(h, qt_ref, qs_refs)
    for h in range(N_HEADS):
        m_refs[h][...] = jnp.full(m_refs[h].shape, -jnp.inf, F32)
        l_refs[h][...] = jnp.zeros(l_refs[h].shape, F32)
        acc_refs[h][...] = jnp.zeros(acc_refs[h].shape, F32)

    n_far = qi - 1

    def far_body(i, carry):
        j = FAR_PER_TRIP * i
        run([(j + d, None) for d in range(FAR_PER_TRIP)], j + FAR_PER_TRIP)
        return carry

    lax.fori_loop(0, jnp.maximum(n_far, 0) // FAR_PER_TRIP, far_body, 0)

    @pl.when(qi == 0)
    def _():
        run([(qi, True)], None)

    for rem in range(FAR_PER_TRIP):
        @pl.when((qi >= 1) & (n_far % FAR_PER_TRIP == rem))
        def _():
            run([(qi - 1 - d, None) for d in range(rem, 0, -1)] + [(qi - 1, False), (qi, True)], None)


def _diff_attn(rel_bias, qt, k, vt, bias, lamv, subln_g):
    b, _, s = qt.shape
    t = ATTN_TILE
    nq = s // t
    per_head = lambda shape, dtype: [pltpu.VMEM(shape, dtype) for _ in range(N_HEADS)]
    return pl.pallas_call(
        _attn_kernel,
        grid=(b, nq),
        in_specs=[
            pl.BlockSpec(memory_space=pltpu.SMEM),
            pl.BlockSpec((1, ATTN_WIDTH, t), lambda bi, qi: (bi, 0, qi)),
            pl.BlockSpec((1, QK_LOOKAHEAD * V_DIM, t),
                         lambda bi, qi: (bi, 0, jnp.minimum(qi + 1, nq - 1))),
            pl.BlockSpec((1, N_HEADS, s, V_DIM), lambda bi, qi: (bi, 0, 0, 0)),
            pl.BlockSpec((1, N_HEADS, nq, V_DIM, t), lambda bi, qi: (bi, 0, 0, 0, 0)),
            pl.BlockSpec((N_HEADS, 2, BIAS_TILE, BIAS_TILE), lambda bi, qi: (0, 0, 0, 0)),
            pl.BlockSpec((4, HEAD_DIM), lambda bi, qi: (0, 0)),
            pl.BlockSpec((1, V_DIM), lambda bi, qi: (0, 0)),
        ],
        out_specs=pl.BlockSpec((1, t, ATTN_WIDTH), lambda bi, qi: (bi, qi, 0)),
        out_shape=jax.ShapeDtypeStruct((b, s, ATTN_WIDTH), BF16),
        scratch_shapes=(per_head((1, 2 * t), F32) + per_head((1, 2 * t), F32)
                        + per_head((1, 2 * t), F32)
                        + per_head((V_DIM, 2 * t), F32) + per_head((V_DIM, 2 * t), BF16)
                        + per_head((t, 2 * t), F32)
                        + [pltpu.VMEM((V_DIM, 2 * t), BF16) for _ in range(QK_LOOKAHEAD)]),
        compiler_params=pltpu.CompilerParams(
            dimension_semantics=("arbitrary", "arbitrary"),
            vmem_limit_bytes=VMEM_LIMIT_BYTES),
        name="diff_attn",
    )(rel_bias, qt, qt, k, vt, bias, lamv, subln_g)


def _ffn_kernel(x_ref, yp_ref, ya_ref, wo_ref, gf_ref, wg_ref, wu_ref, cw_ref, cb_ref,
                w2_ref, gl_ref, out_ref, carry_ref, gbuf_ref, act_ref, hn_ref):
    si = pl.program_id(1)
    tm = x_ref.shape[1]
    ck = FFN_CHUNK
    rows = tm // FFN_ROW_BLOCKS

    @pl.when(si == 0)
    def _():
        carry_ref[...] = jnp.zeros(carry_ref.shape, F32)

    for r0 in range(0, tm, rows):
        y = (jnp.dot(yp_ref[0, r0:r0 + rows, :], wo_ref[0:POOL_WIDTH, :], preferred_element_type=F32)
             + jnp.dot(ya_ref[0, r0:r0 + rows, :], wo_ref[POOL_WIDTH:, :], preferred_element_type=F32))
        x1 = x_ref[0, r0:r0 + rows, :] + y
        out_ref[0, r0:r0 + rows, :] = x1
        ms = jnp.mean(x1 * x1, axis=-1, keepdims=True)
        hn_ref[r0:r0 + rows, :] = (x1 * lax.rsqrt(ms + NORM_EPS) * gf_ref[...]).astype(BF16)

    for c in range(D_FF // ck):
        lo, hi = c * ck, (c + 1) * ck
        g = jnp.dot(hn_ref[...], wg_ref[:, lo:hi], preferred_element_type=F32)
        u = jnp.dot(hn_ref[...], wu_ref[:, lo:hi], preferred_element_type=F32)
        gbuf_ref[0:CONV_HALO, :] = carry_ref[:, lo:hi]
        gbuf_ref[CONV_HALO:, :] = g
        g1 = gbuf_ref[pl.ds(CONV_HALO - 1, tm), :]
        g2 = gbuf_ref[pl.ds(CONV_HALO - 2, tm), :]
        carry_ref[:, lo:hi] = g[tm - CONV_HALO:, :]
        cv = (cw_ref[0:1, lo:hi] * g2 + cw_ref[1:2, lo:hi] * g1 + cw_ref[2:3, lo:hi] * g
              + cb_ref[:, lo:hi])
        a = cv * (1.0 / (1.0 + jnp.exp(-cv))) * u
        act_ref[:, lo:hi] = a.astype(BF16)

    for r0 in range(0, tm, rows):
        x2 = out_ref[0, r0:r0 + rows, :] + jnp.dot(act_ref[r0:r0 + rows, :], w2_ref[...],
                                                   preferred_element_type=F32)
        ms2 = jnp.mean(x2 * x2, axis=-1, keepdims=True)
        out_ref[0, r0:r0 + rows, :] = x2 * lax.rsqrt(ms2 + NORM_EPS) * gl_ref[...]


def _out_ffn(x, y_pool, y_attn, w_out, g_ffn, wg, wu, conv_w, conv_b, w2, g_final):
    b, s, d = x.shape
    tm = FFN_ROWS
    const = lambda bi, si: (0, 0)
    resident = functools.partial(pl.BlockSpec, index_map=const, pipeline_mode=pl.Buffered(1))
    return pl.pallas_call(
        _ffn_kernel,
        grid=(b, s // tm),
        in_specs=[
            pl.BlockSpec((1, tm, d), lambda bi, si: (bi, si, 0)),
            pl.BlockSpec((1, tm, POOL_WIDTH), lambda bi, si: (bi, si, 0)),
            pl.BlockSpec((1, tm, ATTN_WIDTH), lambda bi, si: (bi, si, 0)),
            resident((d, d)),
            resident((1, d)),
            resident((d, D_FF)),
            resident((d, D_FF)),
            resident((3, D_FF)),
            resident((1, D_FF)),
            resident((D_FF, d)),
            resident((1, d)),
        ],
        out_specs=pl.BlockSpec((1, tm, d), lambda bi, si: (bi, si, 0)),
        out_shape=jax.ShapeDtypeStruct((b, s, d), F32),
        scratch_shapes=[pltpu.VMEM((CONV_HALO, D_FF), F32),
                        pltpu.VMEM((CONV_HALO + tm, FFN_CHUNK), F32),
                        pltpu.VMEM((tm, D_FF), BF16),
                        pltpu.VMEM((tm, d), BF16)],
        compiler_params=pltpu.CompilerParams(
            dimension_semantics=("arbitrary", "arbitrary"),
            vmem_limit_bytes=VMEM_LIMIT_BYTES),
        name="out_ffn",
    )(x, y_pool, y_attn, w_out, g_ffn, wg, wu, conv_w, conv_b, w2, g_final)


def kernel(x, norm_mix_g, w_in, pool_w, pool_scale, lambda_q1, lambda_k1, lambda_q2, lambda_k2,
           subln_g, rel_bias, w_out, norm_ffn_g, ffn_w_in, ffn_conv_w, ffn_conv_b, ffn_w_out,
           norm_final_g):
    bias = _bias_tiles(rel_bias)
    y_pool, qt, k, vt = _in_proj(x, norm_mix_g, w_in[0], pool_w[0], pool_scale)
    lamv = jnp.concatenate([lambda_q1, lambda_k1, lambda_q2, lambda_k2], axis=0)
    y_attn = _diff_attn(rel_bias, qt, k, vt, bias, lamv, subln_g)
    wf = ffn_w_in[0]
    return _out_ffn(x, y_pool, y_attn, w_out[0].astype(BF16), norm_ffn_g,
                    wf[:, :D_FF].astype(BF16), wf[:, D_FF:].astype(BF16),
                    ffn_conv_w[0], ffn_conv_b, ffn_w_out[0].astype(BF16),
                    norm_final_g.reshape(1, D_MODEL))
```

```python
import functools
import math

import numpy as np
import jax
import jax.numpy as jnp
from jax import lax
from jax.experimental import pallas as pl
from jax.experimental.pallas import tpu as pltpu

D_MODEL = 1024
POOL_WIDTH = 512
POOL_GROUPS = 4
POOL_GROUP_DIM = 128
POOL_WINDOWS = (2, 4, 8, 16)
POOL_HALO = 16
ATTN_WIDTH = 512
N_HEADS = 4
HEAD_DIM = 64
V_DIM = 128
IN_PROJ_WIDTH = 2048
NUM_BUCKETS = 32
MAX_DISTANCE = 128
D_FF = 2816
NORM_EPS = 1e-6
SUBLN_EPS = 1e-5
NEG_INF = -1e30
LAMBDA_INIT = 0.8 - 0.6 * math.exp(-0.3 * 0)
LOG2E = math.log2(math.e)

ATTN_TILE = 256
BIAS_TILE = 128
QK_LOOKAHEAD = 2
FAR_PER_TRIP = 4
IN_ROWS = 1024
FFN_ROWS = 1024
FFN_CHUNK = 256
FFN_ROW_BLOCKS = 4
CONV_HALO = 8
WEIGHT_CAST_ROWS = 128
N_LATER_WEIGHTS = 3
VMEM_LIMIT_BYTES = 56 * 1024 * 1024

F32 = jnp.float32
BF16 = jnp.bfloat16


def _bucket_tiles(tile):
    k = np.arange(tile)[:, None]
    q = np.arange(tile)[None, :]
    out = []
    for offset in (0, tile):
        n = q - k + offset
        max_exact = NUM_BUCKETS // 2
        nf = np.maximum(n, 1).astype(np.float64)
        large = max_exact + (np.log(nf / max_exact) / math.log(MAX_DISTANCE / max_exact)
                             * (NUM_BUCKETS - max_exact)).astype(np.int64)
        large = np.minimum(large, NUM_BUCKETS - 1)
        b = np.where(n < max_exact, n, large)
        out.append(np.where(n < 0, -1, b))
    return np.stack(out).astype(np.int32)


def _bias_kernel(rb_ref, bkt_ref, out_ref):
    h = pl.program_id(0)
    bkt = bkt_ref[...]

    def body(i, acc):
        return jnp.where(bkt == i, rb_ref[i, h] * LOG2E, acc)

    out_ref[0] = lax.fori_loop(0, NUM_BUCKETS, body, jnp.full(bkt.shape, NEG_INF, F32))


def _bias_tiles(rel_bias):
    t = BIAS_TILE
    bkt = jnp.asarray(_bucket_tiles(t))
    return pl.pallas_call(
        _bias_kernel,
        grid=(N_HEADS,),
        in_specs=[pl.BlockSpec(memory_space=pltpu.SMEM),
                  pl.BlockSpec((2, t, t), lambda h: (0, 0, 0))],
        out_specs=pl.BlockSpec((1, 2, t, t), lambda h: (h, 0, 0, 0)),
        out_shape=jax.ShapeDtypeStruct((N_HEADS, 2, t, t), F32),
        name="bias_tiles",
    )(rel_bias, bkt)


def _in_proj_kernel(x_ref, g_ref, w32_ref, pw_ref, ps_ref, *rest):
    later32_refs = rest[:N_LATER_WEIGHTS]
    ypool_ref, qt_ref, k_ref, vt_ref = rest[N_LATER_WEIGHTS:N_LATER_WEIGHTS + 4]
    later16_refs = rest[N_LATER_WEIGHTS + 4:2 * N_LATER_WEIGHTS + 4]
    ext_ref, w_ref = rest[2 * N_LATER_WEIGHTS + 4:]
    si = pl.program_id(1)
    tm = x_ref.shape[1]

    @pl.when((pl.program_id(0) == 0) & (si == 0))
    def _():
        for r in range(0, D_MODEL, WEIGHT_CAST_ROWS):
            w_ref[r:r + WEIGHT_CAST_ROWS, :] = w32_ref[r:r + WEIGHT_CAST_ROWS, :].astype(BF16)

    @pl.when(si == 0)
    def _():
        ext_ref[0:POOL_HALO, :] = jnp.zeros((POOL_HALO, POOL_WIDTH), F32)

    x = x_ref[0]
    ms = jnp.mean(x * x, axis=-1, keepdims=True)
    hn = (x * lax.rsqrt(ms + NORM_EPS) * g_ref[...]).astype(BF16)
    zp = jnp.dot(hn, w_ref[:, :POOL_WIDTH], preferred_element_type=F32)
    z = jnp.dot(hn, w_ref[:, POOL_WIDTH:], preferred_element_type=F32)

    for src_ref, dst_ref in zip(later32_refs, later16_refs):
        dst_ref[...] = src_ref[...].astype(BF16)

    ext_ref[POOL_HALO:, :] = zp
    tpos = si * tm + lax.broadcasted_iota(jnp.int32, (tm, 1), 0) + 1
    for gi, w in enumerate(POOL_WINDOWS):
        lo, hi = gi * POOL_GROUP_DIM, (gi + 1) * POOL_GROUP_DIM
        tot = ext_ref[:, lo:hi]
        step = 1
        while step < w:
            tot = tot + pltpu.roll(tot, step, axis=0)
            step *= 2
        tot = tot[POOL_HALO:, :]
        cnt = jnp.minimum(tpos, w).astype(F32)
        pooled = tot / cnt - zp[:, lo:hi]
        yg = jnp.dot(pooled.astype(BF16), pw_ref[gi].astype(BF16), preferred_element_type=F32)
        ypool_ref[0, :, lo:hi] = (yg * ps_ref[:, lo:hi]).astype(BF16)
    ext_ref[0:POOL_HALO, :] = zp[tm - POOL_HALO:, :]

    t = ATTN_TILE
    q0, k0, v0 = 0, ATTN_WIDTH, 2 * ATTN_WIDTH
    scale = LOG2E * HEAD_DIM ** -0.5
    for h in range(N_HEADS):
        zq = z[:, q0 + h * V_DIM:q0 + (h + 1) * V_DIM] * scale
        qt_ref[0, h * V_DIM:(h + 1) * V_DIM, :] = zq.T.astype(BF16)
        k_ref[0, h] = z[:, k0 + h * V_DIM:k0 + (h + 1) * V_DIM].astype(BF16)
        vt = z[:, v0 + h * V_DIM:v0 + (h + 1) * V_DIM].T.astype(BF16)
        for j in range(tm // t):
            vt_ref[0, h, j] = vt[:, j * t:(j + 1) * t]


def _in_proj(x, g, w_in, pool_w, pool_scale, later_weights):
    b, s, d = x.shape
    tm, t = IN_ROWS, ATTN_TILE
    steps = s // tm
    assert len(later_weights) == N_LATER_WEIGHTS
    slabs = [(w.shape[0] // (b * steps), w.shape[1]) for w in later_weights]
    slab_specs = [pl.BlockSpec(sl, lambda bi, si: (bi * steps + si, 0)) for sl in slabs]
    return pl.pallas_call(
        _in_proj_kernel,
        grid=(b, s // tm),
        in_specs=[
            pl.BlockSpec((1, tm, d), lambda bi, si: (bi, si, 0)),
            pl.BlockSpec((1, d), lambda bi, si: (0, 0)),
            pl.BlockSpec((d, IN_PROJ_WIDTH), lambda bi, si: (0, 0), pipeline_mode=pl.Buffered(1)),
            pl.BlockSpec((POOL_GROUPS, POOL_GROUP_DIM, POOL_GROUP_DIM), lambda bi, si: (0, 0, 0)),
            pl.BlockSpec((1, POOL_WIDTH), lambda bi, si: (0, 0)),
        ] + slab_specs,
        out_specs=[
            pl.BlockSpec((1, tm, POOL_WIDTH), lambda bi, si: (bi, si, 0)),
            pl.BlockSpec((1, ATTN_WIDTH, tm), lambda bi, si: (bi, 0, si)),
            pl.BlockSpec((1, N_HEADS, tm, V_DIM), lambda bi, si: (bi, 0, si, 0)),
            pl.BlockSpec((1, N_HEADS, tm // t, V_DIM, t), lambda bi, si: (bi, 0, si, 0, 0)),
        ] + slab_specs,
        out_shape=[
            jax.ShapeDtypeStruct((b, s, POOL_WIDTH), BF16),
            jax.ShapeDtypeStruct((b, ATTN_WIDTH, s), BF16),
            jax.ShapeDtypeStruct((b, N_HEADS, s, V_DIM), BF16),
            jax.ShapeDtypeStruct((b, N_HEADS, s // t, V_DIM, t), BF16),
        ] + [jax.ShapeDtypeStruct(w.shape, BF16) for w in later_weights],
        scratch_shapes=[pltpu.VMEM((POOL_HALO + tm, POOL_WIDTH), F32),
                        pltpu.VMEM((d, IN_PROJ_WIDTH), BF16)],
        compiler_params=pltpu.CompilerParams(
            dimension_semantics=("arbitrary", "arbitrary"),
            vmem_limit_bytes=VMEM_LIMIT_BYTES),
        name="in_proj",
    )(x, g, w_in, pool_w, pool_scale, *later_weights)


def _attn_kernel(rb_ref, qt_ref, qtn_ref, k_ref, vt_ref, bias_ref, lamv_ref, sg_ref, out_ref,
                 *scratch):
    qi = pl.program_id(1)
    t = ATTN_TILE
    m_refs, l_refs, smax_refs, acc_refs, qs_refs, s_refs = (
        scratch[i * N_HEADS:(i + 1) * N_HEADS] for i in range(6))
    qsn_refs = scratch[6 * N_HEADS:6 * N_HEADS + QK_LOOKAHEAD]

    def load_q(h, src_ref, dst_refs):
        qt = src_ref[0, h * V_DIM:(h + 1) * V_DIM, :]
        row = lax.broadcasted_iota(jnp.int32, qt.shape, 0)
        zero = jnp.zeros_like(qt)
        dst_refs[h][:, 0:t] = jnp.where(row < HEAD_DIM, qt, zero)
        dst_refs[h][:, t:2 * t] = jnp.where(row >= HEAD_DIM, qt, zero)

    far_bias = [rb_ref[NUM_BUCKETS - 1, h] * LOG2E for h in range(N_HEADS)]

    lamv = lamv_ref[...]
    lam = (jnp.exp(jnp.sum(lamv[0:1] * lamv[1:2], axis=1, keepdims=True))
           - jnp.exp(jnp.sum(lamv[2:3] * lamv[3:4], axis=1, keepdims=True))
           + LAMBDA_INIT)

    def qk(h, j, kind=None, q_refs=qs_refs):
        kj = k_ref[0, h, pl.ds(pl.multiple_of(j * t, t), t), :]
        s = jnp.dot(kj, q_refs[h][...], preferred_element_type=F32)
        if kind is not None:
            s = with_bias(s, h, kind)
        s_refs[h][...] = s
        smax_refs[h][...] = jnp.max(s, axis=0, keepdims=True)

    def with_bias(s, h, diag):
        b = tile_bias(h, diag)
        return s + jnp.concatenate([b, b], axis=1)

    def tile_bias(h, diag):
        n = t // BIAS_TILE
        pieces = {0: bias_ref[h, 0], 1: bias_ref[h, 1]}
        far = jnp.full((BIAS_TILE, BIAS_TILE), far_bias[h], F32)
        masked = jnp.full((BIAS_TILE, BIAS_TILE), NEG_INF, F32)
        rows = []
        for kb in range(n):
            d = [qb - kb + (0 if diag else n) for qb in range(n)]
            rows.append(jnp.concatenate([masked if x < 0 else pieces.get(x, far) for x in d], axis=1))
        return jnp.concatenate(rows, axis=0)

    def softmax_pv(h, j, pending_bias, off):
        s = s_refs[h][...]
        if pending_bias is None:
            smax = smax_refs[h][...]
        else:
            s = with_bias(s, h, pending_bias)
            smax = jnp.max(s, axis=0, keepdims=True)
        m_old = m_refs[h][...]
        m_new = jnp.maximum(m_old, smax + off)
        alpha = jnp.exp2(m_old - m_new)
        p = jnp.exp2(s - (m_new - off))
        l_refs[h][...] = alpha * l_refs[h][...] + jnp.sum(p, axis=0, keepdims=True)
        m_refs[h][...] = m_new
        r = jnp.dot(vt_ref[0, h, j], p.astype(BF16), preferred_element_type=F32)
        acc_refs[h][...] = alpha * acc_refs[h][...] + r

    def run(tiles, next_j):
        stages = [(h, j, kind) for (j, kind) in tiles for h in range(N_HEADS)]
        for i, (h, j, kind) in enumerate(stages):
            ahead = i + QK_LOOKAHEAD
            if ahead < len(stages):
                qk(*stages[ahead])
            elif next_j is not None:
                qk(ahead - len(stages), next_j)
            else:
                load_q(ahead - len(stages), qtn_ref, qsn_refs)
                qk(ahead - len(stages), 0, q_refs=qsn_refs)
            pending = kind if i < QK_LOOKAHEAD else None
            softmax_pv(h, j, pending, far_bias[h] if kind is None else 0.0)
            if kind is True:
                finalize(h)

    def finalize(h):
        o = acc_refs[h][...] * (1.0 / l_refs[h][...])
        o = o[:, :t] - lam * o[:, t:]
        ms = jnp.mean(o * o, axis=0, keepdims=True)
        y = (o * lax.rsqrt(ms + SUBLN_EPS)).T * (sg_ref[...] * (1.0 - LAMBDA_INIT))
        out_ref[0, :, h * V_DIM:(h + 1) * V_DIM] = y.astype(BF16)

    @pl.when(qi == 0)
    def _():
        for h in range(QK_LOOKAHEAD):
            load_q(h, qt_ref, qsn_refs)
            qk(h, 0, q_refs=qsn_refs)

    for h in range(QK_LOOKAHEAD):
        qs_refs[h][...] = qsn_refs[h][...]
    for h in range(QK_LOOKAHEAD, N_HEADS):
        load_q(h, qt_ref, qs_refs)
    for h in range(N_HEADS):
        m_refs[h][...] = jnp.full(m_refs[h].shape, -jnp.inf, F32)
        l_refs[h][...] = jnp.zeros(l_refs[h].shape, F32)
        acc_refs[h][...] = jnp.zeros(acc_refs[h].shape, F32)

    n_far = qi - 1

    def far_body(i, carry):
        j = FAR_PER_TRIP * i
        run([(j + d, None) for d in range(FAR_PER_TRIP)], j + FAR_PER_TRIP)
        return carry

    lax.fori_loop(0, jnp.maximum(n_far, 0) // FAR_PER_TRIP, far_body, 0)

    @pl.when(qi == 0)
    def _():
        run([(qi, True)], None)

    for rem in range(FAR_PER_TRIP):
        @pl.when((qi >= 1) & (n_far % FAR_PER_TRIP == rem))
        def _():
            run([(qi - 1 - d, None) for d in range(rem, 0, -1)] + [(qi - 1, False), (qi, True)], None)


def _diff_attn(rel_bias, qt, k, vt, bias, lamv, subln_g):
    b, _, s = qt.shape
    t = ATTN_TILE
    nq = s // t
    per_head = lambda shape, dtype: [pltpu.VMEM(shape, dtype) for _ in range(N_HEADS)]
    return pl.pallas_call(
        _attn_kernel,
        grid=(b, nq),
        in_specs=[
            pl.BlockSpec(memory_space=pltpu.SMEM),
            pl.BlockSpec((1, ATTN_WIDTH, t), lambda bi, qi: (bi, 0, qi)),
            pl.BlockSpec((1, QK_LOOKAHEAD * V_DIM, t),
                         lambda bi, qi: (bi, 0, jnp.minimum(qi + 1, nq - 1))),
            pl.BlockSpec((1, N_HEADS, s, V_DIM), lambda bi, qi: (bi, 0, 0, 0)),
            pl.BlockSpec((1, N_HEADS, nq, V_DIM, t), lambda bi, qi: (bi, 0, 0, 0, 0)),
            pl.BlockSpec((N_HEADS, 2, BIAS_TILE, BIAS_TILE), lambda bi, qi: (0, 0, 0, 0)),
            pl.BlockSpec((4, HEAD_DIM), lambda bi, qi: (0, 0)),
            pl.BlockSpec((1, V_DIM), lambda bi, qi: (0, 0)),
        ],
        out_specs=pl.BlockSpec((1, t, ATTN_WIDTH), lambda bi, qi: (bi, qi, 0)),
        out_shape=jax.ShapeDtypeStruct((b, s, ATTN_WIDTH), BF16),
        scratch_shapes=(per_head((1, 2 * t), F32) + per_head((1, 2 * t), F32)
                        + per_head((1, 2 * t), F32)
                        + per_head((V_DIM, 2 * t), F32) + per_head((V_DIM, 2 * t), BF16)
                        + per_head((t, 2 * t), F32)
                        + [pltpu.VMEM((V_DIM, 2 * t), BF16) for _ in range(QK_LOOKAHEAD)]),
        compiler_params=pltpu.CompilerParams(
            dimension_semantics=("arbitrary", "arbitrary"),
            vmem_limit_bytes=VMEM_LIMIT_BYTES),
        name="diff_attn",
    )(rel_bias, qt, qt, k, vt, bias, lamv, subln_g)


def _ffn_kernel(x_ref, yp_ref, ya_ref, wo_ref, gf_ref, w1_ref, cw_ref, cb_ref,
                w2_ref, gl_ref, out_ref, carry_ref, gbuf_ref, act_ref, hn_ref):
    si = pl.program_id(1)
    tm = x_ref.shape[1]
    ck = FFN_CHUNK
    rows = tm // FFN_ROW_BLOCKS

    @pl.when(si == 0)
    def _():
        carry_ref[...] = jnp.zeros(carry_ref.shape, F32)

    for r0 in range(0, tm, rows):
        y = (jnp.dot(yp_ref[0, r0:r0 + rows, :], wo_ref[0:POOL_WIDTH, :], preferred_element_type=F32)
             + jnp.dot(ya_ref[0, r0:r0 + rows, :], wo_ref[POOL_WIDTH:, :], preferred_element_type=F32))
        x1 = x_ref[0, r0:r0 + rows, :] + y
        out_ref[0, r0:r0 + rows, :] = x1
        ms = jnp.mean(x1 * x1, axis=-1, keepdims=True)
        hn_ref[r0:r0 + rows, :] = (x1 * lax.rsqrt(ms + NORM_EPS) * gf_ref[...]).astype(BF16)

    for c in range(D_FF // ck):
        lo, hi = c * ck, (c + 1) * ck
        g = jnp.dot(hn_ref[...], w1_ref[:, lo:hi], preferred_element_type=F32)
        u = jnp.dot(hn_ref[...], w1_ref[:, D_FF + lo:D_FF + hi], preferred_element_type=F32)
        gbuf_ref[0:CONV_HALO, :] = carry_ref[:, lo:hi]
        gbuf_ref[CONV_HALO:, :] = g
        g1 = gbuf_ref[pl.ds(CONV_HALO - 1, tm), :]
        g2 = gbuf_ref[pl.ds(CONV_HALO - 2, tm), :]
        carry_ref[:, lo:hi] = g[tm - CONV_HALO:, :]
        cv = (cw_ref[0:1, lo:hi] * g2 + cw_ref[1:2, lo:hi] * g1 + cw_ref[2:3, lo:hi] * g
              + cb_ref[:, lo:hi])
        a = cv * (1.0 / (1.0 + jnp.exp(-cv))) * u
        act_ref[:, lo:hi] = a.astype(BF16)

    for r0 in range(0, tm, rows):
        x2 = out_ref[0, r0:r0 + rows, :] + jnp.dot(act_ref[r0:r0 + rows, :], w2_ref[...],
                                                   preferred_element_type=F32)
        ms2 = jnp.mean(x2 * x2, axis=-1, keepdims=True)
        out_ref[0, r0:r0 + rows, :] = x2 * lax.rsqrt(ms2 + NORM_EPS) * gl_ref[...]


def _out_ffn(x, y_pool, y_attn, w_out, g_ffn, w1, conv_w, conv_b, w2, g_final):
    b, s, d = x.shape
    tm = FFN_ROWS
    const = lambda bi, si: (0, 0)
    resident = functools.partial(pl.BlockSpec, index_map=const, pipeline_mode=pl.Buffered(1))
    return pl.pallas_call(
        _ffn_kernel,
        grid=(b, s // tm),
        in_specs=[
            pl.BlockSpec((1, tm, d), lambda bi, si: (bi, si, 0)),
            pl.BlockSpec((1, tm, POOL_WIDTH), lambda bi, si: (bi, si, 0)),
            pl.BlockSpec((1, tm, ATTN_WIDTH), lambda bi, si: (bi, si, 0)),
            resident((d, d)),
            resident((1, d)),
            resident((d, 2 * D_FF)),
            resident((3, D_FF)),
            resident((1, D_FF)),
            resident((D_FF, d)),
            resident((1, d)),
        ],
        out_specs=pl.BlockSpec((1, tm, d), lambda bi, si: (bi, si, 0)),
        out_shape=jax.ShapeDtypeStruct((b, s, d), F32),
        scratch_shapes=[pltpu.VMEM((CONV_HALO, D_FF), F32),
                        pltpu.VMEM((CONV_HALO + tm, FFN_CHUNK), F32),
                        pltpu.VMEM((tm, D_FF), BF16),
                        pltpu.VMEM((tm, d), BF16)],
        compiler_params=pltpu.CompilerParams(
            dimension_semantics=("arbitrary", "arbitrary"),
            vmem_limit_bytes=VMEM_LIMIT_BYTES),
        name="out_ffn",
    )(x, y_pool, y_attn, w_out, g_ffn, w1, conv_w, conv_b, w2, g_final)


def kernel(x, norm_mix_g, w_in, pool_w, pool_scale, lambda_q1, lambda_k1, lambda_q2, lambda_k2,
           subln_g, rel_bias, w_out, norm_ffn_g, ffn_w_in, ffn_conv_w, ffn_conv_b, ffn_w_out,
           norm_final_g):
    bias = _bias_tiles(rel_bias)
    y_pool, qt, k, vt, w_out16, w1_16, w2_16 = _in_proj(
        x, norm_mix_g, w_in[0], pool_w[0], pool_scale, (w_out[0], ffn_w_in[0], ffn_w_out[0]))
    lamv = jnp.concatenate([lambda_q1, lambda_k1, lambda_q2, lambda_k2], axis=0)
    y_attn = _diff_attn(rel_bias, qt, k, vt, bias, lamv, subln_g)
    return _out_ffn(x, y_pool, y_attn, w_out16, norm_ffn_g, w1_16, ffn_conv_w[0], ffn_conv_b,
                    w2_16, norm_final_g.reshape(1, D_MODEL))
```

```python
import functools
import math

import numpy as np
import jax
import jax.numpy as jnp
from jax import lax
from jax.experimental import pallas as pl
from jax.experimental.pallas import tpu as pltpu

D_MODEL = 1024
POOL_WIDTH = 512
POOL_GROUPS = 4
POOL_GROUP_DIM = 128
POOL_WINDOWS = (2, 4, 8, 16)
POOL_HALO = 16
ATTN_WIDTH = 512
N_HEADS = 4
HEAD_DIM = 64
V_DIM = 128
IN_PROJ_WIDTH = 2048
NUM_BUCKETS = 32
MAX_DISTANCE = 128
D_FF = 2816
NORM_EPS = 1e-6
SUBLN_EPS = 1e-5
NEG_INF = -1e30
LAMBDA_INIT = 0.8 - 0.6 * math.exp(-0.3 * 0)
LOG2E = math.log2(math.e)

ATTN_TILE = 256
BIAS_TILE = 128
QK_LOOKAHEAD = 2
FAR_PER_TRIP = 4
ATTN_TILES_PER_STEP = 2
IN_ROWS = 1024
FFN_ROWS = 1024
FFN_CHUNK = 256
FFN_ROW_BLOCKS = 4
CONV_HALO = 8
WEIGHT_CAST_ROWS = 128
N_LATER_WEIGHTS = 3
VMEM_LIMIT_BYTES = 56 * 1024 * 1024

F32 = jnp.float32
BF16 = jnp.bfloat16


def _bucket_tiles(tile):
    k = np.arange(tile)[:, None]
    q = np.arange(tile)[None, :]
    out = []
    for offset in (0, tile):
        n = q - k + offset
        max_exact = NUM_BUCKETS // 2
        nf = np.maximum(n, 1).astype(np.float64)
        large = max_exact + (np.log(nf / max_exact) / math.log(MAX_DISTANCE / max_exact)
                             * (NUM_BUCKETS - max_exact)).astype(np.int64)
        large = np.minimum(large, NUM_BUCKETS - 1)
        b = np.where(n < max_exact, n, large)
        out.append(np.where(n < 0, -1, b))
    return np.stack(out).astype(np.int32)


def _bias_kernel(rb_ref, bkt_ref, out_ref):
    h = pl.program_id(0)
    bkt = bkt_ref[...]

    def body(i, acc):
        return jnp.where(bkt == i, rb_ref[i, h] * LOG2E, acc)

    out_ref[0] = lax.fori_loop(0, NUM_BUCKETS, body, jnp.full(bkt.shape, NEG_INF, F32))


def _bias_tiles(rel_bias):
    t = BIAS_TILE
    bkt = jnp.asarray(_bucket_tiles(t))
    return pl.pallas_call(
        _bias_kernel,
        grid=(N_HEADS,),
        in_specs=[pl.BlockSpec(memory_space=pltpu.SMEM),
                  pl.BlockSpec((2, t, t), lambda h: (0, 0, 0))],
        out_specs=pl.BlockSpec((1, 2, t, t), lambda h: (h, 0, 0, 0)),
        out_shape=jax.ShapeDtypeStruct((N_HEADS, 2, t, t), F32),
        name="bias_tiles",
    )(rel_bias, bkt)


def _in_proj_kernel(x_ref, g_ref, w32_ref, pw_ref, ps_ref, *rest):
    later32_refs = rest[:N_LATER_WEIGHTS]
    ypool_ref, qt_ref, k_ref, vt_ref = rest[N_LATER_WEIGHTS:N_LATER_WEIGHTS + 4]
    later16_refs = rest[N_LATER_WEIGHTS + 4:2 * N_LATER_WEIGHTS + 4]
    ext_ref, w_ref = rest[2 * N_LATER_WEIGHTS + 4:]
    si = pl.program_id(1)
    tm = x_ref.shape[1]

    @pl.when((pl.program_id(0) == 0) & (si == 0))
    def _():
        for r in range(0, D_MODEL, WEIGHT_CAST_ROWS):
            w_ref[r:r + WEIGHT_CAST_ROWS, :] = w32_ref[r:r + WEIGHT_CAST_ROWS, :].astype(BF16)

    @pl.when(si == 0)
    def _():
        ext_ref[0:POOL_HALO, :] = jnp.zeros((POOL_HALO, POOL_WIDTH), F32)

    x = x_ref[0]
    ms = jnp.mean(x * x, axis=-1, keepdims=True)
    hn = (x * lax.rsqrt(ms + NORM_EPS) * g_ref[...]).astype(BF16)
    zp = jnp.dot(hn, w_ref[:, :POOL_WIDTH], preferred_element_type=F32)
    z = jnp.dot(hn, w_ref[:, POOL_WIDTH:], preferred_element_type=F32)

    for src_ref, dst_ref in zip(later32_refs, later16_refs):
        dst_ref[...] = src_ref[...].astype(BF16)

    ext_ref[POOL_HALO:, :] = zp
    tpos = si * tm + lax.broadcasted_iota(jnp.int32, (tm, 1), 0) + 1
    for gi, w in enumerate(POOL_WINDOWS):
        lo, hi = gi * POOL_GROUP_DIM, (gi + 1) * POOL_GROUP_DIM
        tot = ext_ref[:, lo:hi]
        step = 1
        while step < w:
            tot = tot + pltpu.roll(tot, step, axis=0)
            step *= 2
        tot = tot[POOL_HALO:, :]
        cnt = jnp.minimum(tpos, w).astype(F32)
        pooled = tot / cnt - zp[:, lo:hi]
        yg = jnp.dot(pooled.astype(BF16), pw_ref[gi].astype(BF16), preferred_element_type=F32)
        ypool_ref[0, :, lo:hi] = (yg * ps_ref[:, lo:hi]).astype(BF16)
    ext_ref[0:POOL_HALO, :] = zp[tm - POOL_HALO:, :]

    t = ATTN_TILE
    q0, k0, v0 = 0, ATTN_WIDTH, 2 * ATTN_WIDTH
    scale = LOG2E * HEAD_DIM ** -0.5
    for h in range(N_HEADS):
        zq = z[:, q0 + h * V_DIM:q0 + (h + 1) * V_DIM] * scale
        qt_ref[0, h * V_DIM:(h + 1) * V_DIM, :] = zq.T.astype(BF16)
        k_ref[0, h] = z[:, k0 + h * V_DIM:k0 + (h + 1) * V_DIM].astype(BF16)
        vt = z[:, v0 + h * V_DIM:v0 + (h + 1) * V_DIM].T.astype(BF16)
        for j in range(tm // t):
            vt_ref[0, h, j] = vt[:, j * t:(j + 1) * t]


def _in_proj(x, g, w_in, pool_w, pool_scale, later_weights):
    b, s, d = x.shape
    tm, t = IN_ROWS, ATTN_TILE
    steps = s // tm
    assert len(later_weights) == N_LATER_WEIGHTS
    slabs = [(w.shape[0] // (b * steps), w.shape[1]) for w in later_weights]
    slab_specs = [pl.BlockSpec(sl, lambda bi, si: (bi * steps + si, 0)) for sl in slabs]
    return pl.pallas_call(
        _in_proj_kernel,
        grid=(b, s // tm),
        in_specs=[
            pl.BlockSpec((1, tm, d), lambda bi, si: (bi, si, 0)),
            pl.BlockSpec((1, d), lambda bi, si: (0, 0)),
            pl.BlockSpec((d, IN_PROJ_WIDTH), lambda bi, si: (0, 0), pipeline_mode=pl.Buffered(1)),
            pl.BlockSpec((POOL_GROUPS, POOL_GROUP_DIM, POOL_GROUP_DIM), lambda bi, si: (0, 0, 0)),
            pl.BlockSpec((1, POOL_WIDTH), lambda bi, si: (0, 0)),
        ] + slab_specs,
        out_specs=[
            pl.BlockSpec((1, tm, POOL_WIDTH), lambda bi, si: (bi, si, 0)),
            pl.BlockSpec((1, ATTN_WIDTH, tm), lambda bi, si: (bi, 0, si)),
            pl.BlockSpec((1, N_HEADS, tm, V_DIM), lambda bi, si: (bi, 0, si, 0)),
            pl.BlockSpec((1, N_HEADS, tm // t, V_DIM, t), lambda bi, si: (bi, 0, si, 0, 0)),
        ] + slab_specs,
        out_shape=[
            jax.ShapeDtypeStruct((b, s, POOL_WIDTH), BF16),
            jax.ShapeDtypeStruct((b, ATTN_WIDTH, s), BF16),
            jax.ShapeDtypeStruct((b, N_HEADS, s, V_DIM), BF16),
            jax.ShapeDtypeStruct((b, N_HEADS, s // t, V_DIM, t), BF16),
        ] + [jax.ShapeDtypeStruct(w.shape, BF16) for w in later_weights],
        scratch_shapes=[pltpu.VMEM((POOL_HALO + tm, POOL_WIDTH), F32),
                        pltpu.VMEM((d, IN_PROJ_WIDTH), BF16)],
        compiler_params=pltpu.CompilerParams(
            dimension_semantics=("arbitrary", "arbitrary"),
            vmem_limit_bytes=VMEM_LIMIT_BYTES),
        name="in_proj",
    )(x, g, w_in, pool_w, pool_scale, *later_weights)


def _attn_kernel(rb_ref, qt_ref, qtn_ref, k_ref, vt_ref, bias_ref, lamv_ref, sg_ref, out_ref,
                 *scratch):
    t = ATTN_TILE
    m_refs, l_refs, smax_refs, acc_refs, qs_refs, s_refs = (
        scratch[i * N_HEADS:(i + 1) * N_HEADS] for i in range(6))
    qsn_refs = scratch[6 * N_HEADS:6 * N_HEADS + QK_LOOKAHEAD]

    def load_q(h, src, dst_refs):
        src_ref, col0 = src
        qt = src_ref[0, h * V_DIM:(h + 1) * V_DIM, col0:col0 + t]
        row = lax.broadcasted_iota(jnp.int32, qt.shape, 0)
        zero = jnp.zeros_like(qt)
        dst_refs[h][:, 0:t] = jnp.where(row < HEAD_DIM, qt, zero)
        dst_refs[h][:, t:2 * t] = jnp.where(row >= HEAD_DIM, qt, zero)

    far_bias = [rb_ref[NUM_BUCKETS - 1, h] * LOG2E for h in range(N_HEADS)]

    lamv = lamv_ref[...]
    lam = (jnp.exp(jnp.sum(lamv[0:1] * lamv[1:2], axis=1, keepdims=True))
           - jnp.exp(jnp.sum(lamv[2:3] * lamv[3:4], axis=1, keepdims=True))
           + LAMBDA_INIT)

    def qk(h, j, kind=None, q_refs=qs_refs):
        kj = k_ref[0, h, pl.ds(pl.multiple_of(j * t, t), t), :]
        s = jnp.dot(kj, q_refs[h][...], preferred_element_type=F32)
        if kind is not None:
            s = with_bias(s, h, kind)
        s_refs[h][...] = s
        smax_refs[h][...] = jnp.max(s, axis=0, keepdims=True)

    def with_bias(s, h, diag):
        b = tile_bias(h, diag)
        return s + jnp.concatenate([b, b], axis=1)

    def tile_bias(h, diag):
        n = t // BIAS_TILE
        pieces = {0: bias_ref[h, 0], 1: bias_ref[h, 1]}
        far = jnp.full((BIAS_TILE, BIAS_TILE), far_bias[h], F32)
        masked = jnp.full((BIAS_TILE, BIAS_TILE), NEG_INF, F32)
        rows = []
        for kb in range(n):
            d = [qb - kb + (0 if diag else n) for qb in range(n)]
            rows.append(jnp.concatenate([masked if x < 0 else pieces.get(x, far) for x in d], axis=1))
        return jnp.concatenate(rows, axis=0)

    def softmax_pv(h, j, pending_bias, off):
        s = s_refs[h][...]
        if pending_bias is None:
            smax = smax_refs[h][...]
        else:
            s = with_bias(s, h, pending_bias)
            smax = jnp.max(s, axis=0, keepdims=True)
        m_old = m_refs[h][...]
        m_new = jnp.maximum(m_old, smax + off)
        alpha = jnp.exp2(m_old - m_new)
        p = jnp.exp2(s - (m_new - off))
        l_refs[h][...] = alpha * l_refs[h][...] + jnp.sum(p, axis=0, keepdims=True)
        m_refs[h][...] = m_new
        r = jnp.dot(vt_ref[0, h, j], p.astype(BF16), preferred_element_type=F32)
        acc_refs[h][...] = alpha * acc_refs[h][...] + r

    def run(tiles, next_j, sub):
        stages = [(h, j, kind) for (j, kind) in tiles for h in range(N_HEADS)]
        for i, (h, j, kind) in enumerate(stages):
            ahead = i + QK_LOOKAHEAD
            if ahead < len(stages):
                qk(*stages[ahead])
            elif next_j is not None:
                qk(ahead - len(stages), next_j)
            else:
                nxt = (qt_ref, (sub + 1) * t) if sub + 1 < ATTN_TILES_PER_STEP else (qtn_ref, 0)
                load_q(ahead - len(stages), nxt, qsn_refs)
                qk(ahead - len(stages), 0, q_refs=qsn_refs)
            pending = kind if i < QK_LOOKAHEAD else None
            softmax_pv(h, j, pending, far_bias[h] if kind is None else 0.0)
            if kind is True:
                finalize(h, sub)

    def finalize(h, sub):
        o = acc_refs[h][...] * (1.0 / l_refs[h][...])
        o = o[:, :t] - lam * o[:, t:]
        ms = jnp.mean(o * o, axis=0, keepdims=True)
        y = (o * lax.rsqrt(ms + SUBLN_EPS)).T * (sg_ref[...] * (1.0 - LAMBDA_INIT))
        out_ref[0, sub * t:(sub + 1) * t, h * V_DIM:(h + 1) * V_DIM] = y.astype(BF16)

    def query_tile(sub):
        qi = pl.program_id(1) * ATTN_TILES_PER_STEP + sub

        if sub == 0:
            @pl.when(qi == 0)
            def _():
                for h in range(QK_LOOKAHEAD):
                    load_q(h, (qt_ref, 0), qsn_refs)
                    qk(h, 0, q_refs=qsn_refs)

        for h in range(QK_LOOKAHEAD):
            qs_refs[h][...] = qsn_refs[h][...]
        for h in range(QK_LOOKAHEAD, N_HEADS):
            load_q(h, (qt_ref, sub * t), qs_refs)
        for h in range(N_HEADS):
            m_refs[h][...] = jnp.full(m_refs[h].shape, -jnp.inf, F32)
            l_refs[h][...] = jnp.zeros(l_refs[h].shape, F32)
            acc_refs[h][...] = jnp.zeros(acc_refs[h].shape, F32)

        n_far = qi - 1

        def far_body(i, carry):
            j = FAR_PER_TRIP * i
            run([(j + d, None) for d in range(FAR_PER_TRIP)], j + FAR_PER_TRIP, sub)
            return carry

        lax.fori_loop(0, jnp.maximum(n_far, 0) // FAR_PER_TRIP, far_body, 0)

        if sub == 0:
            @pl.when(qi == 0)
            def _():
                run([(qi, True)], None, sub)

        nq = k_ref.shape[2] // t
        for rem in range(FAR_PER_TRIP):
            if not any((q - 1) % FAR_PER_TRIP == rem
                       for q in range(1, nq) if q % ATTN_TILES_PER_STEP == sub):
                continue

            @pl.when((qi >= 1) & (n_far % FAR_PER_TRIP == rem))
            def _():
                run([(qi - 1 - d, None) for d in range(rem, 0, -1)]
                    + [(qi - 1, False), (qi, True)], None, sub)

    for sub in range(ATTN_TILES_PER_STEP):
        query_tile(sub)


def _diff_attn(rel_bias, qt, k, vt, bias, lamv, subln_g):
    b, _, s = qt.shape
    t, tps = ATTN_TILE, ATTN_TILES_PER_STEP
    nq = s // t
    per_head = lambda shape, dtype: [pltpu.VMEM(shape, dtype) for _ in range(N_HEADS)]
    return pl.pallas_call(
        _attn_kernel,
        grid=(b, nq // tps),
        in_specs=[
            pl.BlockSpec(memory_space=pltpu.SMEM),
            pl.BlockSpec((1, ATTN_WIDTH, tps * t), lambda bi, qi: (bi, 0, qi)),
            pl.BlockSpec((1, QK_LOOKAHEAD * V_DIM, t),
                         lambda bi, qi: (bi, 0, jnp.minimum((qi + 1) * tps, nq - 1))),
            pl.BlockSpec((1, N_HEADS, s, V_DIM), lambda bi, qi: (bi, 0, 0, 0)),
            pl.BlockSpec((1, N_HEADS, nq, V_DIM, t), lambda bi, qi: (bi, 0, 0, 0, 0)),
            pl.BlockSpec((N_HEADS, 2, BIAS_TILE, BIAS_TILE), lambda bi, qi: (0, 0, 0, 0)),
            pl.BlockSpec((4, HEAD_DIM), lambda bi, qi: (0, 0)),
            pl.BlockSpec((1, V_DIM), lambda bi, qi: (0, 0)),
        ],
        out_specs=pl.BlockSpec((1, tps * t, ATTN_WIDTH), lambda bi, qi: (bi, qi, 0)),
        out_shape=jax.ShapeDtypeStruct((b, s, ATTN_WIDTH), BF16),
        scratch_shapes=(per_head((1, 2 * t), F32) + per_head((1, 2 * t), F32)
                        + per_head((1, 2 * t), F32)
                        + per_head((V_DIM, 2 * t), F32) + per_head((V_DIM, 2 * t), BF16)
                        + per_head((t, 2 * t), F32)
                        + [pltpu.VMEM((V_DIM, 2 * t), BF16) for _ in range(QK_LOOKAHEAD)]),
        compiler_params=pltpu.CompilerParams(
            dimension_semantics=("arbitrary", "arbitrary"),
            vmem_limit_bytes=VMEM_LIMIT_BYTES),
        name="diff_attn",
    )(rel_bias, qt, qt, k, vt, bias, lamv, subln_g)


def _ffn_kernel(x_ref, yp_ref, ya_ref, wo_ref, gf_ref, w1_ref, cw_ref, cb_ref,
                w2_ref, gl_ref, out_ref, carry_ref, gbuf_ref, act_ref, hn_ref):
    si = pl.program_id(1)
    tm = x_ref.shape[1]
    ck = FFN_CHUNK
    rows = tm // FFN_ROW_BLOCKS

    @pl.when(si == 0)
    def _():
        carry_ref[...] = jnp.zeros(carry_ref.shape, F32)

    for r0 in range(0, tm, rows):
        y = (jnp.dot(yp_ref[0, r0:r0 + rows, :], wo_ref[0:POOL_WIDTH, :], preferred_element_type=F32)
             + jnp.dot(ya_ref[0, r0:r0 + rows, :], wo_ref[POOL_WIDTH:, :], preferred_element_type=F32))
        x1 = x_ref[0, r0:r0 + rows, :] + y
        out_ref[0, r0:r0 + rows, :] = x1
        ms = jnp.mean(x1 * x1, axis=-1, keepdims=True)
        hn_ref[r0:r0 + rows, :] = (x1 * lax.rsqrt(ms + NORM_EPS) * gf_ref[...]).astype(BF16)

    for c in range(D_FF // ck):
        lo, hi = c * ck, (c + 1) * ck
        g = jnp.dot(hn_ref[...], w1_ref[:, lo:hi], preferred_element_type=F32)
        u = jnp.dot(hn_ref[...], w1_ref[:, D_FF + lo:D_FF + hi], preferred_element_type=F32)
        gbuf_ref[0:CONV_HALO, :] = carry_ref[:, lo:hi]
        gbuf_ref[CONV_HALO:, :] = g
        g1 = gbuf_ref[pl.ds(CONV_HALO - 1, tm), :]
        g2 = gbuf_ref[pl.ds(CONV_HALO - 2, tm), :]
        carry_ref[:, lo:hi] = g[tm - CONV_HALO:, :]
        cv = (cw_ref[0:1, lo:hi] * g2 + cw_ref[1:2, lo:hi] * g1 + cw_ref[2:3, lo:hi] * g
              + cb_ref[:, lo:hi])
        a = cv * (1.0 / (1.0 + jnp.exp(-cv))) * u
        act_ref[:, lo:hi] = a.astype(BF16)

    for r0 in range(0, tm, rows):
        x2 = out_ref[0, r0:r0 + rows, :] + jnp.dot(act_ref[r0:r0 + rows, :], w2_ref[...],
                                                   preferred_element_type=F32)
        ms2 = jnp.mean(x2 * x2, axis=-1, keepdims=True)
        out_ref[0, r0:r0 + rows, :] = x2 * lax.rsqrt(ms2 + NORM_EPS) * gl_ref[...]


def _out_ffn(x, y_pool, y_attn, w_out, g_ffn, w1, conv_w, conv_b, w2, g_final):
    b, s, d = x.shape
    tm = FFN_ROWS
    const = lambda bi, si: (0, 0)
    resident = functools.partial(pl.BlockSpec, index_map=const, pipeline_mode=pl.Buffered(1))
    return pl.pallas_call(
        _ffn_kernel,
        grid=(b, s // tm),
        in_specs=[
            pl.BlockSpec((1, tm, d), lambda bi, si: (bi, si, 0)),
            pl.BlockSpec((1, tm, POOL_WIDTH), lambda bi, si: (bi, si, 0)),
            pl.BlockSpec((1, tm, ATTN_WIDTH), lambda bi, si: (bi, si, 0)),
            resident((d, d)),
            resident((1, d)),
            resident((d, 2 * D_FF)),
            resident((3, D_FF)),
            resident((1, D_FF)),
            resident((D_FF, d)),
            resident((1, d)),
        ],
        out_specs=pl.BlockSpec((1, tm, d), lambda bi, si: (bi, si, 0)),
        out_shape=jax.ShapeDtypeStruct((b, s, d), F32),
        scratch_shapes=[pltpu.VMEM((CONV_HALO, D_FF), F32),
                        pltpu.VMEM((CONV_HALO + tm, FFN_CHUNK), F32),
                        pltpu.VMEM((tm, D_FF), BF16),
                        pltpu.VMEM((tm, d), BF16)],
        compiler_params=pltpu.CompilerParams(
            dimension_semantics=("arbitrary", "arbitrary"),
            vmem_limit_bytes=VMEM_LIMIT_BYTES),
        name="out_ffn",
    )(x, y_pool, y_attn, w_out, g_ffn, w1, conv_w, conv_b, w2, g_final)


def kernel(x, norm_mix_g, w_in, pool_w, pool_scale, lambda_q1, lambda_k1, lambda_q2, lambda_k2,
           subln_g, rel_bias, w_out, norm_ffn_g, ffn_w_in, ffn_conv_w, ffn_conv_b, ffn_w_out,
           norm_final_g):
    bias = _bias_tiles(rel_bias)
    y_pool, qt, k, vt, w_out16, w1_16, w2_16 = _in_proj(
        x, norm_mix_g, w_in[0], pool_w[0], pool_scale, (w_out[0], ffn_w_in[0], ffn_w_out[0]))
    lamv = jnp.concatenate([lambda_q1, lambda_k1, lambda_q2, lambda_k2], axis=0)
    y_attn = _diff_attn(rel_bias, qt, k, vt, bias, lamv, subln_g)
    return _out_ffn(x, y_pool, y_attn, w_out16, norm_ffn_g, w1_16, ffn_conv_w[0], ffn_conv_b,
                    w2_16, norm_final_g.reshape(1, D_MODEL))
```

```python
import functools
import math

import numpy as np
import jax
import jax.numpy as jnp
from jax import lax
from jax.experimental import pallas as pl
from jax.experimental.pallas import tpu as pltpu

D_MODEL = 1024
POOL_WIDTH = 512
POOL_GROUPS = 4
POOL_GROUP_DIM = 128
POOL_WINDOWS = (2, 4, 8, 16)
POOL_HALO = 16
ATTN_WIDTH = 512
N_HEADS = 4
HEAD_DIM = 64
V_DIM = 128
IN_PROJ_WIDTH = 2048
NUM_BUCKETS = 32
MAX_DISTANCE = 128
D_FF = 2816
NORM_EPS = 1e-6
SUBLN_EPS = 1e-5
NEG_INF = -1e30
LAMBDA_INIT = 0.8 - 0.6 * math.exp(-0.3 * 0)
LOG2E = math.log2(math.e)

ATTN_TILE = 256
BIAS_TILE = 128
QK_LOOKAHEAD = 2
FAR_PER_TRIP = 4
ATTN_TILES_PER_STEP = 2
IN_ROWS = 1024
IN_ROW_BLOCKS = 2
FFN_ROWS = 1024
FFN_CHUNK = 256
FFN_ROW_BLOCKS = 4
CONV_HALO = 8
WEIGHT_CAST_ROWS = 128
N_LATER_WEIGHTS = 3
VMEM_LIMIT_BYTES = 56 * 1024 * 1024

F32 = jnp.float32
BF16 = jnp.bfloat16


def _bucket_tiles(tile):
    k = np.arange(tile)[:, None]
    q = np.arange(tile)[None, :]
    out = []
    for offset in (0, tile):
        n = q - k + offset
        max_exact = NUM_BUCKETS // 2
        nf = np.maximum(n, 1).astype(np.float64)
        large = max_exact + (np.log(nf / max_exact) / math.log(MAX_DISTANCE / max_exact)
                             * (NUM_BUCKETS - max_exact)).astype(np.int64)
        large = np.minimum(large, NUM_BUCKETS - 1)
        b = np.where(n < max_exact, n, large)
        out.append(np.where(n < 0, -1, b))
    return np.stack(out).astype(np.int32)


def _in_proj_kernel(x_ref, g_ref, w32_ref, pw_ref, ps_ref, *rest):
    later32_refs = rest[:N_LATER_WEIGHTS]
    ypool_ref, qt_ref, k_ref, vt_ref = rest[N_LATER_WEIGHTS:N_LATER_WEIGHTS + 4]
    later16_refs = rest[N_LATER_WEIGHTS + 4:2 * N_LATER_WEIGHTS + 4]
    ext_ref, w_ref, hn_ref = rest[2 * N_LATER_WEIGHTS + 4:]
    si = pl.program_id(1)
    tm = x_ref.shape[1]

    @pl.when((pl.program_id(0) == 0) & (si == 0))
    def _():
        for r in range(0, D_MODEL, WEIGHT_CAST_ROWS):
            w_ref[r:r + WEIGHT_CAST_ROWS, :] = w32_ref[r:r + WEIGHT_CAST_ROWS, :].astype(BF16)

    @pl.when(si == 0)
    def _():
        ext_ref[0:POOL_HALO, :] = jnp.zeros((POOL_HALO, POOL_WIDTH), F32)

    rows = tm // IN_ROW_BLOCKS
    zp_blocks = []
    for r0 in range(0, tm, rows):
        x = x_ref[0, r0:r0 + rows, :]
        ms = jnp.mean(x * x, axis=-1, keepdims=True)
        hn = (x * lax.rsqrt(ms + NORM_EPS) * g_ref[...]).astype(BF16)
        hn_ref[r0:r0 + rows, :] = hn
        zp_blocks.append(jnp.dot(hn, w_ref[:, :POOL_WIDTH], preferred_element_type=F32))
    zp = jnp.concatenate(zp_blocks, axis=0)
    z = jnp.dot(hn_ref[...], w_ref[:, POOL_WIDTH:], preferred_element_type=F32)

    for src_ref, dst_ref in zip(later32_refs, later16_refs):
        dst_ref[...] = src_ref[...].astype(BF16)

    ext_ref[POOL_HALO:, :] = zp
    tpos = si * tm + lax.broadcasted_iota(jnp.int32, (tm, 1), 0) + 1
    for gi, w in enumerate(POOL_WINDOWS):
        lo, hi = gi * POOL_GROUP_DIM, (gi + 1) * POOL_GROUP_DIM
        tot = ext_ref[:, lo:hi]
        step = 1
        while step < w:
            tot = tot + pltpu.roll(tot, step, axis=0)
            step *= 2
        tot = tot[POOL_HALO:, :]
        cnt = jnp.minimum(tpos, w).astype(F32)
        pooled = tot / cnt - zp[:, lo:hi]
        yg = jnp.dot(pooled.astype(BF16), pw_ref[gi].astype(BF16), preferred_element_type=F32)
        ypool_ref[0, :, lo:hi] = (yg * ps_ref[:, lo:hi]).astype(BF16)
    ext_ref[0:POOL_HALO, :] = zp[tm - POOL_HALO:, :]

    t = ATTN_TILE
    q0, k0, v0 = 0, ATTN_WIDTH, 2 * ATTN_WIDTH
    scale = LOG2E * HEAD_DIM ** -0.5
    for h in range(N_HEADS):
        zq = z[:, q0 + h * V_DIM:q0 + (h + 1) * V_DIM] * scale
        qt_ref[0, h * V_DIM:(h + 1) * V_DIM, :] = zq.T.astype(BF16)
        k_ref[0, h] = z[:, k0 + h * V_DIM:k0 + (h + 1) * V_DIM].astype(BF16)
        vt = z[:, v0 + h * V_DIM:v0 + (h + 1) * V_DIM].T.astype(BF16)
        for j in range(tm // t):
            vt_ref[0, h, j] = vt[:, j * t:(j + 1) * t]


def _in_proj(x, g, w_in, pool_w, pool_scale, later_weights):
    b, s, d = x.shape
    tm, t = IN_ROWS, ATTN_TILE
    steps = s // tm
    assert len(later_weights) == N_LATER_WEIGHTS
    slabs = [(w.shape[0] // (b * steps), w.shape[1]) for w in later_weights]
    slab_specs = [pl.BlockSpec(sl, lambda bi, si: (bi * steps + si, 0)) for sl in slabs]
    return pl.pallas_call(
        _in_proj_kernel,
        grid=(b, s // tm),
        in_specs=[
            pl.BlockSpec((1, tm, d), lambda bi, si: (bi, si, 0)),
            pl.BlockSpec((1, d), lambda bi, si: (0, 0)),
            pl.BlockSpec((d, IN_PROJ_WIDTH), lambda bi, si: (0, 0), pipeline_mode=pl.Buffered(1)),
            pl.BlockSpec((POOL_GROUPS, POOL_GROUP_DIM, POOL_GROUP_DIM), lambda bi, si: (0, 0, 0)),
            pl.BlockSpec((1, POOL_WIDTH), lambda bi, si: (0, 0)),
        ] + slab_specs,
        out_specs=[
            pl.BlockSpec((1, tm, POOL_WIDTH), lambda bi, si: (bi, si, 0)),
            pl.BlockSpec((1, ATTN_WIDTH, tm), lambda bi, si: (bi, 0, si)),
            pl.BlockSpec((1, N_HEADS, tm, V_DIM), lambda bi, si: (bi, 0, si, 0)),
            pl.BlockSpec((1, N_HEADS, tm // t, V_DIM, t), lambda bi, si: (bi, 0, si, 0, 0)),
        ] + slab_specs,
        out_shape=[
            jax.ShapeDtypeStruct((b, s, POOL_WIDTH), BF16),
            jax.ShapeDtypeStruct((b, ATTN_WIDTH, s), BF16),
            jax.ShapeDtypeStruct((b, N_HEADS, s, V_DIM), BF16),
            jax.ShapeDtypeStruct((b, N_HEADS, s // t, V_DIM, t), BF16),
        ] + [jax.ShapeDtypeStruct(w.shape, BF16) for w in later_weights],
        scratch_shapes=[pltpu.VMEM((POOL_HALO + tm, POOL_WIDTH), F32),
                        pltpu.VMEM((d, IN_PROJ_WIDTH), BF16),
                        pltpu.VMEM((tm, d), BF16)],
        compiler_params=pltpu.CompilerParams(
            dimension_semantics=("arbitrary", "arbitrary"),
            vmem_limit_bytes=VMEM_LIMIT_BYTES),
        name="in_proj",
    )(x, g, w_in, pool_w, pool_scale, *later_weights)


def _attn_kernel(rb_ref, qt_ref, qtn_ref, k_ref, vt_ref, bkt_ref, lq1_ref, lk1_ref, lq2_ref,
                 lk2_ref, sg_ref, out_ref, *scratch):
    t = ATTN_TILE
    m_refs, l_refs, smax_refs, acc_refs, qs_refs, s_refs = (
        scratch[i * N_HEADS:(i + 1) * N_HEADS] for i in range(6))
    qsn_refs = scratch[6 * N_HEADS:6 * N_HEADS + QK_LOOKAHEAD]
    bias_ref = scratch[6 * N_HEADS + QK_LOOKAHEAD]

    @pl.when((pl.program_id(0) == 0) & (pl.program_id(1) == 0))
    def _():
        bkt = bkt_ref[...]
        for h in range(N_HEADS):
            def body(i, acc):
                return jnp.where(bkt == i, rb_ref[i, h] * LOG2E, acc)

            bias_ref[h] = lax.fori_loop(0, NUM_BUCKETS, body, jnp.full(bkt.shape, NEG_INF, F32))

    def load_q(h, src, dst_refs):
        src_ref, col0 = src
        qt = src_ref[0, h * V_DIM:(h + 1) * V_DIM, col0:col0 + t]
        row = lax.broadcasted_iota(jnp.int32, qt.shape, 0)
        zero = jnp.zeros_like(qt)
        dst_refs[h][:, 0:t] = jnp.where(row < HEAD_DIM, qt, zero)
        dst_refs[h][:, t:2 * t] = jnp.where(row >= HEAD_DIM, qt, zero)

    far_bias = [rb_ref[NUM_BUCKETS - 1, h] * LOG2E for h in range(N_HEADS)]

    lam = (jnp.exp(jnp.sum(lq1_ref[...] * lk1_ref[...], axis=1, keepdims=True))
           - jnp.exp(jnp.sum(lq2_ref[...] * lk2_ref[...], axis=1, keepdims=True))
           + LAMBDA_INIT)

    def qk(h, j, kind=None, q_refs=qs_refs):
        kj = k_ref[0, h, pl.ds(pl.multiple_of(j * t, t), t), :]
        s = jnp.dot(kj, q_refs[h][...], preferred_element_type=F32)
        if kind is not None:
            s = with_bias(s, h, kind)
        s_refs[h][...] = s
        smax_refs[h][...] = jnp.max(s, axis=0, keepdims=True)

    def with_bias(s, h, diag):
        b = tile_bias(h, diag)
        return s + jnp.concatenate([b, b], axis=1)

    def tile_bias(h, diag):
        n = t // BIAS_TILE
        pieces = {0: bias_ref[h, 0], 1: bias_ref[h, 1]}
        far = jnp.full((BIAS_TILE, BIAS_TILE), far_bias[h], F32)
        masked = jnp.full((BIAS_TILE, BIAS_TILE), NEG_INF, F32)
        rows = []
        for kb in range(n):
            d = [qb - kb + (0 if diag else n) for qb in range(n)]
            rows.append(jnp.concatenate([masked if x < 0 else pieces.get(x, far) for x in d], axis=1))
        return jnp.concatenate(rows, axis=0)

    def softmax_pv(h, j, pending_bias, off):
        s = s_refs[h][...]
        if pending_bias is None:
            smax = smax_refs[h][...]
        else:
            s = with_bias(s, h, pending_bias)
            smax = jnp.max(s, axis=0, keepdims=True)
        m_old = m_refs[h][...]
        m_new = jnp.maximum(m_old, smax + off)
        alpha = jnp.exp2(m_old - m_new)
        p = jnp.exp2(s - (m_new - off))
        l_refs[h][...] = alpha * l_refs[h][...] + jnp.sum(p, axis=0, keepdims=True)
        m_refs[h][...] = m_new
        r = jnp.dot(vt_ref[0, h, j], p.astype(BF16), preferred_element_type=F32)
        acc_refs[h][...] = alpha * acc_refs[h][...] + r

    def run(tiles, next_j, sub):
        stages = [(h, j, kind) for (j, kind) in tiles for h in range(N_HEADS)]
        for i, (h, j, kind) in enumerate(stages):
            ahead = i + QK_LOOKAHEAD
            if ahead < len(stages):
                qk(*stages[ahead])
            elif next_j is not None:
                qk(ahead - len(stages), next_j)
            else:
                nxt = (qt_ref, (sub + 1) * t) if sub + 1 < ATTN_TILES_PER_STEP else (qtn_ref, 0)
                load_q(ahead - len(stages), nxt, qsn_refs)
                qk(ahead - len(stages), 0, q_refs=qsn_refs)
            pending = kind if i < QK_LOOKAHEAD else None
            softmax_pv(h, j, pending, far_bias[h] if kind is None else 0.0)
            if kind is True:
                finalize(h, sub)

    def finalize(h, sub):
        o = acc_refs[h][...] * (1.0 / l_refs[h][...])
        o = o[:, :t] - lam * o[:, t:]
        ms = jnp.mean(o * o, axis=0, keepdims=True)
        y = (o * lax.rsqrt(ms + SUBLN_EPS)).T * (sg_ref[...] * (1.0 - LAMBDA_INIT))
        out_ref[0, sub * t:(sub + 1) * t, h * V_DIM:(h + 1) * V_DIM] = y.astype(BF16)

    def query_tile(sub):
        qi = pl.program_id(1) * ATTN_TILES_PER_STEP + sub

        if sub == 0:
            @pl.when(qi == 0)
            def _():
                for h in range(QK_LOOKAHEAD):
                    load_q(h, (qt_ref, 0), qsn_refs)
                    qk(h, 0, q_refs=qsn_refs)

        for h in range(QK_LOOKAHEAD):
            qs_refs[h][...] = qsn_refs[h][...]
        for h in range(QK_LOOKAHEAD, N_HEADS):
            load_q(h, (qt_ref, sub * t), qs_refs)
        for h in range(N_HEADS):
            m_refs[h][...] = jnp.full(m_refs[h].shape, -jnp.inf, F32)
            l_refs[h][...] = jnp.zeros(l_refs[h].shape, F32)
            acc_refs[h][...] = jnp.zeros(acc_refs[h].shape, F32)

        n_far = qi - 1

        def far_body(i, carry):
            j = FAR_PER_TRIP * i
            run([(j + d, None) for d in range(FAR_PER_TRIP)], j + FAR_PER_TRIP, sub)
            return carry

        lax.fori_loop(0, jnp.maximum(n_far, 0) // FAR_PER_TRIP, far_body, 0)

        if sub == 0:
            @pl.when(qi == 0)
            def _():
                run([(qi, True)], None, sub)

        nq = k_ref.shape[2] // t
        for rem in range(FAR_PER_TRIP):
            if not any((q - 1) % FAR_PER_TRIP == rem
                       for q in range(1, nq) if q % ATTN_TILES_PER_STEP == sub):
                continue

            @pl.when((qi >= 1) & (n_far % FAR_PER_TRIP == rem))
            def _():
                run([(qi - 1 - d, None) for d in range(rem, 0, -1)]
                    + [(qi - 1, False), (qi, True)], None, sub)

    for sub in range(ATTN_TILES_PER_STEP):
        query_tile(sub)


def _diff_attn(rel_bias, qt, k, vt, lambdas, subln_g):
    b, _, s = qt.shape
    t, tps = ATTN_TILE, ATTN_TILES_PER_STEP
    nq = s // t
    per_head = lambda shape, dtype: [pltpu.VMEM(shape, dtype) for _ in range(N_HEADS)]
    return pl.pallas_call(
        _attn_kernel,
        grid=(b, nq // tps),
        in_specs=[
            pl.BlockSpec(memory_space=pltpu.SMEM),
            pl.BlockSpec((1, ATTN_WIDTH, tps * t), lambda bi, qi: (bi, 0, qi)),
            pl.BlockSpec((1, QK_LOOKAHEAD * V_DIM, t),
                         lambda bi, qi: (bi, 0, jnp.minimum((qi + 1) * tps, nq - 1))),
            pl.BlockSpec((1, N_HEADS, s, V_DIM), lambda bi, qi: (bi, 0, 0, 0)),
            pl.BlockSpec((1, N_HEADS, nq, V_DIM, t), lambda bi, qi: (bi, 0, 0, 0, 0)),
            pl.BlockSpec((2, BIAS_TILE, BIAS_TILE), lambda bi, qi: (0, 0, 0)),
        ] + [pl.BlockSpec((1, HEAD_DIM), lambda bi, qi: (0, 0)) for _ in lambdas] + [
            pl.BlockSpec((1, V_DIM), lambda bi, qi: (0, 0)),
        ],
        out_specs=pl.BlockSpec((1, tps * t, ATTN_WIDTH), lambda bi, qi: (bi, qi, 0)),
        out_shape=jax.ShapeDtypeStruct((b, s, ATTN_WIDTH), BF16),
        scratch_shapes=(per_head((1, 2 * t), F32) + per_head((1, 2 * t), F32)
                        + per_head((1, 2 * t), F32)
                        + per_head((V_DIM, 2 * t), F32) + per_head((V_DIM, 2 * t), BF16)
                        + per_head((t, 2 * t), F32)
                        + [pltpu.VMEM((V_DIM, 2 * t), BF16) for _ in range(QK_LOOKAHEAD)]
                        + [pltpu.VMEM((N_HEADS, 2, BIAS_TILE, BIAS_TILE), F32)]),
        compiler_params=pltpu.CompilerParams(
            dimension_semantics=("arbitrary", "arbitrary"),
            vmem_limit_bytes=VMEM_LIMIT_BYTES),
        name="diff_attn",
    )(rel_bias, qt, qt, k, vt, jnp.asarray(_bucket_tiles(BIAS_TILE)), *lambdas, subln_g)


def _ffn_kernel(x_ref, yp_ref, ya_ref, wo_ref, gf_ref, w1_ref, cw_ref, cb_ref,
                w2_ref, gl_ref, out_ref, carry_ref, gbuf_ref, act_ref, hn_ref):
    si = pl.program_id(1)
    tm = x_ref.shape[1]
    ck = FFN_CHUNK
    rows = tm // FFN_ROW_BLOCKS

    @pl.when(si == 0)
    def _():
        carry_ref[...] = jnp.zeros(carry_ref.shape, F32)

    for r0 in range(0, tm, rows):
        y = (jnp.dot(yp_ref[0, r0:r0 + rows, :], wo_ref[0:POOL_WIDTH, :], preferred_element_type=F32)
             + jnp.dot(ya_ref[0, r0:r0 + rows, :], wo_ref[POOL_WIDTH:, :], preferred_element_type=F32))
        x1 = x_ref[0, r0:r0 + rows, :] + y
        out_ref[0, r0:r0 + rows, :] = x1
        ms = jnp.mean(x1 * x1, axis=-1, keepdims=True)
        hn_ref[r0:r0 + rows, :] = (x1 * lax.rsqrt(ms + NORM_EPS) * gf_ref[...]).astype(BF16)

    for c in range(D_FF // ck):
        lo, hi = c * ck, (c + 1) * ck
        g = jnp.dot(hn_ref[...], w1_ref[:, lo:hi], preferred_element_type=F32)
        u = jnp.dot(hn_ref[...], w1_ref[:, D_FF + lo:D_FF + hi], preferred_element_type=F32)
        gbuf_ref[0:CONV_HALO, :] = carry_ref[:, lo:hi]
        gbuf_ref[CONV_HALO:, :] = g
        g1 = gbuf_ref[pl.ds(CONV_HALO - 1, tm), :]
        g2 = gbuf_ref[pl.ds(CONV_HALO - 2, tm), :]
        carry_ref[:, lo:hi] = g[tm - CONV_HALO:, :]
        cv = (cw_ref[0:1, lo:hi] * g2 + cw_ref[1:2, lo:hi] * g1 + cw_ref[2:3, lo:hi] * g
              + cb_ref[:, lo:hi])
        a = cv * (1.0 / (1.0 + jnp.exp(-cv))) * u
        act_ref[:, lo:hi] = a.astype(BF16)

    for r0 in range(0, tm, rows):
        x2 = out_ref[0, r0:r0 + rows, :] + jnp.dot(act_ref[r0:r0 + rows, :], w2_ref[...],
                                                   preferred_element_type=F32)
        ms2 = jnp.mean(x2 * x2, axis=-1, keepdims=True)
        out_ref[0, r0:r0 + rows, :] = x2 * lax.rsqrt(ms2 + NORM_EPS) * gl_ref[...]


def _out_ffn(x, y_pool, y_attn, w_out, g_ffn, w1, conv_w, conv_b, w2, g_final):
    b, s, d = x.shape
    tm = FFN_ROWS
    const = lambda bi, si: (0, 0)
    resident = functools.partial(pl.BlockSpec, index_map=const, pipeline_mode=pl.Buffered(1))
    return pl.pallas_call(
        _ffn_kernel,
        grid=(b, s // tm),
        in_specs=[
            pl.BlockSpec((1, tm, d), lambda bi, si: (bi, si, 0)),
            pl.BlockSpec((1, tm, POOL_WIDTH), lambda bi, si: (bi, si, 0)),
            pl.BlockSpec((1, tm, ATTN_WIDTH), lambda bi, si: (bi, si, 0)),
            resident((d, d)),
            resident((1, d)),
            resident((d, 2 * D_FF)),
            resident((3, D_FF)),
            resident((1, D_FF)),
            resident((D_FF, d)),
            resident((1, d)),
        ],
        out_specs=pl.BlockSpec((1, tm, d), lambda bi, si: (bi, si, 0)),
        out_shape=jax.ShapeDtypeStruct((b, s, d), F32),
        scratch_shapes=[pltpu.VMEM((CONV_HALO, D_FF), F32),
                        pltpu.VMEM((CONV_HALO + tm, FFN_CHUNK), F32),
                        pltpu.VMEM((tm, D_FF), BF16),
                        pltpu.VMEM((tm, d), BF16)],
        compiler_params=pltpu.CompilerParams(
            dimension_semantics=("arbitrary", "arbitrary"),
            vmem_limit_bytes=VMEM_LIMIT_BYTES),
        name="out_ffn",
    )(x, y_pool, y_attn, w_out, g_ffn, w1, conv_w, conv_b, w2, g_final)


def kernel(x, norm_mix_g, w_in, pool_w, pool_scale, lambda_q1, lambda_k1, lambda_q2, lambda_k2,
           subln_g, rel_bias, w_out, norm_ffn_g, ffn_w_in, ffn_conv_w, ffn_conv_b, ffn_w_out,
           norm_final_g):
    y_pool, qt, k, vt, w_out16, w1_16, w2_16 = _in_proj(
        x, norm_mix_g, w_in[0], pool_w[0], pool_scale, (w_out[0], ffn_w_in[0], ffn_w_out[0]))
    y_attn = _diff_attn(rel_bias, qt, k, vt, (lambda_q1, lambda_k1, lambda_q2, lambda_k2), subln_g)
    return _out_ffn(x, y_pool, y_attn, w_out16, norm_ffn_g, w1_16, ffn_conv_w[0], ffn_conv_b,
                    w2_16, norm_final_g.reshape(1, D_MODEL))
```

```python
import functools
import math

import numpy as np
import jax
import jax.numpy as jnp
from jax import lax
from jax.experimental import pallas as pl
from jax.experimental.pallas import tpu as pltpu

D_MODEL = 1024
POOL_WIDTH = 512
POOL_GROUPS = 4
POOL_GROUP_DIM = 128
POOL_WINDOWS = (2, 4, 8, 16)
POOL_HALO = 16
ATTN_WIDTH = 512
N_HEADS = 4
HEAD_DIM = 64
V_DIM = 128
IN_PROJ_WIDTH = 2048
NUM_BUCKETS = 32
MAX_DISTANCE = 128
D_FF = 2816
NORM_EPS = 1e-6
SUBLN_EPS = 1e-5
NEG_INF = -1e30
LAMBDA_INIT = 0.8 - 0.6 * math.exp(-0.3 * 0)
LOG2E = math.log2(math.e)

ATTN_TILE = 256
BIAS_TILE = 128
QK_LOOKAHEAD = 2
FAR_PER_TRIP = 4
NEXT_SPAN = 2
ATTN_TILES_PER_STEP = 2
IN_ROWS = 1024
IN_ROW_BLOCKS = 2
FFN_ROWS = 1024
FFN_CHUNK = 256
FFN_ROW_BLOCKS = 4
CONV_HALO = 8
WEIGHT_CAST_ROWS = 128
N_LATER_WEIGHTS = 3
VMEM_LIMIT_BYTES = 56 * 1024 * 1024

F32 = jnp.float32
BF16 = jnp.bfloat16


def _bucket_tiles(tile):
    k = np.arange(tile)[:, None]
    q = np.arange(tile)[None, :]
    out = []
    for offset in (0, tile):
        n = q - k + offset
        max_exact = NUM_BUCKETS // 2
        nf = np.maximum(n, 1).astype(np.float64)
        large = max_exact + (np.log(nf / max_exact) / math.log(MAX_DISTANCE / max_exact)
                             * (NUM_BUCKETS - max_exact)).astype(np.int64)
        large = np.minimum(large, NUM_BUCKETS - 1)
        b = np.where(n < max_exact, n, large)
        out.append(np.where(n < 0, -1, b))
    return np.stack(out).astype(np.int32)


def _in_proj_kernel(x_ref, g_ref, w32_ref, pw_ref, ps_ref, *rest):
    later32_refs = rest[:N_LATER_WEIGHTS]
    ypool_ref, qt_ref, k_ref, vt_ref = rest[N_LATER_WEIGHTS:N_LATER_WEIGHTS + 4]
    later16_refs = rest[N_LATER_WEIGHTS + 4:2 * N_LATER_WEIGHTS + 4]
    ext_ref, w_ref, hn_ref = rest[2 * N_LATER_WEIGHTS + 4:]
    si = pl.program_id(1)
    tm = x_ref.shape[1]

    @pl.when((pl.program_id(0) == 0) & (si == 0))
    def _():
        for r in range(0, D_MODEL, WEIGHT_CAST_ROWS):
            w_ref[r:r + WEIGHT_CAST_ROWS, :] = w32_ref[r:r + WEIGHT_CAST_ROWS, :].astype(BF16)

    @pl.when(si == 0)
    def _():
        ext_ref[0:POOL_HALO, :] = jnp.zeros((POOL_HALO, POOL_WIDTH), F32)

    rows = tm // IN_ROW_BLOCKS
    zp_blocks = []
    for r0 in range(0, tm, rows):
        x = x_ref[0, r0:r0 + rows, :]
        ms = jnp.mean(x * x, axis=-1, keepdims=True)
        hn = (x * lax.rsqrt(ms + NORM_EPS) * g_ref[...]).astype(BF16)
        hn_ref[r0:r0 + rows, :] = hn
        zp_blocks.append(jnp.dot(hn, w_ref[:, :POOL_WIDTH], preferred_element_type=F32))
    zp = jnp.concatenate(zp_blocks, axis=0)
    z = jnp.dot(hn_ref[...], w_ref[:, POOL_WIDTH:], preferred_element_type=F32)

    for src_ref, dst_ref in zip(later32_refs, later16_refs):
        dst_ref[...] = src_ref[...].astype(BF16)

    ext_ref[POOL_HALO:, :] = zp
    tpos = si * tm + lax.broadcasted_iota(jnp.int32, (tm, 1), 0) + 1
    for gi, w in enumerate(POOL_WINDOWS):
        lo, hi = gi * POOL_GROUP_DIM, (gi + 1) * POOL_GROUP_DIM
        tot = ext_ref[:, lo:hi]
        step = 1
        while step < w:
            tot = tot + pltpu.roll(tot, step, axis=0)
            step *= 2
        tot = tot[POOL_HALO:, :]
        cnt = jnp.minimum(tpos, w).astype(F32)
        pooled = tot / cnt - zp[:, lo:hi]
        yg = jnp.dot(pooled.astype(BF16), pw_ref[gi].astype(BF16), preferred_element_type=F32)
        ypool_ref[0, :, lo:hi] = (yg * ps_ref[:, lo:hi]).astype(BF16)
    ext_ref[0:POOL_HALO, :] = zp[tm - POOL_HALO:, :]

    t = ATTN_TILE
    q0, k0, v0 = 0, ATTN_WIDTH, 2 * ATTN_WIDTH
    scale = LOG2E * HEAD_DIM ** -0.5
    for h in range(N_HEADS):
        zq = z[:, q0 + h * V_DIM:q0 + (h + 1) * V_DIM] * scale
        qt_ref[0, h * V_DIM:(h + 1) * V_DIM, :] = zq.T.astype(BF16)
        k_ref[0, h] = z[:, k0 + h * V_DIM:k0 + (h + 1) * V_DIM].astype(BF16)
        vt = z[:, v0 + h * V_DIM:v0 + (h + 1) * V_DIM].T.astype(BF16)
        for j in range(tm // t):
            vt_ref[0, h, j] = vt[:, j * t:(j + 1) * t]


def _in_proj(x, g, w_in, pool_w, pool_scale, later_weights):
    b, s, d = x.shape
    tm, t = IN_ROWS, ATTN_TILE
    steps = s // tm
    assert len(later_weights) == N_LATER_WEIGHTS
    slabs = [(w.shape[0] // (b * steps), w.shape[1]) for w in later_weights]
    slab_specs = [pl.BlockSpec(sl, lambda bi, si: (bi * steps + si, 0)) for sl in slabs]
    return pl.pallas_call(
        _in_proj_kernel,
        grid=(b, s // tm),
        in_specs=[
            pl.BlockSpec((1, tm, d), lambda bi, si: (bi, si, 0)),
            pl.BlockSpec((1, d), lambda bi, si: (0, 0)),
            pl.BlockSpec((d, IN_PROJ_WIDTH), lambda bi, si: (0, 0), pipeline_mode=pl.Buffered(1)),
            pl.BlockSpec((POOL_GROUPS, POOL_GROUP_DIM, POOL_GROUP_DIM), lambda bi, si: (0, 0, 0)),
            pl.BlockSpec((1, POOL_WIDTH), lambda bi, si: (0, 0)),
        ] + slab_specs,
        out_specs=[
            pl.BlockSpec((1, tm, POOL_WIDTH), lambda bi, si: (bi, si, 0)),
            pl.BlockSpec((1, ATTN_WIDTH, tm), lambda bi, si: (bi, 0, si)),
            pl.BlockSpec((1, N_HEADS, tm, V_DIM), lambda bi, si: (bi, 0, si, 0)),
            pl.BlockSpec((1, N_HEADS, tm // t, V_DIM, t), lambda bi, si: (bi, 0, si, 0, 0)),
        ] + slab_specs,
        out_shape=[
            jax.ShapeDtypeStruct((b, s, POOL_WIDTH), BF16),
            jax.ShapeDtypeStruct((b, ATTN_WIDTH, s), BF16),
            jax.ShapeDtypeStruct((b, N_HEADS, s, V_DIM), BF16),
            jax.ShapeDtypeStruct((b, N_HEADS, s // t, V_DIM, t), BF16),
        ] + [jax.ShapeDtypeStruct(w.shape, BF16) for w in later_weights],
        scratch_shapes=[pltpu.VMEM((POOL_HALO + tm, POOL_WIDTH), F32),
                        pltpu.VMEM((d, IN_PROJ_WIDTH), BF16),
                        pltpu.VMEM((tm, d), BF16)],
        compiler_params=pltpu.CompilerParams(
            dimension_semantics=("arbitrary", "arbitrary"),
            vmem_limit_bytes=VMEM_LIMIT_BYTES),
        name="in_proj",
    )(x, g, w_in, pool_w, pool_scale, *later_weights)


def _attn_kernel(rb_ref, qt_ref, qtn_ref, k_ref, vt_ref, bkt_ref, lq1_ref, lk1_ref, lq2_ref,
                 lk2_ref, sg_ref, out_ref, *scratch):
    t = ATTN_TILE
    m_refs, l_refs, smax_refs, acc_refs, qs_refs, s_refs = (
        scratch[i * N_HEADS:(i + 1) * N_HEADS] for i in range(6))
    qsn_refs = scratch[6 * N_HEADS:6 * N_HEADS + QK_LOOKAHEAD]
    bias_ref = scratch[6 * N_HEADS + QK_LOOKAHEAD]

    @pl.when((pl.program_id(0) == 0) & (pl.program_id(1) == 0))
    def _():
        bkt = bkt_ref[...]
        for h in range(N_HEADS):
            def body(i, acc):
                return jnp.where(bkt == i, rb_ref[i, h] * LOG2E, acc)

            bias_ref[h] = lax.fori_loop(0, NUM_BUCKETS, body, jnp.full(bkt.shape, NEG_INF, F32))

    def load_q(h, src, dst_refs):
        src_ref, col0 = src
        qt = src_ref[0, h * V_DIM:(h + 1) * V_DIM, col0:col0 + t]
        row = lax.broadcasted_iota(jnp.int32, qt.shape, 0)
        zero = jnp.zeros_like(qt)
        dst_refs[h][:, 0:t] = jnp.where(row < HEAD_DIM, qt, zero)
        dst_refs[h][:, t:2 * t] = jnp.where(row >= HEAD_DIM, qt, zero)

    far_bias = [rb_ref[NUM_BUCKETS - 1, h] * LOG2E for h in range(N_HEADS)]

    lam = (jnp.exp(jnp.sum(lq1_ref[...] * lk1_ref[...], axis=1, keepdims=True))
           - jnp.exp(jnp.sum(lq2_ref[...] * lk2_ref[...], axis=1, keepdims=True))
           + LAMBDA_INIT)

    def qk(h, j, kinds, add_bias=True, q_refs=qs_refs):
        n = len(kinds) * t
        kj = k_ref[0, h, pl.ds(pl.multiple_of(j * t, t), n), :]
        s = jnp.dot(kj, q_refs[h][...], preferred_element_type=F32)
        if add_bias and has_bias(kinds):
            s = with_bias(s, h, kinds)
        s_refs[h][0:n, :] = s
        smax_refs[h][...] = jnp.max(s, axis=0, keepdims=True)

    def has_bias(kinds):
        return any(kd is not None for kd in kinds)

    def with_bias(s, h, kinds):
        b = jnp.concatenate([tile_bias(h, kd) for kd in kinds], axis=0)
        return s + jnp.concatenate([b, b], axis=1)

    def tile_bias(h, diag):
        n = t // BIAS_TILE
        pieces = {0: bias_ref[h, 0], 1: bias_ref[h, 1]}
        far = jnp.full((BIAS_TILE, BIAS_TILE), far_bias[h], F32)
        if diag is None:
            return jnp.full((t, t), far_bias[h], F32)
        masked = jnp.full((BIAS_TILE, BIAS_TILE), NEG_INF, F32)
        rows = []
        for kb in range(n):
            d = [qb - kb + (0 if diag else n) for qb in range(n)]
            rows.append(jnp.concatenate([masked if x < 0 else pieces.get(x, far) for x in d], axis=1))
        return jnp.concatenate(rows, axis=0)

    def softmax_pv(h, j, kinds, bias_pending):
        n = len(kinds) * t
        off = 0.0 if has_bias(kinds) else far_bias[h]
        s = s_refs[h][0:n, :]
        if bias_pending and has_bias(kinds):
            s = with_bias(s, h, kinds)
            smax = jnp.max(s, axis=0, keepdims=True)
        else:
            smax = smax_refs[h][...]
        m_old = m_refs[h][...]
        m_new = jnp.maximum(m_old, smax + off)
        alpha = jnp.exp2(m_old - m_new)
        p = jnp.exp2(s - (m_new - off))
        l_refs[h][...] = alpha * l_refs[h][...] + jnp.sum(p, axis=0, keepdims=True)
        m_refs[h][...] = m_new
        vt = jnp.concatenate([vt_ref[0, h, j + d] for d in range(len(kinds))], axis=1)
        r = jnp.dot(vt, p.astype(BF16), preferred_element_type=F32)
        acc_refs[h][...] = alpha * acc_refs[h][...] + r

    def run(groups, next_j, sub):
        stages = [(h, j, kinds) for (j, kinds) in groups for h in range(N_HEADS)]
        for i, (h, j, kinds) in enumerate(stages):
            ahead = i + QK_LOOKAHEAD
            if ahead < len(stages):
                qk(*stages[ahead])
            elif next_j is not None:
                qk(ahead - len(stages), next_j, (None,) * NEXT_SPAN, add_bias=False)
            else:
                nxt = (qt_ref, (sub + 1) * t) if sub + 1 < ATTN_TILES_PER_STEP else (qtn_ref, 0)
                load_q(ahead - len(stages), nxt, qsn_refs)
                qk(ahead - len(stages), 0, (None,) * NEXT_SPAN, add_bias=False, q_refs=qsn_refs)
            softmax_pv(h, j, kinds, bias_pending=i < QK_LOOKAHEAD)
            if kinds[-1] is True:
                finalize(h, sub)

    def finalize(h, sub):
        o = acc_refs[h][...] * (1.0 / l_refs[h][...])
        o = o[:, :t] - lam * o[:, t:]
        ms = jnp.mean(o * o, axis=0, keepdims=True)
        y = (o * lax.rsqrt(ms + SUBLN_EPS)).T * (sg_ref[...] * (1.0 - LAMBDA_INIT))
        out_ref[0, sub * t:(sub + 1) * t, h * V_DIM:(h + 1) * V_DIM] = y.astype(BF16)

    def query_tile(sub):
        qi = pl.program_id(1) * ATTN_TILES_PER_STEP + sub

        if sub == 0:
            @pl.when(qi == 0)
            def _():
                for h in range(QK_LOOKAHEAD):
                    load_q(h, (qt_ref, 0), qsn_refs)
                    qk(h, 0, (True,), add_bias=False, q_refs=qsn_refs)

        for h in range(QK_LOOKAHEAD):
            qs_refs[h][...] = qsn_refs[h][...]
        for h in range(QK_LOOKAHEAD, N_HEADS):
            load_q(h, (qt_ref, sub * t), qs_refs)
        for h in range(N_HEADS):
            m_refs[h][...] = jnp.full(m_refs[h].shape, -jnp.inf, F32)
            l_refs[h][...] = jnp.zeros(l_refs[h].shape, F32)
            acc_refs[h][...] = jnp.zeros(acc_refs[h].shape, F32)

        n_far = qi - 1

        def far_body(i, carry):
            j = FAR_PER_TRIP * i
            run([(j + d, (None, None)) for d in range(0, FAR_PER_TRIP, 2)], j + FAR_PER_TRIP, sub)
            return carry

        lax.fori_loop(0, jnp.maximum(n_far, 0) // FAR_PER_TRIP, far_body, 0)

        if sub == 0:
            @pl.when(qi == 0)
            def _():
                run([(qi, (True,))], None, sub)

        nq = k_ref.shape[2] // t
        for rem in range(FAR_PER_TRIP):
            if not any((q - 1) % FAR_PER_TRIP == rem
                       for q in range(1, nq) if q % ATTN_TILES_PER_STEP == sub):
                continue

            @pl.when((qi >= 1) & (n_far % FAR_PER_TRIP == rem))
            def _():
                kinds = (None,) * rem + (False, True)
                first = qi - 1 - rem
                run([(first + d, kinds[d:d + 2]) for d in range(0, len(kinds), 2)], None, sub)

    for sub in range(ATTN_TILES_PER_STEP):
        query_tile(sub)


def _diff_attn(rel_bias, qt, k, vt, lambdas, subln_g):
    b, _, s = qt.shape
    t, tps = ATTN_TILE, ATTN_TILES_PER_STEP
    nq = s // t
    per_head = lambda shape, dtype: [pltpu.VMEM(shape, dtype) for _ in range(N_HEADS)]
    return pl.pallas_call(
        _attn_kernel,
        grid=(b, nq // tps),
        in_specs=[
            pl.BlockSpec(memory_space=pltpu.SMEM),
            pl.BlockSpec((1, ATTN_WIDTH, tps * t), lambda bi, qi: (bi, 0, qi)),
            pl.BlockSpec((1, QK_LOOKAHEAD * V_DIM, t),
                         lambda bi, qi: (bi, 0, jnp.minimum((qi + 1) * tps, nq - 1))),
            pl.BlockSpec((1, N_HEADS, s, V_DIM), lambda bi, qi: (bi, 0, 0, 0)),
            pl.BlockSpec((1, N_HEADS, nq, V_DIM, t), lambda bi, qi: (bi, 0, 0, 0, 0)),
            pl.BlockSpec((2, BIAS_TILE, BIAS_TILE), lambda bi, qi: (0, 0, 0)),
        ] + [pl.BlockSpec((1, HEAD_DIM), lambda bi, qi: (0, 0)) for _ in lambdas] + [
            pl.BlockSpec((1, V_DIM), lambda bi, qi: (0, 0)),
        ],
        out_specs=pl.BlockSpec((1, tps * t, ATTN_WIDTH), lambda bi, qi: (bi, qi, 0)),
        out_shape=jax.ShapeDtypeStruct((b, s, ATTN_WIDTH), BF16),
        scratch_shapes=(per_head((1, 2 * t), F32) + per_head((1, 2 * t), F32)
                        + per_head((1, 2 * t), F32)
                        + per_head((V_DIM, 2 * t), F32) + per_head((V_DIM, 2 * t), BF16)
                        + per_head((NEXT_SPAN * t, 2 * t), F32)
                        + [pltpu.VMEM((V_DIM, 2 * t), BF16) for _ in range(QK_LOOKAHEAD)]
                        + [pltpu.VMEM((N_HEADS, 2, BIAS_TILE, BIAS_TILE), F32)]),
        compiler_params=pltpu.CompilerParams(
            dimension_semantics=("arbitrary", "arbitrary"),
            vmem_limit_bytes=VMEM_LIMIT_BYTES),
        name="diff_attn",
    )(rel_bias, qt, qt, k, vt, jnp.asarray(_bucket_tiles(BIAS_TILE)), *lambdas, subln_g)


def _ffn_kernel(x_ref, yp_ref, ya_ref, wo_ref, gf_ref, w1_ref, cw_ref, cb_ref,
                w2_ref, gl_ref, out_ref, carry_ref, gbuf_ref, act_ref, hn_ref):
    si = pl.program_id(1)
    tm = x_ref.shape[1]
    ck = FFN_CHUNK
    rows = tm // FFN_ROW_BLOCKS

    @pl.when(si == 0)
    def _():
        carry_ref[...] = jnp.zeros(carry_ref.shape, F32)

    for r0 in range(0, tm, rows):
        y = (jnp.dot(yp_ref[0, r0:r0 + rows, :], wo_ref[0:POOL_WIDTH, :], preferred_element_type=F32)
             + jnp.dot(ya_ref[0, r0:r0 + rows, :], wo_ref[POOL_WIDTH:, :], preferred_element_type=F32))
        x1 = x_ref[0, r0:r0 + rows, :] + y
        out_ref[0, r0:r0 + rows, :] = x1
        ms = jnp.mean(x1 * x1, axis=-1, keepdims=True)
        hn_ref[r0:r0 + rows, :] = (x1 * lax.rsqrt(ms + NORM_EPS) * gf_ref[...]).astype(BF16)

    for c in range(D_FF // ck):
        lo, hi = c * ck, (c + 1) * ck
        g = jnp.dot(hn_ref[...], w1_ref[:, lo:hi], preferred_element_type=F32)
        u = jnp.dot(hn_ref[...], w1_ref[:, D_FF + lo:D_FF + hi], preferred_element_type=F32)
        gbuf_ref[0:CONV_HALO, :] = carry_ref[:, lo:hi]
        gbuf_ref[CONV_HALO:, :] = g
        g1 = gbuf_ref[pl.ds(CONV_HALO - 1, tm), :]
        g2 = gbuf_ref[pl.ds(CONV_HALO - 2, tm), :]
        carry_ref[:, lo:hi] = g[tm - CONV_HALO:, :]
        cv = (cw_ref[0:1, lo:hi] * g2 + cw_ref[1:2, lo:hi] * g1 + cw_ref[2:3, lo:hi] * g
              + cb_ref[:, lo:hi])
        a = cv * (1.0 / (1.0 + jnp.exp(-cv))) * u
        act_ref[:, lo:hi] = a.astype(BF16)

    for r0 in range(0, tm, rows):
        x2 = out_ref[0, r0:r0 + rows, :] + jnp.dot(act_ref[r0:r0 + rows, :], w2_ref[...],
                                                   preferred_element_type=F32)
        ms2 = jnp.mean(x2 * x2, axis=-1, keepdims=True)
        out_ref[0, r0:r0 + rows, :] = x2 * lax.rsqrt(ms2 + NORM_EPS) * gl_ref[...]


def _out_ffn(x, y_pool, y_attn, w_out, g_ffn, w1, conv_w, conv_b, w2, g_final):
    b, s, d = x.shape
    tm = FFN_ROWS
    const = lambda bi, si: (0, 0)
    resident = functools.partial(pl.BlockSpec, index_map=const, pipeline_mode=pl.Buffered(1))
    return pl.pallas_call(
        _ffn_kernel,
        grid=(b, s // tm),
        in_specs=[
            pl.BlockSpec((1, tm, d), lambda bi, si: (bi, si, 0)),
            pl.BlockSpec((1, tm, POOL_WIDTH), lambda bi, si: (bi, si, 0)),
            pl.BlockSpec((1, tm, ATTN_WIDTH), lambda bi, si: (bi, si, 0)),
            resident((d, d)),
            resident((1, d)),
            resident((d, 2 * D_FF)),
            resident((3, D_FF)),
            resident((1, D_FF)),
            resident((D_FF, d)),
            resident((1, d)),
        ],
        out_specs=pl.BlockSpec((1, tm, d), lambda bi, si: (bi, si, 0)),
        out_shape=jax.ShapeDtypeStruct((b, s, d), F32),
        scratch_shapes=[pltpu.VMEM((CONV_HALO, D_FF), F32),
                        pltpu.VMEM((CONV_HALO + tm, FFN_CHUNK), F32),
                        pltpu.VMEM((tm, D_FF), BF16),
                        pltpu.VMEM((tm, d), BF16)],
        compiler_params=pltpu.CompilerParams(
            dimension_semantics=("arbitrary", "arbitrary"),
            vmem_limit_bytes=VMEM_LIMIT_BYTES),
        name="out_ffn",
    )(x, y_pool, y_attn, w_out, g_ffn, w1, conv_w, conv_b, w2, g_final)


def kernel(x, norm_mix_g, w_in, pool_w, pool_scale, lambda_q1, lambda_k1, lambda_q2, lambda_k2,
           subln_g, rel_bias, w_out, norm_ffn_g, ffn_w_in, ffn_conv_w, ffn_conv_b, ffn_w_out,
           norm_final_g):
    y_pool, qt, k, vt, w_out16, w1_16, w2_16 = _in_proj(
        x, norm_mix_g, w_in[0], pool_w[0], pool_scale, (w_out[0], ffn_w_in[0], ffn_w_out[0]))
    y_attn = _diff_attn(rel_bias, qt, k, vt, (lambda_q1, lambda_k1, lambda_q2, lambda_k2), subln_g)
    return _out_ffn(x, y_pool, y_attn, w_out16, norm_ffn_g, w1_16, ffn_conv_w[0], ffn_conv_b,
                    w2_16, norm_final_g.reshape(1, D_MODEL))
```

```python
import functools
import math

import numpy as np
import jax
import jax.numpy as jnp
from jax import lax
from jax.experimental import pallas as pl
from jax.experimental.pallas import tpu as pltpu

D_MODEL = 1024
POOL_WIDTH = 512
POOL_GROUPS = 4
POOL_GROUP_DIM = 128
POOL_WINDOWS = (2, 4, 8, 16)
POOL_HALO = 16
ATTN_WIDTH = 512
N_HEADS = 4
HEAD_DIM = 64
V_DIM = 128
IN_PROJ_WIDTH = 2048
NUM_BUCKETS = 32
MAX_DISTANCE = 128
D_FF = 2816
NORM_EPS = 1e-6
SUBLN_EPS = 1e-5
NEG_INF = -1e30
LAMBDA_INIT = 0.8 - 0.6 * math.exp(-0.3 * 0)
LOG2E = math.log2(math.e)

ATTN_TILE = 256
BIAS_TILE = 128
QK_LOOKAHEAD = 2
FAR_PER_TRIP = 8
NEXT_SPAN = 2
ATTN_TILES_PER_STEP = 2
IN_ROWS = 1024
IN_ROW_BLOCKS = 2
FFN_ROWS = 1024
FFN_CHUNK = 256
FFN_ROW_BLOCKS = 4
CONV_HALO = 8
WEIGHT_CAST_ROWS = 128
N_LATER_WEIGHTS = 3
VMEM_LIMIT_BYTES = 56 * 1024 * 1024

F32 = jnp.float32
BF16 = jnp.bfloat16


def _bucket_tiles(tile):
    k = np.arange(tile)[:, None]
    q = np.arange(tile)[None, :]
    out = []
    for offset in (0, tile):
        n = q - k + offset
        max_exact = NUM_BUCKETS // 2
        nf = np.maximum(n, 1).astype(np.float64)
        large = max_exact + (np.log(nf / max_exact) / math.log(MAX_DISTANCE / max_exact)
                             * (NUM_BUCKETS - max_exact)).astype(np.int64)
        large = np.minimum(large, NUM_BUCKETS - 1)
        b = np.where(n < max_exact, n, large)
        out.append(np.where(n < 0, -1, b))
    return np.stack(out).astype(np.int32)


def _in_proj_kernel(x_ref, g_ref, w32_ref, pw_ref, ps_ref, *rest):
    later32_refs = rest[:N_LATER_WEIGHTS]
    ypool_ref, qt_ref, k_ref, vt_ref = rest[N_LATER_WEIGHTS:N_LATER_WEIGHTS + 4]
    later16_refs = rest[N_LATER_WEIGHTS + 4:2 * N_LATER_WEIGHTS + 4]
    ext_ref, w_ref, hn_ref = rest[2 * N_LATER_WEIGHTS + 4:]
    si = pl.program_id(1)
    tm = x_ref.shape[1]

    @pl.when((pl.program_id(0) == 0) & (si == 0))
    def _():
        for r in range(0, D_MODEL, WEIGHT_CAST_ROWS):
            w_ref[r:r + WEIGHT_CAST_ROWS, :] = w32_ref[r:r + WEIGHT_CAST_ROWS, :].astype(BF16)

    @pl.when(si == 0)
    def _():
        ext_ref[0:POOL_HALO, :] = jnp.zeros((POOL_HALO, POOL_WIDTH), F32)

    rows = tm // IN_ROW_BLOCKS
    zp_blocks = []
    for r0 in range(0, tm, rows):
        x = x_ref[0, r0:r0 + rows, :]
        ms = jnp.mean(x * x, axis=-1, keepdims=True)
        hn = (x * lax.rsqrt(ms + NORM_EPS) * g_ref[...]).astype(BF16)
        hn_ref[r0:r0 + rows, :] = hn
        zp_blocks.append(jnp.dot(hn, w_ref[:, :POOL_WIDTH], preferred_element_type=F32))
    zp = jnp.concatenate(zp_blocks, axis=0)
    z = jnp.dot(hn_ref[...], w_ref[:, POOL_WIDTH:], preferred_element_type=F32)

    for src_ref, dst_ref in zip(later32_refs, later16_refs):
        dst_ref[...] = src_ref[...].astype(BF16)

    ext_ref[POOL_HALO:, :] = zp
    tpos = si * tm + lax.broadcasted_iota(jnp.int32, (tm, 1), 0) + 1
    for gi, w in enumerate(POOL_WINDOWS):
        lo, hi = gi * POOL_GROUP_DIM, (gi + 1) * POOL_GROUP_DIM
        tot = ext_ref[:, lo:hi]
        step = 1
        while step < w:
            tot = tot + pltpu.roll(tot, step, axis=0)
            step *= 2
        tot = tot[POOL_HALO:, :]
        cnt = jnp.minimum(tpos, w).astype(F32)
        pooled = tot / cnt - zp[:, lo:hi]
        yg = jnp.dot(pooled.astype(BF16), pw_ref[gi].astype(BF16), preferred_element_type=F32)
        ypool_ref[0, :, lo:hi] = (yg * ps_ref[:, lo:hi]).astype(BF16)
    ext_ref[0:POOL_HALO, :] = zp[tm - POOL_HALO:, :]

    t = ATTN_TILE
    q0, k0, v0 = 0, ATTN_WIDTH, 2 * ATTN_WIDTH
    scale = LOG2E * HEAD_DIM ** -0.5
    for h in range(N_HEADS):
        zq = z[:, q0 + h * V_DIM:q0 + (h + 1) * V_DIM] * scale
        qt_ref[0, h * V_DIM:(h + 1) * V_DIM, :] = zq.T.astype(BF16)
        k_ref[0, h] = z[:, k0 + h * V_DIM:k0 + (h + 1) * V_DIM].astype(BF16)
        vt = z[:, v0 + h * V_DIM:v0 + (h + 1) * V_DIM].T.astype(BF16)
        for j in range(tm // t):
            vt_ref[0, h, j] = vt[:, j * t:(j + 1) * t]


def _in_proj(x, g, w_in, pool_w, pool_scale, later_weights):
    b, s, d = x.shape
    tm, t = IN_ROWS, ATTN_TILE
    steps = s // tm
    assert len(later_weights) == N_LATER_WEIGHTS
    slabs = [(w.shape[0] // (b * steps), w.shape[1]) for w in later_weights]
    slab_specs = [pl.BlockSpec(sl, lambda bi, si: (bi * steps + si, 0)) for sl in slabs]
    return pl.pallas_call(
        _in_proj_kernel,
        grid=(b, s // tm),
        in_specs=[
            pl.BlockSpec((1, tm, d), lambda bi, si: (bi, si, 0)),
            pl.BlockSpec((1, d), lambda bi, si: (0, 0)),
            pl.BlockSpec((d, IN_PROJ_WIDTH), lambda bi, si: (0, 0), pipeline_mode=pl.Buffered(1)),
            pl.BlockSpec((POOL_GROUPS, POOL_GROUP_DIM, POOL_GROUP_DIM), lambda bi, si: (0, 0, 0)),
            pl.BlockSpec((1, POOL_WIDTH), lambda bi, si: (0, 0)),
        ] + slab_specs,
        out_specs=[
            pl.BlockSpec((1, tm, POOL_WIDTH), lambda bi, si: (bi, si, 0)),
            pl.BlockSpec((1, ATTN_WIDTH, tm), lambda bi, si: (bi, 0, si)),
            pl.BlockSpec((1, N_HEADS, tm, V_DIM), lambda bi, si: (bi, 0, si, 0)),
            pl.BlockSpec((1, N_HEADS, tm // t, V_DIM, t), lambda bi, si: (bi, 0, si, 0, 0)),
        ] + slab_specs,
        out_shape=[
            jax.ShapeDtypeStruct((b, s, POOL_WIDTH), BF16),
            jax.ShapeDtypeStruct((b, ATTN_WIDTH, s), BF16),
            jax.ShapeDtypeStruct((b, N_HEADS, s, V_DIM), BF16),
            jax.ShapeDtypeStruct((b, N_HEADS, s // t, V_DIM, t), BF16),
        ] + [jax.ShapeDtypeStruct(w.shape, BF16) for w in later_weights],
        scratch_shapes=[pltpu.VMEM((POOL_HALO + tm, POOL_WIDTH), F32),
                        pltpu.VMEM((d, IN_PROJ_WIDTH), BF16),
                        pltpu.VMEM((tm, d), BF16)],
        compiler_params=pltpu.CompilerParams(
            dimension_semantics=("arbitrary", "arbitrary"),
            vmem_limit_bytes=VMEM_LIMIT_BYTES),
        name="in_proj",
    )(x, g, w_in, pool_w, pool_scale, *later_weights)


def _attn_kernel(rb_ref, qt_ref, qtn_ref, k_ref, vt_ref, bkt_ref, lq1_ref, lk1_ref, lq2_ref,
                 lk2_ref, sg_ref, out_ref, *scratch):
    t = ATTN_TILE
    m_refs, l_refs, smax_refs, acc_refs, qs_refs, s_refs = (
        scratch[i * N_HEADS:(i + 1) * N_HEADS] for i in range(6))
    qsn_refs = scratch[6 * N_HEADS:6 * N_HEADS + QK_LOOKAHEAD]
    bias_ref = scratch[6 * N_HEADS + QK_LOOKAHEAD]

    @pl.when((pl.program_id(0) == 0) & (pl.program_id(1) == 0))
    def _():
        bkt = bkt_ref[...]
        acc = [jnp.full(bkt.shape, NEG_INF, F32) for _ in range(N_HEADS)]
        for i in range(NUM_BUCKETS):
            hit = bkt == i
            acc = [jnp.where(hit, rb_ref[i, h] * LOG2E, acc[h]) for h in range(N_HEADS)]
        for h in range(N_HEADS):
            bias_ref[h] = acc[h]

    def load_q(h, src, dst_refs):
        src_ref, col0 = src
        qt = src_ref[0, h * V_DIM:(h + 1) * V_DIM, col0:col0 + t]
        row = lax.broadcasted_iota(jnp.int32, qt.shape, 0)
        zero = jnp.zeros_like(qt)
        dst_refs[h][:, 0:t] = jnp.where(row < HEAD_DIM, qt, zero)
        dst_refs[h][:, t:2 * t] = jnp.where(row >= HEAD_DIM, qt, zero)

    far_bias = [rb_ref[NUM_BUCKETS - 1, h] * LOG2E for h in range(N_HEADS)]

    lam = (jnp.exp(jnp.sum(lq1_ref[...] * lk1_ref[...], axis=1, keepdims=True))
           - jnp.exp(jnp.sum(lq2_ref[...] * lk2_ref[...], axis=1, keepdims=True))
           + LAMBDA_INIT)

    def qk(h, j, kinds, add_bias=True, q_refs=qs_refs):
        n = len(kinds) * t
        kj = k_ref[0, h, pl.ds(pl.multiple_of(j * t, t), n), :]
        s = jnp.dot(kj, q_refs[h][...], preferred_element_type=F32)
        if add_bias and has_bias(kinds):
            s = with_bias(s, h, kinds)
        s_refs[h][0:n, :] = s
        smax_refs[h][...] = jnp.max(s, axis=0, keepdims=True)

    def has_bias(kinds):
        return any(kd is not None for kd in kinds)

    def with_bias(s, h, kinds):
        b = jnp.concatenate([tile_bias(h, kd) for kd in kinds], axis=0)
        return s + jnp.concatenate([b, b], axis=1)

    def tile_bias(h, diag):
        n = t // BIAS_TILE
        pieces = {0: bias_ref[h, 0], 1: bias_ref[h, 1]}
        far = jnp.full((BIAS_TILE, BIAS_TILE), far_bias[h], F32)
        if diag is None:
            return jnp.full((t, t), far_bias[h], F32)
        masked = jnp.full((BIAS_TILE, BIAS_TILE), NEG_INF, F32)
        rows = []
        for kb in range(n):
            d = [qb - kb + (0 if diag else n) for qb in range(n)]
            rows.append(jnp.concatenate([masked if x < 0 else pieces.get(x, far) for x in d], axis=1))
        return jnp.concatenate(rows, axis=0)

    def softmax_pv(h, j, kinds, bias_pending):
        n = len(kinds) * t
        off = 0.0 if has_bias(kinds) else far_bias[h]
        s = s_refs[h][0:n, :]
        if bias_pending and has_bias(kinds):
            s = with_bias(s, h, kinds)
            smax = jnp.max(s, axis=0, keepdims=True)
        else:
            smax = smax_refs[h][...]
        m_old = m_refs[h][...]
        m_new = jnp.maximum(m_old, smax + off)
        alpha = jnp.exp2(m_old - m_new)
        p = jnp.exp2(s - (m_new - off))
        l_refs[h][...] = alpha * l_refs[h][...] + jnp.sum(p, axis=0, keepdims=True)
        m_refs[h][...] = m_new
        vt = jnp.concatenate([vt_ref[0, h, j + d] for d in range(len(kinds))], axis=1)
        r = jnp.dot(vt, p.astype(BF16), preferred_element_type=F32)
        acc_refs[h][...] = alpha * acc_refs[h][...] + r

    def run(groups, next_j, sub):
        stages = [(h, j, kinds) for (j, kinds) in groups for h in range(N_HEADS)]
        for i, (h, j, kinds) in enumerate(stages):
            ahead = i + QK_LOOKAHEAD
            if ahead < len(stages):
                qk(*stages[ahead])
            elif next_j is not None:
                qk(ahead - len(stages), next_j, (None,) * NEXT_SPAN, add_bias=False)
            else:
                nxt = (qt_ref, (sub + 1) * t) if sub + 1 < ATTN_TILES_PER_STEP else (qtn_ref, 0)
                load_q(ahead - len(stages), nxt, qsn_refs)
                qk(ahead - len(stages), 0, (None,) * NEXT_SPAN, add_bias=False, q_refs=qsn_refs)
            softmax_pv(h, j, kinds, bias_pending=i < QK_LOOKAHEAD)
            if kinds[-1] is True:
                finalize(h, sub)

    def finalize(h, sub):
        o = acc_refs[h][...] * (1.0 / l_refs[h][...])
        o = o[:, :t] - lam * o[:, t:]
        ms = jnp.mean(o * o, axis=0, keepdims=True)
        y = (o * lax.rsqrt(ms + SUBLN_EPS)).T * (sg_ref[...] * (1.0 - LAMBDA_INIT))
        out_ref[0, sub * t:(sub + 1) * t, h * V_DIM:(h + 1) * V_DIM] = y.astype(BF16)

    def query_tile(sub):
        qi = pl.program_id(1) * ATTN_TILES_PER_STEP + sub

        if sub == 0:
            @pl.when(qi == 0)
            def _():
                for h in range(QK_LOOKAHEAD):
                    load_q(h, (qt_ref, 0), qsn_refs)
                    qk(h, 0, (True,), add_bias=False, q_refs=qsn_refs)

        for h in range(QK_LOOKAHEAD):
            qs_refs[h][...] = qsn_refs[h][...]
        for h in range(QK_LOOKAHEAD, N_HEADS):
            load_q(h, (qt_ref, sub * t), qs_refs)
        for h in range(N_HEADS):
            m_refs[h][...] = jnp.full(m_refs[h].shape, -jnp.inf, F32)
            l_refs[h][...] = jnp.zeros(l_refs[h].shape, F32)
            acc_refs[h][...] = jnp.zeros(acc_refs[h].shape, F32)

        n_far = qi - 1

        def far_body(i, carry):
            j = FAR_PER_TRIP * i
            run([(j + d, (None, None)) for d in range(0, FAR_PER_TRIP, 2)], j + FAR_PER_TRIP, sub)
            return carry

        lax.fori_loop(0, jnp.maximum(n_far, 0) // FAR_PER_TRIP, far_body, 0)

        if sub == 0:
            @pl.when(qi == 0)
            def _():
                run([(qi, (True,))], None, sub)

        nq = k_ref.shape[2] // t
        for rem in range(FAR_PER_TRIP):
            if not any((q - 1) % FAR_PER_TRIP == rem
                       for q in range(1, nq) if q % ATTN_TILES_PER_STEP == sub):
                continue

            @pl.when((qi >= 1) & (n_far % FAR_PER_TRIP == rem))
            def _():
                kinds = (None,) * rem + (False, True)
                first = qi - 1 - rem
                run([(first + d, kinds[d:d + 2]) for d in range(0, len(kinds), 2)], None, sub)

    for sub in range(ATTN_TILES_PER_STEP):
        query_tile(sub)


def _diff_attn(rel_bias, qt, k, vt, lambdas, subln_g):
    b, _, s = qt.shape
    t, tps = ATTN_TILE, ATTN_TILES_PER_STEP
    nq = s // t
    per_head = lambda shape, dtype: [pltpu.VMEM(shape, dtype) for _ in range(N_HEADS)]
    return pl.pallas_call(
        _attn_kernel,
        grid=(b, nq // tps),
        in_specs=[
            pl.BlockSpec(memory_space=pltpu.SMEM),
            pl.BlockSpec((1, ATTN_WIDTH, tps * t), lambda bi, qi: (bi, 0, qi)),
            pl.BlockSpec((1, QK_LOOKAHEAD * V_DIM, t),
                         lambda bi, qi: (bi, 0, jnp.minimum((qi + 1) * tps, nq - 1))),
            pl.BlockSpec((1, N_HEADS, s, V_DIM), lambda bi, qi: (bi, 0, 0, 0)),
            pl.BlockSpec((1, N_HEADS, nq, V_DIM, t), lambda bi, qi: (bi, 0, 0, 0, 0)),
            pl.BlockSpec((2, BIAS_TILE, BIAS_TILE), lambda bi, qi: (0, 0, 0)),
        ] + [pl.BlockSpec((1, HEAD_DIM), lambda bi, qi: (0, 0)) for _ in lambdas] + [
            pl.BlockSpec((1, V_DIM), lambda bi, qi: (0, 0)),
        ],
        out_specs=pl.BlockSpec((1, tps * t, ATTN_WIDTH), lambda bi, qi: (bi, qi, 0)),
        out_shape=jax.ShapeDtypeStruct((b, s, ATTN_WIDTH), BF16),
        scratch_shapes=(per_head((1, 2 * t), F32) + per_head((1, 2 * t), F32)
                        + per_head((1, 2 * t), F32)
                        + per_head((V_DIM, 2 * t), F32) + per_head((V_DIM, 2 * t), BF16)
                        + per_head((NEXT_SPAN * t, 2 * t), F32)
                        + [pltpu.VMEM((V_DIM, 2 * t), BF16) for _ in range(QK_LOOKAHEAD)]
                        + [pltpu.VMEM((N_HEADS, 2, BIAS_TILE, BIAS_TILE), F32)]),
        compiler_params=pltpu.CompilerParams(
            dimension_semantics=("arbitrary", "arbitrary"),
            vmem_limit_bytes=VMEM_LIMIT_BYTES),
        name="diff_attn",
    )(rel_bias, qt, qt, k, vt, jnp.asarray(_bucket_tiles(BIAS_TILE)), *lambdas, subln_g)


def _ffn_kernel(x_ref, yp_ref, ya_ref, wo_ref, gf_ref, w1_ref, cw_ref, cb_ref,
                w2_ref, gl_ref, out_ref, carry_ref, gbuf_ref, act_ref, hn_ref):
    si = pl.program_id(1)
    tm = x_ref.shape[1]
    ck = FFN_CHUNK
    rows = tm // FFN_ROW_BLOCKS

    @pl.when(si == 0)
    def _():
        carry_ref[...] = jnp.zeros(carry_ref.shape, F32)

    for r0 in range(0, tm, rows):
        y = (jnp.dot(yp_ref[0, r0:r0 + rows, :], wo_ref[0:POOL_WIDTH, :], preferred_element_type=F32)
             + jnp.dot(ya_ref[0, r0:r0 + rows, :], wo_ref[POOL_WIDTH:, :], preferred_element_type=F32))
        x1 = x_ref[0, r0:r0 + rows, :] + y
        out_ref[0, r0:r0 + rows, :] = x1
        ms = jnp.mean(x1 * x1, axis=-1, keepdims=True)
        hn_ref[r0:r0 + rows, :] = (x1 * lax.rsqrt(ms + NORM_EPS) * gf_ref[...]).astype(BF16)

    for c in range(D_FF // ck):
        lo, hi = c * ck, (c + 1) * ck
        g = jnp.dot(hn_ref[...], w1_ref[:, lo:hi], preferred_element_type=F32)
        u = jnp.dot(hn_ref[...], w1_ref[:, D_FF + lo:D_FF + hi], preferred_element_type=F32)
        gbuf_ref[0:CONV_HALO, :] = carry_ref[:, lo:hi]
        gbuf_ref[CONV_HALO:, :] = g
        g1 = gbuf_ref[pl.ds(CONV_HALO - 1, tm), :]
        g2 = gbuf_ref[pl.ds(CONV_HALO - 2, tm), :]
        carry_ref[:, lo:hi] = g[tm - CONV_HALO:, :]
        cv = (cw_ref[0:1, lo:hi] * g2 + cw_ref[1:2, lo:hi] * g1 + cw_ref[2:3, lo:hi] * g
              + cb_ref[:, lo:hi])
        a = cv * (1.0 / (1.0 + jnp.exp(-cv))) * u
        act_ref[:, lo:hi] = a.astype(BF16)

    for r0 in range(0, tm, rows):
        x2 = out_ref[0, r0:r0 + rows, :] + jnp.dot(act_ref[r0:r0 + rows, :], w2_ref[...],
                                                   preferred_element_type=F32)
        ms2 = jnp.mean(x2 * x2, axis=-1, keepdims=True)
        out_ref[0, r0:r0 + rows, :] = x2 * lax.rsqrt(ms2 + NORM_EPS) * gl_ref[...]


def _out_ffn(x, y_pool, y_attn, w_out, g_ffn, w1, conv_w, conv_b, w2, g_final):
    b, s, d = x.shape
    tm = FFN_ROWS
    const = lambda bi, si: (0, 0)
    resident = functools.partial(pl.BlockSpec, index_map=const, pipeline_mode=pl.Buffered(1))
    return pl.pallas_call(
        _ffn_kernel,
        grid=(b, s // tm),
        in_specs=[
            pl.BlockSpec((1, tm, d), lambda bi, si: (bi, si, 0)),
            pl.BlockSpec((1, tm, POOL_WIDTH), lambda bi, si: (bi, si, 0)),
            pl.BlockSpec((1, tm, ATTN_WIDTH), lambda bi, si: (bi, si, 0)),
            resident((d, d)),
            resident((1, d)),
            resident((d, 2 * D_FF)),
            resident((3, D_FF)),
            resident((1, D_FF)),
            resident((D_FF, d)),
            resident((1, d)),
        ],
        out_specs=pl.BlockSpec((1, tm, d), lambda bi, si: (bi, si, 0)),
        out_shape=jax.ShapeDtypeStruct((b, s, d), F32),
        scratch_shapes=[pltpu.VMEM((CONV_HALO, D_FF), F32),
                        pltpu.VMEM((CONV_HALO + tm, FFN_CHUNK), F32),
                        pltpu.VMEM((tm, D_FF), BF16),
                        pltpu.VMEM((tm, d), BF16)],
        compiler_params=pltpu.CompilerParams(
            dimension_semantics=("arbitrary", "arbitrary"),
            vmem_limit_bytes=VMEM_LIMIT_BYTES),
        name="out_ffn",
    )(x, y_pool, y_attn, w_out, g_ffn, w1, conv_w, conv_b, w2, g_final)


def kernel(x, norm_mix_g, w_in, pool_w, pool_scale, lambda_q1, lambda_k1, lambda_q2, lambda_k2,
           subln_g, rel_bias, w_out, norm_ffn_g, ffn_w_in, ffn_conv_w, ffn_conv_b, ffn_w_out,
           norm_final_g):
    y_pool, qt, k, vt, w_out16, w1_16, w2_16 = _in_proj(
        x, norm_mix_g, w_in[0], pool_w[0], pool_scale, (w_out[0], ffn_w_in[0], ffn_w_out[0]))
    y_attn = _diff_attn(rel_bias, qt, k, vt, (lambda_q1, lambda_k1, lambda_q2, lambda_k2), subln_g)
    return _out_ffn(x, y_pool, y_attn, w_out16, norm_ffn_g, w1_16, ffn_conv_w[0], ffn_conv_b,
                    w2_16, norm_final_g.reshape(1, D_MODEL))
```

```python
import functools
import math

import numpy as np
import jax
import jax.numpy as jnp
from jax import lax
from jax.experimental import pallas as pl
from jax.experimental.pallas import tpu as pltpu

D_MODEL = 1024
POOL_WIDTH = 512
POOL_GROUPS = 4
POOL_GROUP_DIM = 128
POOL_WINDOWS = (2, 4, 8, 16)
POOL_HALO = 16
ATTN_WIDTH = 512
N_HEADS = 4
HEAD_DIM = 64
V_DIM = 128
IN_PROJ_WIDTH = 2048
NUM_BUCKETS = 32
MAX_DISTANCE = 128
D_FF = 2816
NORM_EPS = 1e-6
SUBLN_EPS = 1e-5
NEG_INF = -1e30
LAMBDA_INIT = 0.8 - 0.6 * math.exp(-0.3 * 0)
LOG2E = math.log2(math.e)

ATTN_TILE = 256
BIAS_TILE = 128
QK_LOOKAHEAD = 2
FAR_PER_TRIP = 4
NEXT_SPAN = 2
ATTN_TILES_PER_STEP = 2
IN_ROWS = 1024
IN_ROW_BLOCKS = 2
FFN_ROWS = 1024
FFN_CHUNK = 256
FFN_ROW_BLOCKS = 4
CONV_HALO = 8
WEIGHT_CAST_ROWS = 128
N_LATER_WEIGHTS = 3
VMEM_LIMIT_BYTES = 56 * 1024 * 1024

F32 = jnp.float32
BF16 = jnp.bfloat16


def _bucket_tiles(tile):
    k = np.arange(tile)[:, None]
    q = np.arange(tile)[None, :]
    out = []
    for offset in (0, tile):
        n = q - k + offset
        max_exact = NUM_BUCKETS // 2
        nf = np.maximum(n, 1).astype(np.float64)
        large = max_exact + (np.log(nf / max_exact) / math.log(MAX_DISTANCE / max_exact)
                             * (NUM_BUCKETS - max_exact)).astype(np.int64)
        large = np.minimum(large, NUM_BUCKETS - 1)
        b = np.where(n < max_exact, n, large)
        out.append(np.where(n < 0, -1, b))
    return np.stack(out).astype(np.int32)


def _in_proj_kernel(x_ref, g_ref, w32_ref, pw_ref, ps_ref, *rest):
    later32_refs = rest[:N_LATER_WEIGHTS]
    ypool_ref, qt_ref, k_ref, vt_ref = rest[N_LATER_WEIGHTS:N_LATER_WEIGHTS + 4]
    later16_refs = rest[N_LATER_WEIGHTS + 4:2 * N_LATER_WEIGHTS + 4]
    ext_ref, w_ref, hn_ref = rest[2 * N_LATER_WEIGHTS + 4:]
    si = pl.program_id(1)
    tm = x_ref.shape[1]

    @pl.when((pl.program_id(0) == 0) & (si == 0))
    def _():
        for r in range(0, D_MODEL, WEIGHT_CAST_ROWS):
            w_ref[r:r + WEIGHT_CAST_ROWS, :] = w32_ref[r:r + WEIGHT_CAST_ROWS, :].astype(BF16)

    @pl.when(si == 0)
    def _():
        ext_ref[0:POOL_HALO, :] = jnp.zeros((POOL_HALO, POOL_WIDTH), F32)

    rows = tm // IN_ROW_BLOCKS
    zp_blocks = []
    for r0 in range(0, tm, rows):
        x = x_ref[0, r0:r0 + rows, :]
        ms = jnp.mean(x * x, axis=-1, keepdims=True)
        hn = (x * lax.rsqrt(ms + NORM_EPS) * g_ref[...]).astype(BF16)
        hn_ref[r0:r0 + rows, :] = hn
        zp_blocks.append(jnp.dot(hn, w_ref[:, :POOL_WIDTH], preferred_element_type=F32))
    zp = jnp.concatenate(zp_blocks, axis=0)
    z = jnp.dot(hn_ref[...], w_ref[:, POOL_WIDTH:], preferred_element_type=F32)

    for src_ref, dst_ref in zip(later32_refs, later16_refs):
        dst_ref[...] = src_ref[...].astype(BF16)

    ext_ref[POOL_HALO:, :] = zp
    tpos = si * tm + lax.broadcasted_iota(jnp.int32, (tm, 1), 0) + 1
    for gi, w in enumerate(POOL_WINDOWS):
        lo, hi = gi * POOL_GROUP_DIM, (gi + 1) * POOL_GROUP_DIM
        tot = ext_ref[:, lo:hi]
        step = 1
        while step < w:
            tot = tot + pltpu.roll(tot, step, axis=0)
            step *= 2
        tot = tot[POOL_HALO:, :]
        cnt = jnp.minimum(tpos, w).astype(F32)
        pooled = tot / cnt - zp[:, lo:hi]
        yg = jnp.dot(pooled.astype(BF16), pw_ref[gi].astype(BF16), preferred_element_type=F32)
        ypool_ref[0, :, lo:hi] = (yg * ps_ref[:, lo:hi]).astype(BF16)
    ext_ref[0:POOL_HALO, :] = zp[tm - POOL_HALO:, :]

    t = ATTN_TILE
    q0, k0, v0 = 0, ATTN_WIDTH, 2 * ATTN_WIDTH
    scale = LOG2E * HEAD_DIM ** -0.5
    for h in range(N_HEADS):
        zq = z[:, q0 + h * V_DIM:q0 + (h + 1) * V_DIM] * scale
        qt_ref[0, h * V_DIM:(h + 1) * V_DIM, :] = zq.T.astype(BF16)
        k_ref[0, h] = z[:, k0 + h * V_DIM:k0 + (h + 1) * V_DIM].astype(BF16)
        vt = z[:, v0 + h * V_DIM:v0 + (h + 1) * V_DIM].T.astype(BF16)
        for j in range(tm // t):
            vt_ref[0, h, j] = vt[:, j * t:(j + 1) * t]


def _in_proj(x, g, w_in, pool_w, pool_scale, later_weights):
    b, s, d = x.shape
    tm, t = IN_ROWS, ATTN_TILE
    steps = s // tm
    assert len(later_weights) == N_LATER_WEIGHTS
    slabs = [(w.shape[0] // (b * steps), w.shape[1]) for w in later_weights]
    slab_specs = [pl.BlockSpec(sl, lambda bi, si: (bi * steps + si, 0)) for sl in slabs]
    return pl.pallas_call(
        _in_proj_kernel,
        grid=(b, s // tm),
        in_specs=[
            pl.BlockSpec((1, tm, d), lambda bi, si: (bi, si, 0)),
            pl.BlockSpec((1, d), lambda bi, si: (0, 0)),
            pl.BlockSpec((d, IN_PROJ_WIDTH), lambda bi, si: (0, 0), pipeline_mode=pl.Buffered(1)),
            pl.BlockSpec((POOL_GROUPS, POOL_GROUP_DIM, POOL_GROUP_DIM), lambda bi, si: (0, 0, 0)),
            pl.BlockSpec((1, POOL_WIDTH), lambda bi, si: (0, 0)),
        ] + slab_specs,
        out_specs=[
            pl.BlockSpec((1, tm, POOL_WIDTH), lambda bi, si: (bi, si, 0)),
            pl.BlockSpec((1, ATTN_WIDTH, tm), lambda bi, si: (bi, 0, si)),
            pl.BlockSpec((1, N_HEADS, tm, V_DIM), lambda bi, si: (bi, 0, si, 0)),
            pl.BlockSpec((1, N_HEADS, tm // t, V_DIM, t), lambda bi, si: (bi, 0, si, 0, 0)),
        ] + slab_specs,
        out_shape=[
            jax.ShapeDtypeStruct((b, s, POOL_WIDTH), BF16),
            jax.ShapeDtypeStruct((b, ATTN_WIDTH, s), BF16),
            jax.ShapeDtypeStruct((b, N_HEADS, s, V_DIM), BF16),
            jax.ShapeDtypeStruct((b, N_HEADS, s // t, V_DIM, t), BF16),
        ] + [jax.ShapeDtypeStruct(w.shape, BF16) for w in later_weights],
        scratch_shapes=[pltpu.VMEM((POOL_HALO + tm, POOL_WIDTH), F32),
                        pltpu.VMEM((d, IN_PROJ_WIDTH), BF16),
                        pltpu.VMEM((tm, d), BF16)],
        compiler_params=pltpu.CompilerParams(
            dimension_semantics=("arbitrary", "arbitrary"),
            vmem_limit_bytes=VMEM_LIMIT_BYTES),
        name="in_proj",
    )(x, g, w_in, pool_w, pool_scale, *later_weights)


def _attn_kernel(rb_ref, qt_ref, qtn_ref, k_ref, vt_ref, bkt_ref, lq1_ref, lk1_ref, lq2_ref,
                 lk2_ref, sg_ref, out_ref, *scratch):
    t = ATTN_TILE
    m_refs, l_refs, smax_refs, acc_refs, qs_refs, s_refs = (
        scratch[i * N_HEADS:(i + 1) * N_HEADS] for i in range(6))
    qsn_refs = scratch[6 * N_HEADS:6 * N_HEADS + QK_LOOKAHEAD]
    bias_ref = scratch[6 * N_HEADS + QK_LOOKAHEAD]

    @pl.when((pl.program_id(0) == 0) & (pl.program_id(1) == 0))
    def _():
        bkt = bkt_ref[...]
        acc = [jnp.full(bkt.shape, NEG_INF, F32) for _ in range(N_HEADS)]
        for i in range(NUM_BUCKETS):
            hit = bkt == i
            acc = [jnp.where(hit, rb_ref[i, h] * LOG2E, acc[h]) for h in range(N_HEADS)]
        for h in range(N_HEADS):
            bias_ref[h] = acc[h]

    def load_q(h, src, dst_refs):
        src_ref, col0 = src
        qt = src_ref[0, h * V_DIM:(h + 1) * V_DIM, col0:col0 + t]
        row = lax.broadcasted_iota(jnp.int32, qt.shape, 0)
        zero = jnp.zeros_like(qt)
        dst_refs[h][:, 0:t] = jnp.where(row < HEAD_DIM, qt, zero)
        dst_refs[h][:, t:2 * t] = jnp.where(row >= HEAD_DIM, qt, zero)

    far_bias = [rb_ref[NUM_BUCKETS - 1, h] * LOG2E for h in range(N_HEADS)]

    lam = (jnp.exp(jnp.sum(lq1_ref[...] * lk1_ref[...], axis=1, keepdims=True))
           - jnp.exp(jnp.sum(lq2_ref[...] * lk2_ref[...], axis=1, keepdims=True))
           + LAMBDA_INIT)

    def qk(h, j, kinds, add_bias=True, q_refs=qs_refs):
        n = len(kinds) * t
        kj = k_ref[0, h, pl.ds(pl.multiple_of(j * t, t), n), :]
        s = jnp.dot(kj, q_refs[h][...], preferred_element_type=F32)
        if add_bias and has_bias(kinds):
            s = with_bias(s, h, kinds)
        s_refs[h][0:n, :] = s
        smax_refs[h][...] = jnp.max(s, axis=0, keepdims=True)

    def has_bias(kinds):
        return any(kd is not None for kd in kinds)

    def with_bias(s, h, kinds):
        b = jnp.concatenate([tile_bias(h, kd) for kd in kinds], axis=0)
        return s + jnp.concatenate([b, b], axis=1)

    def tile_bias(h, diag):
        n = t // BIAS_TILE
        pieces = {0: bias_ref[h, 0], 1: bias_ref[h, 1]}
        far = jnp.full((BIAS_TILE, BIAS_TILE), far_bias[h], F32)
        if diag is None:
            return jnp.full((t, t), far_bias[h], F32)
        masked = jnp.full((BIAS_TILE, BIAS_TILE), NEG_INF, F32)
        rows = []
        for kb in range(n):
            d = [qb - kb + (0 if diag else n) for qb in range(n)]
            rows.append(jnp.concatenate([masked if x < 0 else pieces.get(x, far) for x in d], axis=1))
        return jnp.concatenate(rows, axis=0)

    def softmax_pv(h, j, kinds, bias_pending):
        n = len(kinds) * t
        off = 0.0 if has_bias(kinds) else far_bias[h]
        s = s_refs[h][0:n, :]
        if bias_pending and has_bias(kinds):
            s = with_bias(s, h, kinds)
            smax = jnp.max(s, axis=0, keepdims=True)
        else:
            smax = smax_refs[h][...]
        m_old = m_refs[h][...]
        m_new = jnp.maximum(m_old, smax + off)
        alpha = jnp.exp2(m_old - m_new)
        p = jnp.exp2(s - (m_new - off))
        l_refs[h][...] = alpha * l_refs[h][...] + jnp.sum(p, axis=0, keepdims=True)
        m_refs[h][...] = m_new
        vt = jnp.concatenate([vt_ref[0, h, j + d] for d in range(len(kinds))], axis=1)
        r = jnp.dot(vt, p.astype(BF16), preferred_element_type=F32)
        acc_refs[h][...] = alpha * acc_refs[h][...] + r

    def run(groups, next_j, sub):
        stages = [(h, j, kinds) for (j, kinds) in groups for h in range(N_HEADS)]
        for i, (h, j, kinds) in enumerate(stages):
            ahead = i + QK_LOOKAHEAD
            if ahead < len(stages):
                qk(*stages[ahead])
            elif next_j is not None:
                qk(ahead - len(stages), next_j, (None,) * NEXT_SPAN, add_bias=False)
            else:
                nxt = (qt_ref, (sub + 1) * t) if sub + 1 < ATTN_TILES_PER_STEP else (qtn_ref, 0)
                load_q(ahead - len(stages), nxt, qsn_refs)
                qk(ahead - len(stages), 0, (None,) * NEXT_SPAN, add_bias=False, q_refs=qsn_refs)
            softmax_pv(h, j, kinds, bias_pending=i < QK_LOOKAHEAD)
            if kinds[-1] is True:
                finalize(h, sub)

    def finalize(h, sub):
        o = acc_refs[h][...] * (1.0 / l_refs[h][...])
        o = o[:, :t] - lam * o[:, t:]
        ms = jnp.mean(o * o, axis=0, keepdims=True)
        y = (o * lax.rsqrt(ms + SUBLN_EPS)).T * (sg_ref[...] * (1.0 - LAMBDA_INIT))
        out_ref[0, sub * t:(sub + 1) * t, h * V_DIM:(h + 1) * V_DIM] = y.astype(BF16)

    def query_tile(sub):
        qi = pl.program_id(1) * ATTN_TILES_PER_STEP + sub

        if sub == 0:
            @pl.when(qi == 0)
            def _():
                for h in range(QK_LOOKAHEAD):
                    load_q(h, (qt_ref, 0), qsn_refs)
                    qk(h, 0, (True,), add_bias=False, q_refs=qsn_refs)

        for h in range(QK_LOOKAHEAD):
            qs_refs[h][...] = qsn_refs[h][...]
        for h in range(QK_LOOKAHEAD, N_HEADS):
            load_q(h, (qt_ref, sub * t), qs_refs)
        for h in range(N_HEADS):
            m_refs[h][...] = jnp.full(m_refs[h].shape, -jnp.inf, F32)
            l_refs[h][...] = jnp.zeros(l_refs[h].shape, F32)
            acc_refs[h][...] = jnp.zeros(acc_refs[h].shape, F32)

        n_far = qi - 1

        def far_body(i, carry):
            j = FAR_PER_TRIP * i
            run([(j + d, (None, None)) for d in range(0, FAR_PER_TRIP, 2)], j + FAR_PER_TRIP, sub)
            return carry

        lax.fori_loop(0, jnp.maximum(n_far, 0) // FAR_PER_TRIP, far_body, 0)

        if sub == 0:
            @pl.when(qi == 0)
            def _():
                run([(qi, (True,))], None, sub)

        nq = k_ref.shape[2] // t
        for rem in range(FAR_PER_TRIP):
            if not any((q - 1) % FAR_PER_TRIP == rem
                       for q in range(1, nq) if q % ATTN_TILES_PER_STEP == sub):
                continue

            @pl.when((qi >= 1) & (n_far % FAR_PER_TRIP == rem))
            def _():
                kinds = (None,) * rem + (False, True)
                first = qi - 1 - rem
                run([(first + d, kinds[d:d + 2]) for d in range(0, len(kinds), 2)], None, sub)

    for sub in range(ATTN_TILES_PER_STEP):
        query_tile(sub)


def _diff_attn(rel_bias, qt, k, vt, lambdas, subln_g):
    b, _, s = qt.shape
    t, tps = ATTN_TILE, ATTN_TILES_PER_STEP
    nq = s // t
    per_head = lambda shape, dtype: [pltpu.VMEM(shape, dtype) for _ in range(N_HEADS)]
    return pl.pallas_call(
        _attn_kernel,
        grid=(b, nq // tps),
        in_specs=[
            pl.BlockSpec(memory_space=pltpu.SMEM),
            pl.BlockSpec((1, ATTN_WIDTH, tps * t), lambda bi, qi: (bi, 0, qi)),
            pl.BlockSpec((1, QK_LOOKAHEAD * V_DIM, t),
                         lambda bi, qi: (bi, 0, jnp.minimum((qi + 1) * tps, nq - 1))),
            pl.BlockSpec((1, N_HEADS, s, V_DIM), lambda bi, qi: (bi, 0, 0, 0)),
            pl.BlockSpec((1, N_HEADS, nq, V_DIM, t), lambda bi, qi: (bi, 0, 0, 0, 0)),
            pl.BlockSpec((2, BIAS_TILE, BIAS_TILE), lambda bi, qi: (0, 0, 0)),
        ] + [pl.BlockSpec((1, HEAD_DIM), lambda bi, qi: (0, 0)) for _ in lambdas] + [
            pl.BlockSpec((1, V_DIM), lambda bi, qi: (0, 0)),
        ],
        out_specs=pl.BlockSpec((1, tps * t, ATTN_WIDTH), lambda bi, qi: (bi, qi, 0)),
        out_shape=jax.ShapeDtypeStruct((b, s, ATTN_WIDTH), BF16),
        scratch_shapes=(per_head((1, 2 * t), F32) + per_head((1, 2 * t), F32)
                        + per_head((1, 2 * t), F32)
                        + per_head((V_DIM, 2 * t), F32) + per_head((V_DIM, 2 * t), BF16)
                        + per_head((NEXT_SPAN * t, 2 * t), F32)
                        + [pltpu.VMEM((V_DIM, 2 * t), BF16) for _ in range(QK_LOOKAHEAD)]
                        + [pltpu.VMEM((N_HEADS, 2, BIAS_TILE, BIAS_TILE), F32)]),
        compiler_params=pltpu.CompilerParams(
            dimension_semantics=("arbitrary", "arbitrary"),
            vmem_limit_bytes=VMEM_LIMIT_BYTES),
        name="diff_attn",
    )(rel_bias, qt, qt, k, vt, jnp.asarray(_bucket_tiles(BIAS_TILE)), *lambdas, subln_g)


def _ffn_kernel(x_ref, yp_ref, ya_ref, wo_ref, gf_ref, w1_ref, cw_ref, cb_ref,
                w2_ref, gl_ref, out_ref, carry_ref, gbuf_ref, act_ref, hn_ref):
    si = pl.program_id(1)
    tm = x_ref.shape[1]
    ck = FFN_CHUNK
    rows = tm // FFN_ROW_BLOCKS

    @pl.when(si == 0)
    def _():
        carry_ref[...] = jnp.zeros(carry_ref.shape, F32)

    for r0 in range(0, tm, rows):
        y = (jnp.dot(yp_ref[0, r0:r0 + rows, :], wo_ref[0:POOL_WIDTH, :], preferred_element_type=F32)
             + jnp.dot(ya_ref[0, r0:r0 + rows, :], wo_ref[POOL_WIDTH:, :], preferred_element_type=F32))
        x1 = x_ref[0, r0:r0 + rows, :] + y
        out_ref[0, r0:r0 + rows, :] = x1
        ms = jnp.mean(x1 * x1, axis=-1, keepdims=True)
        hn_ref[r0:r0 + rows, :] = (x1 * lax.rsqrt(ms + NORM_EPS) * gf_ref[...]).astype(BF16)

    for c in range(D_FF // ck):
        lo, hi = c * ck, (c + 1) * ck
        g = jnp.dot(hn_ref[...], w1_ref[:, lo:hi], preferred_element_type=F32)
        u = jnp.dot(hn_ref[...], w1_ref[:, D_FF + lo:D_FF + hi], preferred_element_type=F32)
        gbuf_ref[0:CONV_HALO, :] = carry_ref[:, lo:hi]
        gbuf_ref[CONV_HALO:, :] = g
        g1 = gbuf_ref[pl.ds(CONV_HALO - 1, tm), :]
        g2 = gbuf_ref[pl.ds(CONV_HALO - 2, tm), :]
        carry_ref[:, lo:hi] = g[tm - CONV_HALO:, :]
        cv = (cw_ref[0:1, lo:hi] * g2 + cw_ref[1:2, lo:hi] * g1 + cw_ref[2:3, lo:hi] * g
              + cb_ref[:, lo:hi])
        a = cv * (1.0 / (1.0 + jnp.exp(-cv))) * u
        act_ref[:, lo:hi] = a.astype(BF16)

    for r0 in range(0, tm, rows):
        x2 = out_ref[0, r0:r0 + rows, :] + jnp.dot(act_ref[r0:r0 + rows, :], w2_ref[...],
                                                   preferred_element_type=F32)
        ms2 = jnp.mean(x2 * x2, axis=-1, keepdims=True)
        out_ref[0, r0:r0 + rows, :] = x2 * lax.rsqrt(ms2 + NORM_EPS) * gl_ref[...]


def _out_ffn(x, y_pool, y_attn, w_out, g_ffn, w1, conv_w, conv_b, w2, g_final):
    b, s, d = x.shape
    tm = FFN_ROWS
    const = lambda bi, si: (0, 0)
    resident = functools.partial(pl.BlockSpec, index_map=const, pipeline_mode=pl.Buffered(1))
    return pl.pallas_call(
        _ffn_kernel,
        grid=(b, s // tm),
        in_specs=[
            pl.BlockSpec((1, tm, d), lambda bi, si: (bi, si, 0)),
            pl.BlockSpec((1, tm, POOL_WIDTH), lambda bi, si: (bi, si, 0)),
            pl.BlockSpec((1, tm, ATTN_WIDTH), lambda bi, si: (bi, si, 0)),
            resident((d, d)),
            resident((1, d)),
            resident((d, 2 * D_FF)),
            resident((3, D_FF)),
            resident((1, D_FF)),
            resident((D_FF, d)),
            resident((1, d)),
        ],
        out_specs=pl.BlockSpec((1, tm, d), lambda bi, si: (bi, si, 0)),
        out_shape=jax.ShapeDtypeStruct((b, s, d), F32),
        scratch_shapes=[pltpu.VMEM((CONV_HALO, D_FF), F32),
                        pltpu.VMEM((CONV_HALO + tm, FFN_CHUNK), F32),
                        pltpu.VMEM((tm, D_FF), BF16),
                        pltpu.VMEM((tm, d), BF16)],
        compiler_params=pltpu.CompilerParams(
            dimension_semantics=("arbitrary", "arbitrary"),
            vmem_limit_bytes=VMEM_LIMIT_BYTES),
        name="out_ffn",
    )(x, y_pool, y_attn, w_out, g_ffn, w1, conv_w, conv_b, w2, g_final)


def kernel(x, norm_mix_g, w_in, pool_w, pool_scale, lambda_q1, lambda_k1, lambda_q2, lambda_k2,
           subln_g, rel_bias, w_out, norm_ffn_g, ffn_w_in, ffn_conv_w, ffn_conv_b, ffn_w_out,
           norm_final_g):
    y_pool, qt, k, vt, w_out16, w1_16, w2_16 = _in_proj(
        x, norm_mix_g, w_in[0], pool_w[0], pool_scale, (w_out[0], ffn_w_in[0], ffn_w_out[0]))
    y_attn = _diff_attn(rel_bias, qt, k, vt, (lambda_q1, lambda_k1, lambda_q2, lambda_k2), subln_g)
    return _out_ffn(x, y_pool, y_attn, w_out16, norm_ffn_g, w1_16, ffn_conv_w[0], ffn_conv_b,
                    w2_16, norm_final_g.reshape(1, D_MODEL))
```

```python
import functools
import math

import numpy as np
import jax
import jax.numpy as jnp
from jax import lax
from jax.experimental import pallas as pl
from jax.experimental.pallas import tpu as pltpu

D_MODEL = 1024
POOL_WIDTH = 512
POOL_GROUPS = 4
POOL_GROUP_DIM = 128
POOL_WINDOWS = (2, 4, 8, 16)
POOL_HALO = 16
ATTN_WIDTH = 512
N_HEADS = 4
HEAD_DIM = 64
V_DIM = 128
IN_PROJ_WIDTH = 2048
NUM_BUCKETS = 32
MAX_DISTANCE = 128
D_FF = 2816
NORM_EPS = 1e-6
SUBLN_EPS = 1e-5
NEG_INF = -1e30
LAMBDA_INIT = 0.8 - 0.6 * math.exp(-0.3 * 0)
LOG2E = math.log2(math.e)

ATTN_TILE = 256
BIAS_TILE = 128
QK_LOOKAHEAD = 2
FAR_PER_TRIP = 2
NEXT_SPAN = 2
ATTN_TILES_PER_STEP = 2
IN_ROWS = 1024
IN_ROW_BLOCKS = 2
FFN_ROWS = 1024
FFN_CHUNK = 256
FFN_ROW_BLOCKS = 4
CONV_HALO = 8
WEIGHT_CAST_ROWS = 128
N_LATER_WEIGHTS = 3
VMEM_LIMIT_BYTES = 56 * 1024 * 1024

F32 = jnp.float32
BF16 = jnp.bfloat16


def _bucket_tiles(tile):
    k = np.arange(tile)[:, None]
    q = np.arange(tile)[None, :]
    out = []
    for offset in (0, tile):
        n = q - k + offset
        max_exact = NUM_BUCKETS // 2
        nf = np.maximum(n, 1).astype(np.float64)
        large = max_exact + (np.log(nf / max_exact) / math.log(MAX_DISTANCE / max_exact)
                             * (NUM_BUCKETS - max_exact)).astype(np.int64)
        large = np.minimum(large, NUM_BUCKETS - 1)
        b = np.where(n < max_exact, n, large)
        out.append(np.where(n < 0, -1, b))
    return np.stack(out).astype(np.int32)


def _in_proj_kernel(x_ref, g_ref, w32_ref, pw_ref, ps_ref, *rest):
    later32_refs = rest[:N_LATER_WEIGHTS]
    ypool_ref, qt_ref, k_ref, vt_ref = rest[N_LATER_WEIGHTS:N_LATER_WEIGHTS + 4]
    later16_refs = rest[N_LATER_WEIGHTS + 4:2 * N_LATER_WEIGHTS + 4]
    ext_ref, w_ref, hn_ref = rest[2 * N_LATER_WEIGHTS + 4:]
    si = pl.program_id(1)
    tm = x_ref.shape[1]

    @pl.when((pl.program_id(0) == 0) & (si == 0))
    def _():
        for r in range(0, D_MODEL, WEIGHT_CAST_ROWS):
            w_ref[r:r + WEIGHT_CAST_ROWS, :] = w32_ref[r:r + WEIGHT_CAST_ROWS, :].astype(BF16)

    @pl.when(si == 0)
    def _():
        ext_ref[0:POOL_HALO, :] = jnp.zeros((POOL_HALO, POOL_WIDTH), F32)

    rows = tm // IN_ROW_BLOCKS
    zp_blocks = []
    for r0 in range(0, tm, rows):
        x = x_ref[0, r0:r0 + rows, :]
        ms = jnp.mean(x * x, axis=-1, keepdims=True)
        hn = (x * lax.rsqrt(ms + NORM_EPS) * g_ref[...]).astype(BF16)
        hn_ref[r0:r0 + rows, :] = hn
        zp_blocks.append(jnp.dot(hn, w_ref[:, :POOL_WIDTH], preferred_element_type=F32))
    zp = jnp.concatenate(zp_blocks, axis=0)
    z = jnp.dot(hn_ref[...], w_ref[:, POOL_WIDTH:], preferred_element_type=F32)

    for src_ref, dst_ref in zip(later32_refs, later16_refs):
        dst_ref[...] = src_ref[...].astype(BF16)

    ext_ref[POOL_HALO:, :] = zp
    tpos = si * tm + lax.broadcasted_iota(jnp.int32, (tm, 1), 0) + 1
    for gi, w in enumerate(POOL_WINDOWS):
        lo, hi = gi * POOL_GROUP_DIM, (gi + 1) * POOL_GROUP_DIM
        tot = ext_ref[:, lo:hi]
        step = 1
        while step < w:
            tot = tot + pltpu.roll(tot, step, axis=0)
            step *= 2
        tot = tot[POOL_HALO:, :]
        cnt = jnp.minimum(tpos, w).astype(F32)
        pooled = tot / cnt - zp[:, lo:hi]
        yg = jnp.dot(pooled.astype(BF16), pw_ref[gi].astype(BF16), preferred_element_type=F32)
        ypool_ref[0, :, lo:hi] = (yg * ps_ref[:, lo:hi]).astype(BF16)
    ext_ref[0:POOL_HALO, :] = zp[tm - POOL_HALO:, :]

    t = ATTN_TILE
    q0, k0, v0 = 0, ATTN_WIDTH, 2 * ATTN_WIDTH
    scale = LOG2E * HEAD_DIM ** -0.5
    for h in range(N_HEADS):
        zq = z[:, q0 + h * V_DIM:q0 + (h + 1) * V_DIM] * scale
        qt_ref[0, h * V_DIM:(h + 1) * V_DIM, :] = zq.T.astype(BF16)
        k_ref[0, h] = z[:, k0 + h * V_DIM:k0 + (h + 1) * V_DIM].astype(BF16)
        vt = z[:, v0 + h * V_DIM:v0 + (h + 1) * V_DIM].T.astype(BF16)
        for j in range(tm // t):
            vt_ref[0, h, j] = vt[:, j * t:(j + 1) * t]


def _in_proj(x, g, w_in, pool_w, pool_scale, later_weights):
    b, s, d = x.shape
    tm, t = IN_ROWS, ATTN_TILE
    steps = s // tm
    assert len(later_weights) == N_LATER_WEIGHTS
    slabs = [(w.shape[0] // (b * steps), w.shape[1]) for w in later_weights]
    slab_specs = [pl.BlockSpec(sl, lambda bi, si: (bi * steps + si, 0)) for sl in slabs]
    return pl.pallas_call(
        _in_proj_kernel,
        grid=(b, s // tm),
        in_specs=[
            pl.BlockSpec((1, tm, d), lambda bi, si: (bi, si, 0)),
            pl.BlockSpec((1, d), lambda bi, si: (0, 0)),
            pl.BlockSpec((d, IN_PROJ_WIDTH), lambda bi, si: (0, 0), pipeline_mode=pl.Buffered(1)),
            pl.BlockSpec((POOL_GROUPS, POOL_GROUP_DIM, POOL_GROUP_DIM), lambda bi, si: (0, 0, 0)),
            pl.BlockSpec((1, POOL_WIDTH), lambda bi, si: (0, 0)),
        ] + slab_specs,
        out_specs=[
            pl.BlockSpec((1, tm, POOL_WIDTH), lambda bi, si: (bi, si, 0)),
            pl.BlockSpec((1, ATTN_WIDTH, tm), lambda bi, si: (bi, 0, si)),
            pl.BlockSpec((1, N_HEADS, tm, V_DIM), lambda bi, si: (bi, 0, si, 0)),
            pl.BlockSpec((1, N_HEADS, tm // t, V_DIM, t), lambda bi, si: (bi, 0, si, 0, 0)),
        ] + slab_specs,
        out_shape=[
            jax.ShapeDtypeStruct((b, s, POOL_WIDTH), BF16),
            jax.ShapeDtypeStruct((b, ATTN_WIDTH, s), BF16),
            jax.ShapeDtypeStruct((b, N_HEADS, s, V_DIM), BF16),
            jax.ShapeDtypeStruct((b, N_HEADS, s // t, V_DIM, t), BF16),
        ] + [jax.ShapeDtypeStruct(w.shape, BF16) for w in later_weights],
        scratch_shapes=[pltpu.VMEM((POOL_HALO + tm, POOL_WIDTH), F32),
                        pltpu.VMEM((d, IN_PROJ_WIDTH), BF16),
                        pltpu.VMEM((tm, d), BF16)],
        compiler_params=pltpu.CompilerParams(
            dimension_semantics=("arbitrary", "arbitrary"),
            vmem_limit_bytes=VMEM_LIMIT_BYTES),
        name="in_proj",
    )(x, g, w_in, pool_w, pool_scale, *later_weights)


def _attn_kernel(rb_ref, qt_ref, qtn_ref, k_ref, vt_ref, bkt_ref, lq1_ref, lk1_ref, lq2_ref,
                 lk2_ref, sg_ref, out_ref, *scratch):
    t = ATTN_TILE
    m_refs, l_refs, smax_refs, acc_refs, qs_refs, s_refs = (
        scratch[i * N_HEADS:(i + 1) * N_HEADS] for i in range(6))
    qsn_refs = scratch[6 * N_HEADS:6 * N_HEADS + QK_LOOKAHEAD]
    bias_ref = scratch[6 * N_HEADS + QK_LOOKAHEAD]

    @pl.when((pl.program_id(0) == 0) & (pl.program_id(1) == 0))
    def _():
        bkt = bkt_ref[...]
        acc = [jnp.full(bkt.shape, NEG_INF, F32) for _ in range(N_HEADS)]
        for i in range(NUM_BUCKETS):
            hit = bkt == i
            acc = [jnp.where(hit, rb_ref[i, h] * LOG2E, acc[h]) for h in range(N_HEADS)]
        for h in range(N_HEADS):
            bias_ref[h] = acc[h]

    def load_q(h, src, dst_refs):
        src_ref, col0 = src
        qt = src_ref[0, h * V_DIM:(h + 1) * V_DIM, col0:col0 + t]
        row = lax.broadcasted_iota(jnp.int32, qt.shape, 0)
        zero = jnp.zeros_like(qt)
        dst_refs[h][:, 0:t] = jnp.where(row < HEAD_DIM, qt, zero)
        dst_refs[h][:, t:2 * t] = jnp.where(row >= HEAD_DIM, qt, zero)

    far_bias = [rb_ref[NUM_BUCKETS - 1, h] * LOG2E for h in range(N_HEADS)]

    lam = (jnp.exp(jnp.sum(lq1_ref[...] * lk1_ref[...], axis=1, keepdims=True))
           - jnp.exp(jnp.sum(lq2_ref[...] * lk2_ref[...], axis=1, keepdims=True))
           + LAMBDA_INIT)

    def qk(h, j, kinds, add_bias=True, q_refs=qs_refs):
        n = len(kinds) * t
        kj = k_ref[0, h, pl.ds(pl.multiple_of(j * t, t), n), :]
        s = jnp.dot(kj, q_refs[h][...], preferred_element_type=F32)
        if add_bias and has_bias(kinds):
            s = with_bias(s, h, kinds)
        s_refs[h][0:n, :] = s
        smax_refs[h][...] = jnp.max(s, axis=0, keepdims=True)

    def has_bias(kinds):
        return any(kd is not None for kd in kinds)

    def with_bias(s, h, kinds):
        b = jnp.concatenate([tile_bias(h, kd) for kd in kinds], axis=0)
        return s + jnp.concatenate([b, b], axis=1)

    def tile_bias(h, diag):
        n = t // BIAS_TILE
        pieces = {0: bias_ref[h, 0], 1: bias_ref[h, 1]}
        far = jnp.full((BIAS_TILE, BIAS_TILE), far_bias[h], F32)
        if diag is None:
            return jnp.full((t, t), far_bias[h], F32)
        masked = jnp.full((BIAS_TILE, BIAS_TILE), NEG_INF, F32)
        rows = []
        for kb in range(n):
            d = [qb - kb + (0 if diag else n) for qb in range(n)]
            rows.append(jnp.concatenate([masked if x < 0 else pieces.get(x, far) for x in d], axis=1))
        return jnp.concatenate(rows, axis=0)

    def softmax_pv(h, j, kinds, bias_pending):
        n = len(kinds) * t
        off = 0.0 if has_bias(kinds) else far_bias[h]
        s = s_refs[h][0:n, :]
        if bias_pending and has_bias(kinds):
            s = with_bias(s, h, kinds)
            smax = jnp.max(s, axis=0, keepdims=True)
        else:
            smax = smax_refs[h][...]
        m_old = m_refs[h][...]
        m_new = jnp.maximum(m_old, smax + off)
        alpha = jnp.exp2(m_old - m_new)
        p = jnp.exp2(s - (m_new - off))
        l_refs[h][...] = alpha * l_refs[h][...] + jnp.sum(p, axis=0, keepdims=True)
        m_refs[h][...] = m_new
        vt = jnp.concatenate([vt_ref[0, h, j + d] for d in range(len(kinds))], axis=1)
        r = jnp.dot(vt, p.astype(BF16), preferred_element_type=F32)
        acc_refs[h][...] = alpha * acc_refs[h][...] + r

    def run(groups, next_j, sub):
        stages = [(h, j, kinds) for (j, kinds) in groups for h in range(N_HEADS)]
        for i, (h, j, kinds) in enumerate(stages):
            ahead = i + QK_LOOKAHEAD
            if ahead < len(stages):
                qk(*stages[ahead])
            elif next_j is not None:
                qk(ahead - len(stages), next_j, (None,) * NEXT_SPAN, add_bias=False)
            else:
                nxt = (qt_ref, (sub + 1) * t) if sub + 1 < ATTN_TILES_PER_STEP else (qtn_ref, 0)
                load_q(ahead - len(stages), nxt, qsn_refs)
                qk(ahead - len(stages), 0, (None,) * NEXT_SPAN, add_bias=False, q_refs=qsn_refs)
            softmax_pv(h, j, kinds, bias_pending=i < QK_LOOKAHEAD)
            if kinds[-1] is True:
                finalize(h, sub)

    def finalize(h, sub):
        o = acc_refs[h][...] * (1.0 / l_refs[h][...])
        o = o[:, :t] - lam * o[:, t:]
        ms = jnp.mean(o * o, axis=0, keepdims=True)
        y = (o * lax.rsqrt(ms + SUBLN_EPS)).T * (sg_ref[...] * (1.0 - LAMBDA_INIT))
        out_ref[0, sub * t:(sub + 1) * t, h * V_DIM:(h + 1) * V_DIM] = y.astype(BF16)

    def query_tile(sub):
        qi = pl.program_id(1) * ATTN_TILES_PER_STEP + sub

        if sub == 0:
            @pl.when(qi == 0)
            def _():
                for h in range(QK_LOOKAHEAD):
                    load_q(h, (qt_ref, 0), qsn_refs)
                    qk(h, 0, (True,), add_bias=False, q_refs=qsn_refs)

        for h in range(QK_LOOKAHEAD):
            qs_refs[h][...] = qsn_refs[h][...]
        for h in range(QK_LOOKAHEAD, N_HEADS):
            load_q(h, (qt_ref, sub * t), qs_refs)
        for h in range(N_HEADS):
            m_refs[h][...] = jnp.full(m_refs[h].shape, -jnp.inf, F32)
            l_refs[h][...] = jnp.zeros(l_refs[h].shape, F32)
            acc_refs[h][...] = jnp.zeros(acc_refs[h].shape, F32)

        n_far = qi - 1

        def far_body(i, carry):
            j = FAR_PER_TRIP * i
            run([(j + d, (None, None)) for d in range(0, FAR_PER_TRIP, 2)], j + FAR_PER_TRIP, sub)
            return carry

        lax.fori_loop(0, jnp.maximum(n_far, 0) // FAR_PER_TRIP, far_body, 0)

        if sub == 0:
            @pl.when(qi == 0)
            def _():
                run([(qi, (True,))], None, sub)

        nq = k_ref.shape[2] // t
        for rem in range(FAR_PER_TRIP):
            if not any((q - 1) % FAR_PER_TRIP == rem
                       for q in range(1, nq) if q % ATTN_TILES_PER_STEP == sub):
                continue

            @pl.when((qi >= 1) & (n_far % FAR_PER_TRIP == rem))
            def _():
                kinds = (None,) * rem + (False, True)
                first = qi - 1 - rem
                run([(first + d, kinds[d:d + 2]) for d in range(0, len(kinds), 2)], None, sub)

    for sub in range(ATTN_TILES_PER_STEP):
        query_tile(sub)


def _diff_attn(rel_bias, qt, k, vt, lambdas, subln_g):
    b, _, s = qt.shape
    t, tps = ATTN_TILE, ATTN_TILES_PER_STEP
    nq = s // t
    per_head = lambda shape, dtype: [pltpu.VMEM(shape, dtype) for _ in range(N_HEADS)]
    return pl.pallas_call(
        _attn_kernel,
        grid=(b, nq // tps),
        in_specs=[
            pl.BlockSpec(memory_space=pltpu.SMEM),
            pl.BlockSpec((1, ATTN_WIDTH, tps * t), lambda bi, qi: (bi, 0, qi)),
            pl.BlockSpec((1, QK_LOOKAHEAD * V_DIM, t),
                         lambda bi, qi: (bi, 0, jnp.minimum((qi + 1) * tps, nq - 1))),
            pl.BlockSpec((1, N_HEADS, s, V_DIM), lambda bi, qi: (bi, 0, 0, 0)),
            pl.BlockSpec((1, N_HEADS, nq, V_DIM, t), lambda bi, qi: (bi, 0, 0, 0, 0)),
            pl.BlockSpec((2, BIAS_TILE, BIAS_TILE), lambda bi, qi: (0, 0, 0)),
        ] + [pl.BlockSpec((1, HEAD_DIM), lambda bi, qi: (0, 0)) for _ in lambdas] + [
            pl.BlockSpec((1, V_DIM), lambda bi, qi: (0, 0)),
        ],
        out_specs=pl.BlockSpec((1, tps * t, ATTN_WIDTH), lambda bi, qi: (bi, qi, 0)),
        out_shape=jax.ShapeDtypeStruct((b, s, ATTN_WIDTH), BF16),
        scratch_shapes=(per_head((1, 2 * t), F32) + per_head((1, 2 * t), F32)
                        + per_head((1, 2 * t), F32)
                        + per_head((V_DIM, 2 * t), F32) + per_head((V_DIM, 2 * t), BF16)
                        + per_head((NEXT_SPAN * t, 2 * t), F32)
                        + [pltpu.VMEM((V_DIM, 2 * t), BF16) for _ in range(QK_LOOKAHEAD)]
                        + [pltpu.VMEM((N_HEADS, 2, BIAS_TILE, BIAS_TILE), F32)]),
        compiler_params=pltpu.CompilerParams(
            dimension_semantics=("arbitrary", "arbitrary"),
            vmem_limit_bytes=VMEM_LIMIT_BYTES),
        name="diff_attn",
    )(rel_bias, qt, qt, k, vt, jnp.asarray(_bucket_tiles(BIAS_TILE)), *lambdas, subln_g)


def _ffn_kernel(x_ref, yp_ref, ya_ref, wo_ref, gf_ref, w1_ref, cw_ref, cb_ref,
                w2_ref, gl_ref, out_ref, carry_ref, gbuf_ref, act_ref, hn_ref):
    si = pl.program_id(1)
    tm = x_ref.shape[1]
    ck = FFN_CHUNK
    rows = tm // FFN_ROW_BLOCKS

    @pl.when(si == 0)
    def _():
        carry_ref[...] = jnp.zeros(carry_ref.shape, F32)

    for r0 in range(0, tm, rows):
        y = (jnp.dot(yp_ref[0, r0:r0 + rows, :], wo_ref[0:POOL_WIDTH, :], preferred_element_type=F32)
             + jnp.dot(ya_ref[0, r0:r0 + rows, :], wo_ref[POOL_WIDTH:, :], preferred_element_type=F32))
        x1 = x_ref[0, r0:r0 + rows, :] + y
        out_ref[0, r0:r0 + rows, :] = x1
        ms = jnp.mean(x1 * x1, axis=-1, keepdims=True)
        hn_ref[r0:r0 + rows, :] = (x1 * lax.rsqrt(ms + NORM_EPS) * gf_ref[...]).astype(BF16)

    for c in range(D_FF // ck):
        lo, hi = c * ck, (c + 1) * ck
        g = jnp.dot(hn_ref[...], w1_ref[:, lo:hi], preferred_element_type=F32)
        u = jnp.dot(hn_ref[...], w1_ref[:, D_FF + lo:D_FF + hi], preferred_element_type=F32)
        gbuf_ref[0:CONV_HALO, :] = carry_ref[:, lo:hi]
        gbuf_ref[CONV_HALO:, :] = g
        g1 = gbuf_ref[pl.ds(CONV_HALO - 1, tm), :]
        g2 = gbuf_ref[pl.ds(CONV_HALO - 2, tm), :]
        carry_ref[:, lo:hi] = g[tm - CONV_HALO:, :]
        cv = (cw_ref[0:1, lo:hi] * g2 + cw_ref[1:2, lo:hi] * g1 + cw_ref[2:3, lo:hi] * g
              + cb_ref[:, lo:hi])
        a = cv * (1.0 / (1.0 + jnp.exp(-cv))) * u
        act_ref[:, lo:hi] = a.astype(BF16)

    for r0 in range(0, tm, rows):
        x2 = out_ref[0, r0:r0 + rows, :] + jnp.dot(act_ref[r0:r0 + rows, :], w2_ref[...],
                                                   preferred_element_type=F32)
        ms2 = jnp.mean(x2 * x2, axis=-1, keepdims=True)
        out_ref[0, r0:r0 + rows, :] = x2 * lax.rsqrt(ms2 + NORM_EPS) * gl_ref[...]


def _out_ffn(x, y_pool, y_attn, w_out, g_ffn, w1, conv_w, conv_b, w2, g_final):
    b, s, d = x.shape
    tm = FFN_ROWS
    const = lambda bi, si: (0, 0)
    resident = functools.partial(pl.BlockSpec, index_map=const, pipeline_mode=pl.Buffered(1))
    return pl.pallas_call(
        _ffn_kernel,
        grid=(b, s // tm),
        in_specs=[
            pl.BlockSpec((1, tm, d), lambda bi, si: (bi, si, 0)),
            pl.BlockSpec((1, tm, POOL_WIDTH), lambda bi, si: (bi, si, 0)),
            pl.BlockSpec((1, tm, ATTN_WIDTH), lambda bi, si: (bi, si, 0)),
            resident((d, d)),
            resident((1, d)),
            resident((d, 2 * D_FF)),
            resident((3, D_FF)),
            resident((1, D_FF)),
            resident((D_FF, d)),
            resident((1, d)),
        ],
        out_specs=pl.BlockSpec((1, tm, d), lambda bi, si: (bi, si, 0)),
        out_shape=jax.ShapeDtypeStruct((b, s, d), F32),
        scratch_shapes=[pltpu.VMEM((CONV_HALO, D_FF), F32),
                        pltpu.VMEM((CONV_HALO + tm, FFN_CHUNK), F32),
                        pltpu.VMEM((tm, D_FF), BF16),
                        pltpu.VMEM((tm, d), BF16)],
        compiler_params=pltpu.CompilerParams(
            dimension_semantics=("arbitrary", "arbitrary"),
            vmem_limit_bytes=VMEM_LIMIT_BYTES),
        name="out_ffn",
    )(x, y_pool, y_attn, w_out, g_ffn, w1, conv_w, conv_b, w2, g_final)


def kernel(x, norm_mix_g, w_in, pool_w, pool_scale, lambda_q1, lambda_k1, lambda_q2, lambda_k2,
           subln_g, rel_bias, w_out, norm_ffn_g, ffn_w_in, ffn_conv_w, ffn_conv_b, ffn_w_out,
           norm_final_g):
    y_pool, qt, k, vt, w_out16, w1_16, w2_16 = _in_proj(
        x, norm_mix_g, w_in[0], pool_w[0], pool_scale, (w_out[0], ffn_w_in[0], ffn_w_out[0]))
    y_attn = _diff_attn(rel_bias, qt, k, vt, (lambda_q1, lambda_k1, lambda_q2, lambda_k2), subln_g)
    return _out_ffn(x, y_pool, y_attn, w_out16, norm_ffn_g, w1_16, ffn_conv_w[0], ffn_conv_b,
                    w2_16, norm_final_g.reshape(1, D_MODEL))
```

```python
import functools
import math

import numpy as np
import jax
import jax.numpy as jnp
from jax import lax
from jax.experimental import pallas as pl
from jax.experimental.pallas import tpu as pltpu

D_MODEL = 1024
POOL_WIDTH = 512
POOL_GROUPS = 4
POOL_GROUP_DIM = 128
POOL_WINDOWS = (2, 4, 8, 16)
POOL_HALO = 16
ATTN_WIDTH = 512
N_HEADS = 4
HEAD_DIM = 64
V_DIM = 128
IN_PROJ_WIDTH = 2048
NUM_BUCKETS = 32
MAX_DISTANCE = 128
D_FF = 2816
NORM_EPS = 1e-6
SUBLN_EPS = 1e-5
NEG_INF = -1e30
LAMBDA_INIT = 0.8 - 0.6 * math.exp(-0.3 * 0)
LOG2E = math.log2(math.e)

ATTN_TILE = 256
BIAS_TILE = 128
QK_LOOKAHEAD = 2
FAR_PER_TRIP = 6
NEXT_SPAN = 2
ATTN_TILES_PER_STEP = 2
IN_ROWS = 1024
IN_ROW_BLOCKS = 2
FFN_ROWS = 1024
FFN_CHUNK = 256
FFN_ROW_BLOCKS = 4
CONV_HALO = 8
WEIGHT_CAST_ROWS = 128
N_LATER_WEIGHTS = 3
VMEM_LIMIT_BYTES = 56 * 1024 * 1024

F32 = jnp.float32
BF16 = jnp.bfloat16


def _bucket_tiles(tile):
    k = np.arange(tile)[:, None]
    q = np.arange(tile)[None, :]
    out = []
    for offset in (0, tile):
        n = q - k + offset
        max_exact = NUM_BUCKETS // 2
        nf = np.maximum(n, 1).astype(np.float64)
        large = max_exact + (np.log(nf / max_exact) / math.log(MAX_DISTANCE / max_exact)
                             * (NUM_BUCKETS - max_exact)).astype(np.int64)
        large = np.minimum(large, NUM_BUCKETS - 1)
        b = np.where(n < max_exact, n, large)
        out.append(np.where(n < 0, -1, b))
    return np.stack(out).astype(np.int32)


def _in_proj_kernel(x_ref, g_ref, w32_ref, pw_ref, ps_ref, *rest):
    later32_refs = rest[:N_LATER_WEIGHTS]
    ypool_ref, qt_ref, k_ref, vt_ref = rest[N_LATER_WEIGHTS:N_LATER_WEIGHTS + 4]
    later16_refs = rest[N_LATER_WEIGHTS + 4:2 * N_LATER_WEIGHTS + 4]
    ext_ref, w_ref, hn_ref = rest[2 * N_LATER_WEIGHTS + 4:]
    si = pl.program_id(1)
    tm = x_ref.shape[1]

    @pl.when((pl.program_id(0) == 0) & (si == 0))
    def _():
        for r in range(0, D_MODEL, WEIGHT_CAST_ROWS):
            w_ref[r:r + WEIGHT_CAST_ROWS, :] = w32_ref[r:r + WEIGHT_CAST_ROWS, :].astype(BF16)

    @pl.when(si == 0)
    def _():
        ext_ref[0:POOL_HALO, :] = jnp.zeros((POOL_HALO, POOL_WIDTH), F32)

    rows = tm // IN_ROW_BLOCKS
    zp_blocks = []
    for r0 in range(0, tm, rows):
        x = x_ref[0, r0:r0 + rows, :]
        ms = jnp.mean(x * x, axis=-1, keepdims=True)
        hn = (x * lax.rsqrt(ms + NORM_EPS) * g_ref[...]).astype(BF16)
        hn_ref[r0:r0 + rows, :] = hn
        zp_blocks.append(jnp.dot(hn, w_ref[:, :POOL_WIDTH], preferred_element_type=F32))
    zp = jnp.concatenate(zp_blocks, axis=0)
    z = jnp.dot(hn_ref[...], w_ref[:, POOL_WIDTH:], preferred_element_type=F32)

    for src_ref, dst_ref in zip(later32_refs, later16_refs):
        dst_ref[...] = src_ref[...].astype(BF16)

    ext_ref[POOL_HALO:, :] = zp
    tpos = si * tm + lax.broadcasted_iota(jnp.int32, (tm, 1), 0) + 1
    for gi, w in enumerate(POOL_WINDOWS):
        lo, hi = gi * POOL_GROUP_DIM, (gi + 1) * POOL_GROUP_DIM
        tot = ext_ref[:, lo:hi]
        step = 1
        while step < w:
            tot = tot + pltpu.roll(tot, step, axis=0)
            step *= 2
        tot = tot[POOL_HALO:, :]
        cnt = jnp.minimum(tpos, w).astype(F32)
        pooled = tot / cnt - zp[:, lo:hi]
        yg = jnp.dot(pooled.astype(BF16), pw_ref[gi].astype(BF16), preferred_element_type=F32)
        ypool_ref[0, :, lo:hi] = (yg * ps_ref[:, lo:hi]).astype(BF16)
    ext_ref[0:POOL_HALO, :] = zp[tm - POOL_HALO:, :]

    t = ATTN_TILE
    q0, k0, v0 = 0, ATTN_WIDTH, 2 * ATTN_WIDTH
    scale = LOG2E * HEAD_DIM ** -0.5
    for h in range(N_HEADS):
        zq = z[:, q0 + h * V_DIM:q0 + (h + 1) * V_DIM] * scale
        qt_ref[0, h * V_DIM:(h + 1) * V_DIM, :] = zq.T.astype(BF16)
        k_ref[0, h] = z[:, k0 + h * V_DIM:k0 + (h + 1) * V_DIM].astype(BF16)
        vt = z[:, v0 + h * V_DIM:v0 + (h + 1) * V_DIM].T.astype(BF16)
        for j in range(tm // t):
            vt_ref[0, h, j] = vt[:, j * t:(j + 1) * t]


def _in_proj(x, g, w_in, pool_w, pool_scale, later_weights):
    b, s, d = x.shape
    tm, t = IN_ROWS, ATTN_TILE
    steps = s // tm
    assert len(later_weights) == N_LATER_WEIGHTS
    slabs = [(w.shape[0] // (b * steps), w.shape[1]) for w in later_weights]
    slab_specs = [pl.BlockSpec(sl, lambda bi, si: (bi * steps + si, 0)) for sl in slabs]
    return pl.pallas_call(
        _in_proj_kernel,
        grid=(b, s // tm),
        in_specs=[
            pl.BlockSpec((1, tm, d), lambda bi, si: (bi, si, 0)),
            pl.BlockSpec((1, d), lambda bi, si: (0, 0)),
            pl.BlockSpec((d, IN_PROJ_WIDTH), lambda bi, si: (0, 0), pipeline_mode=pl.Buffered(1)),
            pl.BlockSpec((POOL_GROUPS, POOL_GROUP_DIM, POOL_GROUP_DIM), lambda bi, si: (0, 0, 0)),
            pl.BlockSpec((1, POOL_WIDTH), lambda bi, si: (0, 0)),
        ] + slab_specs,
        out_specs=[
            pl.BlockSpec((1, tm, POOL_WIDTH), lambda bi, si: (bi, si, 0)),
            pl.BlockSpec((1, ATTN_WIDTH, tm), lambda bi, si: (bi, 0, si)),
            pl.BlockSpec((1, N_HEADS, tm, V_DIM), lambda bi, si: (bi, 0, si, 0)),
            pl.BlockSpec((1, N_HEADS, tm // t, V_DIM, t), lambda bi, si: (bi, 0, si, 0, 0)),
        ] + slab_specs,
        out_shape=[
            jax.ShapeDtypeStruct((b, s, POOL_WIDTH), BF16),
            jax.ShapeDtypeStruct((b, ATTN_WIDTH, s), BF16),
            jax.ShapeDtypeStruct((b, N_HEADS, s, V_DIM), BF16),
            jax.ShapeDtypeStruct((b, N_HEADS, s // t, V_DIM, t), BF16),
        ] + [jax.ShapeDtypeStruct(w.shape, BF16) for w in later_weights],
        scratch_shapes=[pltpu.VMEM((POOL_HALO + tm, POOL_WIDTH), F32),
                        pltpu.VMEM((d, IN_PROJ_WIDTH), BF16),
                        pltpu.VMEM((tm, d), BF16)],
        compiler_params=pltpu.CompilerParams(
            dimension_semantics=("arbitrary", "arbitrary"),
            vmem_limit_bytes=VMEM_LIMIT_BYTES),
        name="in_proj",
    )(x, g, w_in, pool_w, pool_scale, *later_weights)


def _attn_kernel(rb_ref, qt_ref, qtn_ref, k_ref, vt_ref, bkt_ref, lq1_ref, lk1_ref, lq2_ref,
                 lk2_ref, sg_ref, out_ref, *scratch):
    t = ATTN_TILE
    m_refs, l_refs, smax_refs, acc_refs, qs_refs, s_refs = (
        scratch[i * N_HEADS:(i + 1) * N_HEADS] for i in range(6))
    qsn_refs = scratch[6 * N_HEADS:6 * N_HEADS + QK_LOOKAHEAD]
    bias_ref = scratch[6 * N_HEADS + QK_LOOKAHEAD]

    @pl.when((pl.program_id(0) == 0) & (pl.program_id(1) == 0))
    def _():
        bkt = bkt_ref[...]
        acc = [jnp.full(bkt.shape, NEG_INF, F32) for _ in range(N_HEADS)]
        for i in range(NUM_BUCKETS):
            hit = bkt == i
            acc = [jnp.where(hit, rb_ref[i, h] * LOG2E, acc[h]) for h in range(N_HEADS)]
        for h in range(N_HEADS):
            bias_ref[h] = acc[h]

    def load_q(h, src, dst_refs):
        src_ref, col0 = src
        qt = src_ref[0, h * V_DIM:(h + 1) * V_DIM, col0:col0 + t]
        row = lax.broadcasted_iota(jnp.int32, qt.shape, 0)
        zero = jnp.zeros_like(qt)
        dst_refs[h][:, 0:t] = jnp.where(row < HEAD_DIM, qt, zero)
        dst_refs[h][:, t:2 * t] = jnp.where(row >= HEAD_DIM, qt, zero)

    far_bias = [rb_ref[NUM_BUCKETS - 1, h] * LOG2E for h in range(N_HEADS)]

    lam = (jnp.exp(jnp.sum(lq1_ref[...] * lk1_ref[...], axis=1, keepdims=True))
           - jnp.exp(jnp.sum(lq2_ref[...] * lk2_ref[...], axis=1, keepdims=True))
           + LAMBDA_INIT)

    def qk(h, j, kinds, add_bias=True, q_refs=qs_refs):
        n = len(kinds) * t
        kj = k_ref[0, h, pl.ds(pl.multiple_of(j * t, t), n), :]
        s = jnp.dot(kj, q_refs[h][...], preferred_element_type=F32)
        if add_bias and has_bias(kinds):
            s = with_bias(s, h, kinds)
        s_refs[h][0:n, :] = s
        smax_refs[h][...] = jnp.max(s, axis=0, keepdims=True)

    def has_bias(kinds):
        return any(kd is not None for kd in kinds)

    def with_bias(s, h, kinds):
        b = jnp.concatenate([tile_bias(h, kd) for kd in kinds], axis=0)
        return s + jnp.concatenate([b, b], axis=1)

    def tile_bias(h, diag):
        n = t // BIAS_TILE
        pieces = {0: bias_ref[h, 0], 1: bias_ref[h, 1]}
        far = jnp.full((BIAS_TILE, BIAS_TILE), far_bias[h], F32)
        if diag is None:
            return jnp.full((t, t), far_bias[h], F32)
        masked = jnp.full((BIAS_TILE, BIAS_TILE), NEG_INF, F32)
        rows = []
        for kb in range(n):
            d = [qb - kb + (0 if diag else n) for qb in range(n)]
            rows.append(jnp.concatenate([masked if x < 0 else pieces.get(x, far) for x in d], axis=1))
        return jnp.concatenate(rows, axis=0)

    def softmax_pv(h, j, kinds, bias_pending):
        n = len(kinds) * t
        off = 0.0 if has_bias(kinds) else far_bias[h]
        s = s_refs[h][0:n, :]
        if bias_pending and has_bias(kinds):
            s = with_bias(s, h, kinds)
            smax = jnp.max(s, axis=0, keepdims=True)
        else:
            smax = smax_refs[h][...]
        m_old = m_refs[h][...]
        m_new = jnp.maximum(m_old, smax + off)
        alpha = jnp.exp2(m_old - m_new)
        p = jnp.exp2(s - (m_new - off))
        l_refs[h][...] = alpha * l_refs[h][...] + jnp.sum(p, axis=0, keepdims=True)
        m_refs[h][...] = m_new
        vt = jnp.concatenate([vt_ref[0, h, j + d] for d in range(len(kinds))], axis=1)
        r = jnp.dot(vt, p.astype(BF16), preferred_element_type=F32)
        acc_refs[h][...] = alpha * acc_refs[h][...] + r

    def run(groups, next_j, sub):
        stages = [(h, j, kinds) for (j, kinds) in groups for h in range(N_HEADS)]
        for i, (h, j, kinds) in enumerate(stages):
            ahead = i + QK_LOOKAHEAD
            if ahead < len(stages):
                qk(*stages[ahead])
            elif next_j is not None:
                qk(ahead - len(stages), next_j, (None,) * NEXT_SPAN, add_bias=False)
            else:
                nxt = (qt_ref, (sub + 1) * t) if sub + 1 < ATTN_TILES_PER_STEP else (qtn_ref, 0)
                load_q(ahead - len(stages), nxt, qsn_refs)
                qk(ahead - len(stages), 0, (None,) * NEXT_SPAN, add_bias=False, q_refs=qsn_refs)
            softmax_pv(h, j, kinds, bias_pending=i < QK_LOOKAHEAD)
            if kinds[-1] is True:
                finalize(h, sub)

    def finalize(h, sub):
        o = acc_refs[h][...] * (1.0 / l_refs[h][...])
        o = o[:, :t] - lam * o[:, t:]
        ms = jnp.mean(o * o, axis=0, keepdims=True)
        y = (o * lax.rsqrt(ms + SUBLN_EPS)).T * (sg_ref[...] * (1.0 - LAMBDA_INIT))
        out_ref[0, sub * t:(sub + 1) * t, h * V_DIM:(h + 1) * V_DIM] = y.astype(BF16)

    def query_tile(sub):
        qi = pl.program_id(1) * ATTN_TILES_PER_STEP + sub

        if sub == 0:
            @pl.when(qi == 0)
            def _():
                for h in range(QK_LOOKAHEAD):
                    load_q(h, (qt_ref, 0), qsn_refs)
                    qk(h, 0, (True,), add_bias=False, q_refs=qsn_refs)

        for h in range(QK_LOOKAHEAD):
            qs_refs[h][...] = qsn_refs[h][...]
        for h in range(QK_LOOKAHEAD, N_HEADS):
            load_q(h, (qt_ref, sub * t), qs_refs)
        for h in range(N_HEADS):
            m_refs[h][...] = jnp.full(m_refs[h].shape, -jnp.inf, F32)
            l_refs[h][...] = jnp.zeros(l_refs[h].shape, F32)
            acc_refs[h][...] = jnp.zeros(acc_refs[h].shape, F32)

        n_far = qi - 1

        def far_body(i, carry):
            j = FAR_PER_TRIP * i
            run([(j + d, (None, None)) for d in range(0, FAR_PER_TRIP, 2)], j + FAR_PER_TRIP, sub)
            return carry

        lax.fori_loop(0, jnp.maximum(n_far, 0) // FAR_PER_TRIP, far_body, 0)

        if sub == 0:
            @pl.when(qi == 0)
            def _():
                run([(qi, (True,))], None, sub)

        nq = k_ref.shape[2] // t
        for rem in range(FAR_PER_TRIP):
            if not any((q - 1) % FAR_PER_TRIP == rem
                       for q in range(1, nq) if q % ATTN_TILES_PER_STEP == sub):
                continue

            @pl.when((qi >= 1) & (n_far % FAR_PER_TRIP == rem))
            def _():
                kinds = (None,) * rem + (False, True)
                first = qi - 1 - rem
                run([(first + d, kinds[d:d + 2]) for d in range(0, len(kinds), 2)], None, sub)

    for sub in range(ATTN_TILES_PER_STEP):
        query_tile(sub)


def _diff_attn(rel_bias, qt, k, vt, lambdas, subln_g):
    b, _, s = qt.shape
    t, tps = ATTN_TILE, ATTN_TILES_PER_STEP
    nq = s // t
    per_head = lambda shape, dtype: [pltpu.VMEM(shape, dtype) for _ in range(N_HEADS)]
    return pl.pallas_call(
        _attn_kernel,
        grid=(b, nq // tps),
        in_specs=[
            pl.BlockSpec(memory_space=pltpu.SMEM),
            pl.BlockSpec((1, ATTN_WIDTH, tps * t), lambda bi, qi: (bi, 0, qi)),
            pl.BlockSpec((1, QK_LOOKAHEAD * V_DIM, t),
                         lambda bi, qi: (bi, 0, jnp.minimum((qi + 1) * tps, nq - 1))),
            pl.BlockSpec((1, N_HEADS, s, V_DIM), lambda bi, qi: (bi, 0, 0, 0)),
            pl.BlockSpec((1, N_HEADS, nq, V_DIM, t), lambda bi, qi: (bi, 0, 0, 0, 0)),
            pl.BlockSpec((2, BIAS_TILE, BIAS_TILE), lambda bi, qi: (0, 0, 0)),
        ] + [pl.BlockSpec((1, HEAD_DIM), lambda bi, qi: (0, 0)) for _ in lambdas] + [
            pl.BlockSpec((1, V_DIM), lambda bi, qi: (0, 0)),
        ],
        out_specs=pl.BlockSpec((1, tps * t, ATTN_WIDTH), lambda bi, qi: (bi, qi, 0)),
        out_shape=jax.ShapeDtypeStruct((b, s, ATTN_WIDTH), BF16),
        scratch_shapes=(per_head((1, 2 * t), F32) + per_head((1, 2 * t), F32)
                        + per_head((1, 2 * t), F32)
                        + per_head((V_DIM, 2 * t), F32) + per_head((V_DIM, 2 * t), BF16)
                        + per_head((NEXT_SPAN * t, 2 * t), F32)
                        + [pltpu.VMEM((V_DIM, 2 * t), BF16) for _ in range(QK_LOOKAHEAD)]
                        + [pltpu.VMEM((N_HEADS, 2, BIAS_TILE, BIAS_TILE), F32)]),
        compiler_params=pltpu.CompilerParams(
            dimension_semantics=("arbitrary", "arbitrary"),
            vmem_limit_bytes=VMEM_LIMIT_BYTES),
        name="diff_attn",
    )(rel_bias, qt, qt, k, vt, jnp.asarray(_bucket_tiles(BIAS_TILE)), *lambdas, subln_g)


def _ffn_kernel(x_ref, yp_ref, ya_ref, wo_ref, gf_ref, w1_ref, cw_ref, cb_ref,
                w2_ref, gl_ref, out_ref, carry_ref, gbuf_ref, act_ref, hn_ref):
    si = pl.program_id(1)
    tm = x_ref.shape[1]
    ck = FFN_CHUNK
    rows = tm // FFN_ROW_BLOCKS

    @pl.when(si == 0)
    def _():
        carry_ref[...] = jnp.zeros(carry_ref.shape, F32)

    for r0 in range(0, tm, rows):
        y = (jnp.dot(yp_ref[0, r0:r0 + rows, :], wo_ref[0:POOL_WIDTH, :], preferred_element_type=F32)
             + jnp.dot(ya_ref[0, r0:r0 + rows, :], wo_ref[POOL_WIDTH:, :], preferred_element_type=F32))
        x1 = x_ref[0, r0:r0 + rows, :] + y
        out_ref[0, r0:r0 + rows, :] = x1
        ms = jnp.mean(x1 * x1, axis=-1, keepdims=True)
        hn_ref[r0:r0 + rows, :] = (x1 * lax.rsqrt(ms + NORM_EPS) * gf_ref[...]).astype(BF16)

    for c in range(D_FF // ck):
        lo, hi = c * ck, (c + 1) * ck
        g = jnp.dot(hn_ref[...], w1_ref[:, lo:hi], preferred_element_type=F32)
        u = jnp.dot(hn_ref[...], w1_ref[:, D_FF + lo:D_FF + hi], preferred_element_type=F32)
        gbuf_ref[0:CONV_HALO, :] = carry_ref[:, lo:hi]
        gbuf_ref[CONV_HALO:, :] = g
        g1 = gbuf_ref[pl.ds(CONV_HALO - 1, tm), :]
        g2 = gbuf_ref[pl.ds(CONV_HALO - 2, tm), :]
        carry_ref[:, lo:hi] = g[tm - CONV_HALO:, :]
        cv = (cw_ref[0, 0:1, lo:hi] * g2 + cw_ref[0, 1:2, lo:hi] * g1 + cw_ref[0, 2:3, lo:hi] * g
              + cb_ref[:, lo:hi])
        a = cv * (1.0 / (1.0 + jnp.exp(-cv))) * u
        act_ref[:, lo:hi] = a.astype(BF16)

    for r0 in range(0, tm, rows):
        x2 = out_ref[0, r0:r0 + rows, :] + jnp.dot(act_ref[r0:r0 + rows, :], w2_ref[...],
                                                   preferred_element_type=F32)
        ms2 = jnp.mean(x2 * x2, axis=-1, keepdims=True)
        out_ref[0, r0:r0 + rows, :] = x2 * lax.rsqrt(ms2 + NORM_EPS) * gl_ref[...]


def _out_ffn(x, y_pool, y_attn, w_out, g_ffn, w1, conv_w, conv_b, w2, g_final):
    b, s, d = x.shape
    tm = FFN_ROWS
    const = lambda bi, si: (0, 0)
    resident = functools.partial(pl.BlockSpec, index_map=const, pipeline_mode=pl.Buffered(1))
    return pl.pallas_call(
        _ffn_kernel,
        grid=(b, s // tm),
        in_specs=[
            pl.BlockSpec((1, tm, d), lambda bi, si: (bi, si, 0)),
            pl.BlockSpec((1, tm, POOL_WIDTH), lambda bi, si: (bi, si, 0)),
            pl.BlockSpec((1, tm, ATTN_WIDTH), lambda bi, si: (bi, si, 0)),
            resident((d, d)),
            resident((1, d)),
            resident((d, 2 * D_FF)),
            pl.BlockSpec((1, 3, D_FF), lambda bi, si: (0, 0, 0), pipeline_mode=pl.Buffered(1)),
            resident((1, D_FF)),
            resident((D_FF, d)),
            resident((1, d)),
        ],
        out_specs=pl.BlockSpec((1, tm, d), lambda bi, si: (bi, si, 0)),
        out_shape=jax.ShapeDtypeStruct((b, s, d), F32),
        scratch_shapes=[pltpu.VMEM((CONV_HALO, D_FF), F32),
                        pltpu.VMEM((CONV_HALO + tm, FFN_CHUNK), F32),
                        pltpu.VMEM((tm, D_FF), BF16),
                        pltpu.VMEM((tm, d), BF16)],
        compiler_params=pltpu.CompilerParams(
            dimension_semantics=("arbitrary", "arbitrary"),
            vmem_limit_bytes=VMEM_LIMIT_BYTES),
        name="out_ffn",
    )(x, y_pool, y_attn, w_out, g_ffn, w1, conv_w, conv_b, w2, g_final)


def kernel(x, norm_mix_g, w_in, pool_w, pool_scale, lambda_q1, lambda_k1, lambda_q2, lambda_k2,
           subln_g, rel_bias, w_out, norm_ffn_g, ffn_w_in, ffn_conv_w, ffn_conv_b, ffn_w_out,
           norm_final_g):
    y_pool, qt, k, vt, w_out16, w1_16, w2_16 = _in_proj(
        x, norm_mix_g, w_in[0], pool_w[0], pool_scale, (w_out[0], ffn_w_in[0], ffn_w_out[0]))
    y_attn = _diff_attn(rel_bias, qt, k, vt, (lambda_q1, lambda_k1, lambda_q2, lambda_k2), subln_g)
    return _out_ffn(x, y_pool, y_attn, w_out16, norm_ffn_g, w1_16, ffn_conv_w, ffn_conv_b,
                    w2_16, norm_final_g.reshape(1, D_MODEL))
```

```python
import functools
import math

import numpy as np
import jax
import jax.numpy as jnp
from jax import lax
from jax.experimental import pallas as pl
from jax.experimental.pallas import tpu as pltpu

D_MODEL = 1024
POOL_WIDTH = 512
POOL_GROUPS = 4
POOL_GROUP_DIM = 128
POOL_WINDOWS = (2, 4, 8, 16)
POOL_HALO = 16
ATTN_WIDTH = 512
N_HEADS = 4
HEAD_DIM = 64
V_DIM = 128
IN_PROJ_WIDTH = 2048
NUM_BUCKETS = 32
MAX_DISTANCE = 128
D_FF = 2816
NORM_EPS = 1e-6
SUBLN_EPS = 1e-5
NEG_INF = -1e30
LAMBDA_INIT = 0.8 - 0.6 * math.exp(-0.3 * 0)
LOG2E = math.log2(math.e)

ATTN_TILE = 256
BIAS_TILE = 128
QK_LOOKAHEAD = 2
FAR_PER_TRIP = 4
NEXT_SPAN = 2
ATTN_TILES_PER_STEP = 2
IN_ROWS = 1024
IN_ROW_BLOCKS = 2
FFN_ROWS = 1024
FFN_CHUNK = 256
FFN_ROW_BLOCKS = 4
CONV_HALO = 8
WEIGHT_CAST_ROWS = 128
N_LATER_WEIGHTS = 3
VMEM_LIMIT_BYTES = 56 * 1024 * 1024
BF16_SUBLANES = 16

F32 = jnp.float32
BF16 = jnp.bfloat16


def _causal_bucket(n):
    max_exact = NUM_BUCKETS // 2
    nf = np.maximum(n, 1).astype(np.float64)
    large = max_exact + (np.log(nf / max_exact) / math.log(MAX_DISTANCE / max_exact)
                         * (NUM_BUCKETS - max_exact)).astype(np.int64)
    return np.where(n < max_exact, n, np.minimum(large, NUM_BUCKETS - 1))


def _bucket_tiles(tile):
    k = np.arange(tile)[:, None]
    q = np.arange(tile)[None, :]
    out = []
    for offset in (0, tile):
        n = q - k + offset
        out.append(np.where(n < 0, -1, _causal_bucket(n)))
    return np.stack(out).astype(np.int32)


def _in_proj_kernel(x_ref, g_ref, w32_ref, pw_ref, ps_ref, *rest):
    later32_refs = rest[:N_LATER_WEIGHTS]
    ypool_ref, qt_ref, k_ref, vt_ref = rest[N_LATER_WEIGHTS:N_LATER_WEIGHTS + 4]
    later16_refs = rest[N_LATER_WEIGHTS + 4:2 * N_LATER_WEIGHTS + 4]
    ext_ref, w_ref, hn_ref = rest[2 * N_LATER_WEIGHTS + 4:]
    si = pl.program_id(1)
    tm = x_ref.shape[1]

    @pl.when((pl.program_id(0) == 0) & (si == 0))
    def _():
        for r in range(0, D_MODEL, WEIGHT_CAST_ROWS):
            w_ref[r:r + WEIGHT_CAST_ROWS, :] = w32_ref[r:r + WEIGHT_CAST_ROWS, :].astype(BF16)

    @pl.when(si == 0)
    def _():
        ext_ref[0:POOL_HALO, :] = jnp.zeros((POOL_HALO, POOL_WIDTH), F32)

    rows = tm // IN_ROW_BLOCKS
    zp_blocks = []
    for r0 in range(0, tm, rows):
        x = x_ref[0, r0:r0 + rows, :]
        ms = jnp.mean(x * x, axis=-1, keepdims=True)
        hn = (x * lax.rsqrt(ms + NORM_EPS) * g_ref[...]).astype(BF16)
        hn_ref[r0:r0 + rows, :] = hn
        zp_blocks.append(jnp.dot(hn, w_ref[:, :POOL_WIDTH], preferred_element_type=F32))
    zp = jnp.concatenate(zp_blocks, axis=0)
    z = jnp.dot(hn_ref[...], w_ref[:, POOL_WIDTH:], preferred_element_type=F32)

    for src_ref, dst_ref in zip(later32_refs, later16_refs):
        dst_ref[...] = src_ref[...].astype(BF16)

    ext_ref[POOL_HALO:, :] = zp
    tpos = si * tm + lax.broadcasted_iota(jnp.int32, (tm, 1), 0) + 1
    for gi, w in enumerate(POOL_WINDOWS):
        lo, hi = gi * POOL_GROUP_DIM, (gi + 1) * POOL_GROUP_DIM
        tot = ext_ref[:, lo:hi]
        step = 1
        while step < w:
            tot = tot + pltpu.roll(tot, step, axis=0)
            step *= 2
        tot = tot[POOL_HALO:, :]
        cnt = jnp.minimum(tpos, w).astype(F32)
        pooled = tot / cnt - zp[:, lo:hi]
        yg = jnp.dot(pooled.astype(BF16), pw_ref[gi].astype(BF16), preferred_element_type=F32)
        ypool_ref[0, :, lo:hi] = (yg * ps_ref[:, lo:hi]).astype(BF16)
    ext_ref[0:POOL_HALO, :] = zp[tm - POOL_HALO:, :]

    t = ATTN_TILE
    q0, k0, v0 = 0, ATTN_WIDTH, 2 * ATTN_WIDTH
    scale = LOG2E * HEAD_DIM ** -0.5
    for h in range(N_HEADS):
        zq = z[:, q0 + h * V_DIM:q0 + (h + 1) * V_DIM] * scale
        qt_ref[0, h * V_DIM:(h + 1) * V_DIM, :] = zq.T.astype(BF16)
        k_ref[0, h] = z[:, k0 + h * V_DIM:k0 + (h + 1) * V_DIM].astype(BF16)
        vt = z[:, v0 + h * V_DIM:v0 + (h + 1) * V_DIM].T.astype(BF16)
        for j in range(tm // t):
            vt_ref[0, h, j] = vt[:, j * t:(j + 1) * t]


def _in_proj(x, g, w_in, pool_w, pool_scale, later_weights):
    b, s, d = x.shape
    tm, t = IN_ROWS, ATTN_TILE
    steps = s // tm
    assert s % tm == 0 and tm % t == 0 and tm % IN_ROW_BLOCKS == 0
    assert len(later_weights) == N_LATER_WEIGHTS
    assert all(w.shape[0] % (b * steps * BF16_SUBLANES) == 0 for w in later_weights)
    slabs = [(w.shape[0] // (b * steps), w.shape[1]) for w in later_weights]
    slab_specs = [pl.BlockSpec(sl, lambda bi, si: (bi * steps + si, 0)) for sl in slabs]
    return pl.pallas_call(
        _in_proj_kernel,
        grid=(b, s // tm),
        in_specs=[
            pl.BlockSpec((1, tm, d), lambda bi, si: (bi, si, 0)),
            pl.BlockSpec((1, d), lambda bi, si: (0, 0)),
            pl.BlockSpec((d, IN_PROJ_WIDTH), lambda bi, si: (0, 0), pipeline_mode=pl.Buffered(1)),
            pl.BlockSpec((POOL_GROUPS, POOL_GROUP_DIM, POOL_GROUP_DIM), lambda bi, si: (0, 0, 0)),
            pl.BlockSpec((1, POOL_WIDTH), lambda bi, si: (0, 0)),
        ] + slab_specs,
        out_specs=[
            pl.BlockSpec((1, tm, POOL_WIDTH), lambda bi, si: (bi, si, 0)),
            pl.BlockSpec((1, ATTN_WIDTH, tm), lambda bi, si: (bi, 0, si)),
            pl.BlockSpec((1, N_HEADS, tm, V_DIM), lambda bi, si: (bi, 0, si, 0)),
            pl.BlockSpec((1, N_HEADS, tm // t, V_DIM, t), lambda bi, si: (bi, 0, si, 0, 0)),
        ] + slab_specs,
        out_shape=[
            jax.ShapeDtypeStruct((b, s, POOL_WIDTH), BF16),
            jax.ShapeDtypeStruct((b, ATTN_WIDTH, s), BF16),
            jax.ShapeDtypeStruct((b, N_HEADS, s, V_DIM), BF16),
            jax.ShapeDtypeStruct((b, N_HEADS, s // t, V_DIM, t), BF16),
        ] + [jax.ShapeDtypeStruct(w.shape, BF16) for w in later_weights],
        scratch_shapes=[pltpu.VMEM((POOL_HALO + tm, POOL_WIDTH), F32),
                        pltpu.VMEM((d, IN_PROJ_WIDTH), BF16),
                        pltpu.VMEM((tm, d), BF16)],
        compiler_params=pltpu.CompilerParams(
            dimension_semantics=("arbitrary", "arbitrary"),
            vmem_limit_bytes=VMEM_LIMIT_BYTES),
        name="in_proj",
    )(x, g, w_in, pool_w, pool_scale, *later_weights)


def _attn_kernel(rb_ref, qt_ref, qtn_ref, k_ref, vt_ref, bkt_ref, lq1_ref, lk1_ref, lq2_ref,
                 lk2_ref, sg_ref, out_ref, *scratch):
    t = ATTN_TILE
    m_refs, l_refs, smax_refs, acc_refs, qs_refs, s_refs = (
        scratch[i * N_HEADS:(i + 1) * N_HEADS] for i in range(6))
    qsn_refs = scratch[6 * N_HEADS:6 * N_HEADS + QK_LOOKAHEAD]
    bias_ref = scratch[6 * N_HEADS + QK_LOOKAHEAD]

    @pl.when((pl.program_id(0) == 0) & (pl.program_id(1) == 0))
    def _():
        bkt = bkt_ref[...]
        acc = [jnp.full(bkt.shape, NEG_INF, F32) for _ in range(N_HEADS)]
        for i in range(NUM_BUCKETS):
            hit = bkt == i
            acc = [jnp.where(hit, rb_ref[i, h] * LOG2E, acc[h]) for h in range(N_HEADS)]
        for h in range(N_HEADS):
            bias_ref[h] = acc[h]

    def load_q(h, src, dst_refs):
        src_ref, col0 = src
        qt = src_ref[0, h * V_DIM:(h + 1) * V_DIM, col0:col0 + t]
        row = lax.broadcasted_iota(jnp.int32, qt.shape, 0)
        zero = jnp.zeros_like(qt)
        dst_refs[h][:, 0:t] = jnp.where(row < HEAD_DIM, qt, zero)
        dst_refs[h][:, t:2 * t] = jnp.where(row >= HEAD_DIM, qt, zero)

    far_bias = [rb_ref[NUM_BUCKETS - 1, h] * LOG2E for h in range(N_HEADS)]

    lam = (jnp.exp(jnp.sum(lq1_ref[...] * lk1_ref[...], axis=1, keepdims=True))
           - jnp.exp(jnp.sum(lq2_ref[...] * lk2_ref[...], axis=1, keepdims=True))
           + LAMBDA_INIT)

    def qk(h, j, kinds, add_bias=True, q_refs=qs_refs):
        n = len(kinds) * t
        kj = k_ref[0, h, pl.ds(pl.multiple_of(j * t, t), n), :]
        s = jnp.dot(kj, q_refs[h][...], preferred_element_type=F32)
        if add_bias and has_bias(kinds):
            s = with_bias(s, h, kinds)
        s_refs[h][0:n, :] = s
        smax_refs[h][...] = jnp.max(s, axis=0, keepdims=True)

    def has_bias(kinds):
        return any(kd is not None for kd in kinds)

    def with_bias(s, h, kinds):
        b = jnp.concatenate([tile_bias(h, kd) for kd in kinds], axis=0)
        return s + jnp.concatenate([b, b], axis=1)

    def tile_bias(h, diag):
        n = t // BIAS_TILE
        pieces = {0: bias_ref[h, 0], 1: bias_ref[h, 1]}
        far = jnp.full((BIAS_TILE, BIAS_TILE), far_bias[h], F32)
        if diag is None:
            return jnp.full((t, t), far_bias[h], F32)
        masked = jnp.full((BIAS_TILE, BIAS_TILE), NEG_INF, F32)
        rows = []
        for kb in range(n):
            d = [qb - kb + (0 if diag else n) for qb in range(n)]
            rows.append(jnp.concatenate([masked if x < 0 else pieces.get(x, far) for x in d], axis=1))
        return jnp.concatenate(rows, axis=0)

    def softmax_pv(h, j, kinds, bias_pending):
        n = len(kinds) * t
        off = 0.0 if has_bias(kinds) else far_bias[h]
        s = s_refs[h][0:n, :]
        if bias_pending and has_bias(kinds):
            s = with_bias(s, h, kinds)
            smax = jnp.max(s, axis=0, keepdims=True)
        else:
            smax = smax_refs[h][...]
        m_old = m_refs[h][...]
        m_new = jnp.maximum(m_old, smax + off)
        alpha = jnp.exp2(m_old - m_new)
        p = jnp.exp2(s - (m_new - off))
        l_refs[h][...] = alpha * l_refs[h][...] + jnp.sum(p, axis=0, keepdims=True)
        m_refs[h][...] = m_new
        vt = jnp.concatenate([vt_ref[0, h, j + d] for d in range(len(kinds))], axis=1)
        r = jnp.dot(vt, p.astype(BF16), preferred_element_type=F32)
        acc_refs[h][...] = alpha * acc_refs[h][...] + r

    def run(groups, next_j, sub):
        stages = [(h, j, kinds) for (j, kinds) in groups for h in range(N_HEADS)]
        for i, (h, j, kinds) in enumerate(stages):
            ahead = i + QK_LOOKAHEAD
            if ahead < len(stages):
                qk(*stages[ahead])
            elif next_j is not None:
                qk(ahead - len(stages), next_j, (None,) * NEXT_SPAN, add_bias=False)
            else:
                nxt = (qt_ref, (sub + 1) * t) if sub + 1 < ATTN_TILES_PER_STEP else (qtn_ref, 0)
                load_q(ahead - len(stages), nxt, qsn_refs)
                qk(ahead - len(stages), 0, (None,) * NEXT_SPAN, add_bias=False, q_refs=qsn_refs)
            softmax_pv(h, j, kinds, bias_pending=i < QK_LOOKAHEAD)
            if kinds[-1] is True:
                finalize(h, sub)

    def finalize(h, sub):
        o = acc_refs[h][...] * (1.0 / l_refs[h][...])
        o = o[:, :t] - lam * o[:, t:]
        ms = jnp.mean(o * o, axis=0, keepdims=True)
        y = (o * lax.rsqrt(ms + SUBLN_EPS)).T * (sg_ref[...] * (1.0 - LAMBDA_INIT))
        out_ref[0, sub * t:(sub + 1) * t, h * V_DIM:(h + 1) * V_DIM] = y.astype(BF16)

    def query_tile(sub):
        qi = pl.program_id(1) * ATTN_TILES_PER_STEP + sub

        if sub == 0:
            @pl.when(qi == 0)
            def _():
                for h in range(QK_LOOKAHEAD):
                    load_q(h, (qt_ref, 0), qsn_refs)
                    qk(h, 0, (True,), add_bias=False, q_refs=qsn_refs)

        for h in range(QK_LOOKAHEAD):
            qs_refs[h][...] = qsn_refs[h][...]
        for h in range(QK_LOOKAHEAD, N_HEADS):
            load_q(h, (qt_ref, sub * t), qs_refs)
        for h in range(N_HEADS):
            m_refs[h][...] = jnp.full(m_refs[h].shape, -jnp.inf, F32)
            l_refs[h][...] = jnp.zeros(l_refs[h].shape, F32)
            acc_refs[h][...] = jnp.zeros(acc_refs[h].shape, F32)

        n_far = qi - 1

        def far_body(i, carry):
            j = FAR_PER_TRIP * i
            run([(j + d, (None, None)) for d in range(0, FAR_PER_TRIP, 2)], j + FAR_PER_TRIP, sub)
            return carry

        lax.fori_loop(0, jnp.maximum(n_far, 0) // FAR_PER_TRIP, far_body, 0)

        if sub == 0:
            @pl.when(qi == 0)
            def _():
                run([(qi, (True,))], None, sub)

        nq = k_ref.shape[2] // t
        for rem in range(FAR_PER_TRIP):
            if not any((q - 1) % FAR_PER_TRIP == rem
                       for q in range(1, nq) if q % ATTN_TILES_PER_STEP == sub):
                continue

            @pl.when((qi >= 1) & (n_far % FAR_PER_TRIP == rem))
            def _():
                kinds = (None,) * rem + (False, True)
                first = qi - 1 - rem
                run([(first + d, kinds[d:d + 2]) for d in range(0, len(kinds), 2)], None, sub)

    for sub in range(ATTN_TILES_PER_STEP):
        query_tile(sub)


def _diff_attn(rel_bias, qt, k, vt, lambdas, subln_g):
    b, _, s = qt.shape
    t, tps = ATTN_TILE, ATTN_TILES_PER_STEP
    nq = s // t
    assert s % (t * tps) == 0 and t % BIAS_TILE == 0
    assert FAR_PER_TRIP % NEXT_SPAN == 0 and QK_LOOKAHEAD < N_HEADS
    assert (_causal_bucket(np.arange(BIAS_TILE + 1, s)) == NUM_BUCKETS - 1).all()
    per_head = lambda shape, dtype: [pltpu.VMEM(shape, dtype) for _ in range(N_HEADS)]
    return pl.pallas_call(
        _attn_kernel,
        grid=(b, nq // tps),
        in_specs=[
            pl.BlockSpec(memory_space=pltpu.SMEM),
            pl.BlockSpec((1, ATTN_WIDTH, tps * t), lambda bi, qi: (bi, 0, qi)),
            pl.BlockSpec((1, QK_LOOKAHEAD * V_DIM, t),
                         lambda bi, qi: (bi, 0, jnp.minimum((qi + 1) * tps, nq - 1))),
            pl.BlockSpec((1, N_HEADS, s, V_DIM), lambda bi, qi: (bi, 0, 0, 0)),
            pl.BlockSpec((1, N_HEADS, nq, V_DIM, t), lambda bi, qi: (bi, 0, 0, 0, 0)),
            pl.BlockSpec((2, BIAS_TILE, BIAS_TILE), lambda bi, qi: (0, 0, 0)),
        ] + [pl.BlockSpec((1, HEAD_DIM), lambda bi, qi: (0, 0)) for _ in lambdas] + [
            pl.BlockSpec((1, V_DIM), lambda bi, qi: (0, 0)),
        ],
        out_specs=pl.BlockSpec((1, tps * t, ATTN_WIDTH), lambda bi, qi: (bi, qi, 0)),
        out_shape=jax.ShapeDtypeStruct((b, s, ATTN_WIDTH), BF16),
        scratch_shapes=(per_head((1, 2 * t), F32) + per_head((1, 2 * t), F32)
                        + per_head((1, 2 * t), F32)
                        + per_head((V_DIM, 2 * t), F32) + per_head((V_DIM, 2 * t), BF16)
                        + per_head((NEXT_SPAN * t, 2 * t), F32)
                        + [pltpu.VMEM((V_DIM, 2 * t), BF16) for _ in range(QK_LOOKAHEAD)]
                        + [pltpu.VMEM((N_HEADS, 2, BIAS_TILE, BIAS_TILE), F32)]),
        compiler_params=pltpu.CompilerParams(
            dimension_semantics=("arbitrary", "arbitrary"),
            vmem_limit_bytes=VMEM_LIMIT_BYTES),
        name="diff_attn",
    )(rel_bias, qt, qt, k, vt, jnp.asarray(_bucket_tiles(BIAS_TILE)), *lambdas, subln_g)


def _ffn_kernel(x_ref, yp_ref, ya_ref, wo_ref, gf_ref, w1_ref, cw_ref, cb_ref,
                w2_ref, gl_ref, out_ref, carry_ref, gbuf_ref, act_ref, hn_ref):
    si = pl.program_id(1)
    tm = x_ref.shape[1]
    ck = FFN_CHUNK
    rows = tm // FFN_ROW_BLOCKS

    @pl.when(si == 0)
    def _():
        carry_ref[...] = jnp.zeros(carry_ref.shape, F32)

    for r0 in range(0, tm, rows):
        y = (jnp.dot(yp_ref[0, r0:r0 + rows, :], wo_ref[0:POOL_WIDTH, :], preferred_element_type=F32)
             + jnp.dot(ya_ref[0, r0:r0 + rows, :], wo_ref[POOL_WIDTH:, :], preferred_element_type=F32))
        x1 = x_ref[0, r0:r0 + rows, :] + y
        out_ref[0, r0:r0 + rows, :] = x1
        ms = jnp.mean(x1 * x1, axis=-1, keepdims=True)
        hn_ref[r0:r0 + rows, :] = (x1 * lax.rsqrt(ms + NORM_EPS) * gf_ref[...]).astype(BF16)

    for c in range(D_FF // ck):
        lo, hi = c * ck, (c + 1) * ck
        g = jnp.dot(hn_ref[...], w1_ref[:, lo:hi], preferred_element_type=F32)
        u = jnp.dot(hn_ref[...], w1_ref[:, D_FF + lo:D_FF + hi], preferred_element_type=F32)
        gbuf_ref[0:CONV_HALO, :] = carry_ref[:, lo:hi]
        gbuf_ref[CONV_HALO:, :] = g
        g1 = gbuf_ref[pl.ds(CONV_HALO - 1, tm), :]
        g2 = gbuf_ref[pl.ds(CONV_HALO - 2, tm), :]
        carry_ref[:, lo:hi] = g[tm - CONV_HALO:, :]
        cv = (cw_ref[0:1, lo:hi] * g2 + cw_ref[1:2, lo:hi] * g1 + cw_ref[2:3, lo:hi] * g
              + cb_ref[:, lo:hi])
        a = cv * (1.0 / (1.0 + jnp.exp(-cv))) * u
        act_ref[:, lo:hi] = a.astype(BF16)

    for r0 in range(0, tm, rows):
        x2 = out_ref[0, r0:r0 + rows, :] + jnp.dot(act_ref[r0:r0 + rows, :], w2_ref[...],
                                                   preferred_element_type=F32)
        ms2 = jnp.mean(x2 * x2, axis=-1, keepdims=True)
        out_ref[0, r0:r0 + rows, :] = x2 * lax.rsqrt(ms2 + NORM_EPS) * gl_ref[...]


def _out_ffn(x, y_pool, y_attn, w_out, g_ffn, w1, conv_w, conv_b, w2, g_final):
    b, s, d = x.shape
    tm = FFN_ROWS
    assert s % tm == 0 and tm % FFN_ROW_BLOCKS == 0 and D_FF % FFN_CHUNK == 0
    const = lambda bi, si: (0, 0)
    resident = functools.partial(pl.BlockSpec, index_map=const, pipeline_mode=pl.Buffered(1))
    return pl.pallas_call(
        _ffn_kernel,
        grid=(b, s // tm),
        in_specs=[
            pl.BlockSpec((1, tm, d), lambda bi, si: (bi, si, 0)),
            pl.BlockSpec((1, tm, POOL_WIDTH), lambda bi, si: (bi, si, 0)),
            pl.BlockSpec((1, tm, ATTN_WIDTH), lambda bi, si: (bi, si, 0)),
            resident((d, d)),
            resident((1, d)),
            resident((d, 2 * D_FF)),
            resident((3, D_FF)),
            resident((1, D_FF)),
            resident((D_FF, d)),
            resident((1, d)),
        ],
        out_specs=pl.BlockSpec((1, tm, d), lambda bi, si: (bi, si, 0)),
        out_shape=jax.ShapeDtypeStruct((b, s, d), F32),
        scratch_shapes=[pltpu.VMEM((CONV_HALO, D_FF), F32),
                        pltpu.VMEM((CONV_HALO + tm, FFN_CHUNK), F32),
                        pltpu.VMEM((tm, D_FF), BF16),
                        pltpu.VMEM((tm, d), BF16)],
        compiler_params=pltpu.CompilerParams(
            dimension_semantics=("arbitrary", "arbitrary"),
            vmem_limit_bytes=VMEM_LIMIT_BYTES),
        name="out_ffn",
    )(x, y_pool, y_attn, w_out, g_ffn, w1, conv_w, conv_b, w2, g_final)


def kernel(x, norm_mix_g, w_in, pool_w, pool_scale, lambda_q1, lambda_k1, lambda_q2, lambda_k2,
           subln_g, rel_bias, w_out, norm_ffn_g, ffn_w_in, ffn_conv_w, ffn_conv_b, ffn_w_out,
           norm_final_g):
    y_pool, qt, k, vt, w_out16, w1_16, w2_16 = _in_proj(
        x, norm_mix_g, w_in[0], pool_w[0], pool_scale, (w_out[0], ffn_w_in[0], ffn_w_out[0]))
    y_attn = _diff_attn(rel_bias, qt, k, vt, (lambda_q1, lambda_k1, lambda_q2, lambda_k2), subln_g)
    return _out_ffn(x, y_pool, y_attn, w_out16, norm_ffn_g, w1_16, ffn_conv_w[0], ffn_conv_b,
                    w2_16, norm_final_g.reshape(1, D_MODEL))
```

```python
import functools
import math

import numpy as np
import jax
import jax.numpy as jnp
from jax import lax
from jax.experimental import pallas as pl
from jax.experimental.pallas import tpu as pltpu

D_MODEL = 1024
POOL_WIDTH = 512
POOL_GROUPS = 4
POOL_GROUP_DIM = 128
POOL_WINDOWS = (2, 4, 8, 16)
POOL_HALO = 16
ATTN_WIDTH = 512
N_HEADS = 4
HEAD_DIM = 64
V_DIM = 128
IN_PROJ_WIDTH = 2048
NUM_BUCKETS = 32
MAX_DISTANCE = 128
D_FF = 2816
NORM_EPS = 1e-6
SUBLN_EPS = 1e-5
NEG_INF = -1e30
LAMBDA_INIT = 0.8 - 0.6 * math.exp(-0.3 * 0)
LOG2E = math.log2(math.e)

ATTN_TILE = 256
BIAS_TILE = 128
QK_LOOKAHEAD = 2
FAR_PER_TRIP = 4
NEXT_SPAN = 2
ATTN_TILES_PER_STEP = 2
IN_ROWS = 1024
IN_ROW_BLOCKS = 2
FFN_ROWS = 1024
FFN_CHUNK = 256
FFN_ROW_BLOCKS = 4
CONV_HALO = 8
WEIGHT_CAST_ROWS = 128
N_LATER_WEIGHTS = 3
VMEM_LIMIT_BYTES = 56 * 1024 * 1024
BF16_SUBLANES = 16

F32 = jnp.float32
BF16 = jnp.bfloat16


def _causal_bucket(n):
    max_exact = NUM_BUCKETS // 2
    nf = np.maximum(n, 1).astype(np.float64)
    large = max_exact + (np.log(nf / max_exact) / math.log(MAX_DISTANCE / max_exact)
                         * (NUM_BUCKETS - max_exact)).astype(np.int64)
    return np.where(n < max_exact, n, np.minimum(large, NUM_BUCKETS - 1))


def _bucket_tiles(tile):
    k = np.arange(tile)[:, None]
    q = np.arange(tile)[None, :]
    out = []
    for offset in (0, tile):
        n = q - k + offset
        out.append(np.where(n < 0, -1, _causal_bucket(n)))
    return np.stack(out).astype(np.int32)


def _in_proj_kernel(x_ref, g_ref, w32_ref, pw_ref, ps_ref, *rest):
    later32_refs = rest[:N_LATER_WEIGHTS]
    ypool_ref, qt_ref, k_ref, vt_ref = rest[N_LATER_WEIGHTS:N_LATER_WEIGHTS + 4]
    later16_refs = rest[N_LATER_WEIGHTS + 4:2 * N_LATER_WEIGHTS + 4]
    ext_ref, w_ref, hn_ref = rest[2 * N_LATER_WEIGHTS + 4:]
    si = pl.program_id(1)
    tm = x_ref.shape[1]

    @pl.when((pl.program_id(0) == 0) & (si == 0))
    def _():
        for r in range(0, D_MODEL, WEIGHT_CAST_ROWS):
            w_ref[r:r + WEIGHT_CAST_ROWS, :] = w32_ref[r:r + WEIGHT_CAST_ROWS, :].astype(BF16)

    @pl.when(si == 0)
    def _():
        ext_ref[0:POOL_HALO, :] = jnp.zeros((POOL_HALO, POOL_WIDTH), F32)

    rows = tm // IN_ROW_BLOCKS
    zp_blocks = []
    for r0 in range(0, tm, rows):
        x = x_ref[0, r0:r0 + rows, :]
        ms = jnp.mean(x * x, axis=-1, keepdims=True)
        hn = (x * lax.rsqrt(ms + NORM_EPS) * g_ref[...]).astype(BF16)
        hn_ref[r0:r0 + rows, :] = hn
        zp_blocks.append(jnp.dot(hn, w_ref[:, :POOL_WIDTH], preferred_element_type=F32))
    zp = jnp.concatenate(zp_blocks, axis=0)
    z = jnp.dot(hn_ref[...], w_ref[:, POOL_WIDTH:], preferred_element_type=F32)

    for src_ref, dst_ref in zip(later32_refs, later16_refs):
        dst_ref[...] = src_ref[...].astype(BF16)

    ext_ref[POOL_HALO:, :] = zp
    tpos = si * tm + lax.broadcasted_iota(jnp.int32, (tm, 1), 0) + 1
    for gi, w in enumerate(POOL_WINDOWS):
        lo, hi = gi * POOL_GROUP_DIM, (gi + 1) * POOL_GROUP_DIM
        tot = ext_ref[:, lo:hi]
        step = 1
        while step < w:
            tot = tot + pltpu.roll(tot, step, axis=0)
            step *= 2
        tot = tot[POOL_HALO:, :]
        cnt = jnp.minimum(tpos, w).astype(F32)
        pooled = tot / cnt - zp[:, lo:hi]
        yg = jnp.dot(pooled.astype(BF16), pw_ref[gi].astype(BF16), preferred_element_type=F32)
        ypool_ref[0, :, lo:hi] = (yg * ps_ref[:, lo:hi]).astype(BF16)
    ext_ref[0:POOL_HALO, :] = zp[tm - POOL_HALO:, :]

    t = ATTN_TILE
    q0, k0, v0 = 0, ATTN_WIDTH, 2 * ATTN_WIDTH
    scale = LOG2E * HEAD_DIM ** -0.5
    for h in range(N_HEADS):
        zq = z[:, q0 + h * V_DIM:q0 + (h + 1) * V_DIM] * scale
        qt_ref[0, h * V_DIM:(h + 1) * V_DIM, :] = zq.T.astype(BF16)
        k_ref[0, h] = z[:, k0 + h * V_DIM:k0 + (h + 1) * V_DIM].astype(BF16)
        vt = z[:, v0 + h * V_DIM:v0 + (h + 1) * V_DIM].T.astype(BF16)
        for j in range(tm // t):
            vt_ref[0, h, j] = vt[:, j * t:(j + 1) * t]


def _in_proj(x, g, w_in, pool_w, pool_scale, later_weights):
    b, s, d = x.shape
    tm, t = IN_ROWS, ATTN_TILE
    steps = s // tm
    assert s % tm == 0 and tm % t == 0 and tm % IN_ROW_BLOCKS == 0
    assert len(later_weights) == N_LATER_WEIGHTS
    assert all(w.shape[0] % (b * steps * BF16_SUBLANES) == 0 for w in later_weights)
    slabs = [(w.shape[0] // (b * steps), w.shape[1]) for w in later_weights]
    slab_specs = [pl.BlockSpec(sl, lambda bi, si: (bi * steps + si, 0)) for sl in slabs]
    return pl.pallas_call(
        _in_proj_kernel,
        grid=(b, s // tm),
        in_specs=[
            pl.BlockSpec((1, tm, d), lambda bi, si: (bi, si, 0)),
            pl.BlockSpec((1, d), lambda bi, si: (0, 0)),
            pl.BlockSpec((d, IN_PROJ_WIDTH), lambda bi, si: (0, 0), pipeline_mode=pl.Buffered(1)),
            pl.BlockSpec((POOL_GROUPS, POOL_GROUP_DIM, POOL_GROUP_DIM), lambda bi, si: (0, 0, 0)),
            pl.BlockSpec((1, POOL_WIDTH), lambda bi, si: (0, 0)),
        ] + slab_specs,
        out_specs=[
            pl.BlockSpec((1, tm, POOL_WIDTH), lambda bi, si: (bi, si, 0)),
            pl.BlockSpec((1, ATTN_WIDTH, tm), lambda bi, si: (bi, 0, si)),
            pl.BlockSpec((1, N_HEADS, tm, V_DIM), lambda bi, si: (bi, 0, si, 0)),
            pl.BlockSpec((1, N_HEADS, tm // t, V_DIM, t), lambda bi, si: (bi, 0, si, 0, 0)),
        ] + slab_specs,
        out_shape=[
            jax.ShapeDtypeStruct((b, s, POOL_WIDTH), BF16),
            jax.ShapeDtypeStruct((b, ATTN_WIDTH, s), BF16),
            jax.ShapeDtypeStruct((b, N_HEADS, s, V_DIM), BF16),
            jax.ShapeDtypeStruct((b, N_HEADS, s // t, V_DIM, t), BF16),
        ] + [jax.ShapeDtypeStruct(w.shape, BF16) for w in later_weights],
        scratch_shapes=[pltpu.VMEM((POOL_HALO + tm, POOL_WIDTH), F32),
                        pltpu.VMEM((d, IN_PROJ_WIDTH), BF16),
                        pltpu.VMEM((tm, d), BF16)],
        compiler_params=pltpu.CompilerParams(
            dimension_semantics=("arbitrary", "arbitrary"),
            vmem_limit_bytes=VMEM_LIMIT_BYTES),
        name="in_proj",
    )(x, g, w_in, pool_w, pool_scale, *later_weights)


def _attn_kernel(rb_ref, qt_ref, qtn_ref, k_ref, vt_ref, bkt_ref, lq1_ref, lk1_ref, lq2_ref,
                 lk2_ref, sg_ref, out_ref, *scratch):
    t = ATTN_TILE
    m_refs, l_refs, smax_refs, acc_refs, qs_refs, s_refs = (
        scratch[i * N_HEADS:(i + 1) * N_HEADS] for i in range(6))
    qsn_refs = scratch[6 * N_HEADS:6 * N_HEADS + QK_LOOKAHEAD]
    bias_ref = scratch[6 * N_HEADS + QK_LOOKAHEAD]

    @pl.when((pl.program_id(0) == 0) & (pl.program_id(1) == 0))
    def _():
        bkt = bkt_ref[...]
        acc = [jnp.full(bkt.shape, NEG_INF, F32) for _ in range(N_HEADS)]
        for i in range(NUM_BUCKETS):
            hit = bkt == i
            acc = [jnp.where(hit, rb_ref[i, h] * LOG2E, acc[h]) for h in range(N_HEADS)]
        for h in range(N_HEADS):
            bias_ref[h] = acc[h]

    def load_q(h, src, dst_refs):
        src_ref, col0 = src
        qt = src_ref[0, h * V_DIM:(h + 1) * V_DIM, col0:col0 + t]
        row = lax.broadcasted_iota(jnp.int32, qt.shape, 0)
        zero = jnp.zeros_like(qt)
        dst_refs[h][:, 0:t] = jnp.where(row < HEAD_DIM, qt, zero)
        dst_refs[h][:, t:2 * t] = jnp.where(row >= HEAD_DIM, qt, zero)

    far_bias = [rb_ref[NUM_BUCKETS - 1, h] * LOG2E for h in range(N_HEADS)]

    lam = (jnp.exp(jnp.sum(lq1_ref[...] * lk1_ref[...], axis=1, keepdims=True))
           - jnp.exp(jnp.sum(lq2_ref[...] * lk2_ref[...], axis=1, keepdims=True))
           + LAMBDA_INIT)

    def qk(h, j, kinds, add_bias=True, q_refs=qs_refs):
        n = len(kinds) * t
        kj = k_ref[0, h, pl.ds(pl.multiple_of(j * t, t), n), :]
        s = jnp.dot(kj, q_refs[h][...], preferred_element_type=F32)
        if add_bias and has_bias(kinds):
            s = with_bias(s, h, kinds)
        s_refs[h][0:n, :] = s
        smax_refs[h][...] = jnp.max(s, axis=0, keepdims=True)

    def has_bias(kinds):
        return any(kd is not None for kd in kinds)

    def with_bias(s, h, kinds):
        b = jnp.concatenate([tile_bias(h, kd) for kd in kinds], axis=0)
        return s + jnp.concatenate([b, b], axis=1)

    def tile_bias(h, diag):
        n = t // BIAS_TILE
        pieces = {0: bias_ref[h, 0], 1: bias_ref[h, 1]}
        far = jnp.full((BIAS_TILE, BIAS_TILE), far_bias[h], F32)
        if diag is None:
            return jnp.full((t, t), far_bias[h], F32)
        masked = jnp.full((BIAS_TILE, BIAS_TILE), NEG_INF, F32)
        rows = []
        for kb in range(n):
            d = [qb - kb + (0 if diag else n) for qb in range(n)]
            rows.append(jnp.concatenate([masked if x < 0 else pieces.get(x, far) for x in d], axis=1))
        return jnp.concatenate(rows, axis=0)

    def softmax_pv(h, j, kinds, bias_pending):
        n = len(kinds) * t
        off = 0.0 if has_bias(kinds) else far_bias[h]
        s = s_refs[h][0:n, :]
        if bias_pending and has_bias(kinds):
            s = with_bias(s, h, kinds)
            smax = jnp.max(s, axis=0, keepdims=True)
        else:
            smax = smax_refs[h][...]
        m_old = m_refs[h][...]
        m_new = jnp.maximum(m_old, smax + off)
        alpha = jnp.exp2(m_old - m_new)
        p = jnp.exp2(s - (m_new - off))
        l_refs[h][...] = alpha * l_refs[h][...] + jnp.sum(p, axis=0, keepdims=True)
        m_refs[h][...] = m_new
        vt = jnp.concatenate([vt_ref[0, h, j + d] for d in range(len(kinds))], axis=1)
        r = jnp.dot(vt, p.astype(BF16), preferred_element_type=F32)
        acc_refs[h][...] = alpha * acc_refs[h][...] + r

    def run(groups, next_j, sub):
        stages = [(h, j, kinds) for (j, kinds) in groups for h in range(N_HEADS)]
        for i, (h, j, kinds) in enumerate(stages):
            ahead = i + QK_LOOKAHEAD
            if ahead < len(stages):
                qk(*stages[ahead])
            elif next_j is not None:
                qk(ahead - len(stages), next_j, (None,) * NEXT_SPAN, add_bias=False)
            else:
                nxt = (qt_ref, (sub + 1) * t) if sub + 1 < ATTN_TILES_PER_STEP else (qtn_ref, 0)
                load_q(ahead - len(stages), nxt, qsn_refs)
                qk(ahead - len(stages), 0, (None,) * NEXT_SPAN, add_bias=False, q_refs=qsn_refs)
            softmax_pv(h, j, kinds, bias_pending=i < QK_LOOKAHEAD)
            if kinds[-1] is True:
                finalize(h, sub)

    def finalize(h, sub):
        o = acc_refs[h][...] * (1.0 / l_refs[h][...])
        o = o[:, :t] - lam * o[:, t:]
        ms = jnp.mean(o * o, axis=0, keepdims=True)
        y = (o * lax.rsqrt(ms + SUBLN_EPS)).T * (sg_ref[...] * (1.0 - LAMBDA_INIT))
        out_ref[0, sub * t:(sub + 1) * t, h * V_DIM:(h + 1) * V_DIM] = y.astype(BF16)

    def query_tile(sub):
        qi = pl.program_id(1) * ATTN_TILES_PER_STEP + sub

        if sub == 0:
            @pl.when(qi == 0)
            def _():
                for h in range(QK_LOOKAHEAD):
                    load_q(h, (qt_ref, 0), qsn_refs)
                    qk(h, 0, (True,), add_bias=False, q_refs=qsn_refs)

        for h in range(QK_LOOKAHEAD):
            qs_refs[h][...] = qsn_refs[h][...]
        for h in range(QK_LOOKAHEAD, N_HEADS):
            load_q(h, (qt_ref, sub * t), qs_refs)
        for h in range(N_HEADS):
            m_refs[h][...] = jnp.full(m_refs[h].shape, -jnp.inf, F32)
            l_refs[h][...] = jnp.zeros(l_refs[h].shape, F32)
            acc_refs[h][...] = jnp.zeros(acc_refs[h].shape, F32)

        n_far = qi - 1

        def far_body(i, carry):
            j = FAR_PER_TRIP * i
            run([(j + d, (None, None)) for d in range(0, FAR_PER_TRIP, 2)], j + FAR_PER_TRIP, sub)
            return carry

        lax.fori_loop(0, jnp.maximum(n_far, 0) // FAR_PER_TRIP, far_body, 0)

        if sub == 0:
            @pl.when(qi == 0)
            def _():
                run([(qi, (True,))], None, sub)

        nq = k_ref.shape[2] // t
        for rem in range(FAR_PER_TRIP):
            if not any((q - 1) % FAR_PER_TRIP == rem
                       for q in range(1, nq) if q % ATTN_TILES_PER_STEP == sub):
                continue

            @pl.when((qi >= 1) & (n_far % FAR_PER_TRIP == rem))
            def _():
                kinds = (None,) * rem + (False, True)
                first = qi - 1 - rem
                run([(first + d, kinds[d:d + 2]) for d in range(0, len(kinds), 2)], None, sub)

    for sub in range(ATTN_TILES_PER_STEP):
        query_tile(sub)


def _diff_attn(rel_bias, qt, k, vt, lambdas, subln_g):
    b, _, s = qt.shape
    t, tps = ATTN_TILE, ATTN_TILES_PER_STEP
    nq = s // t
    assert s % (t * tps) == 0 and t % BIAS_TILE == 0
    assert FAR_PER_TRIP % NEXT_SPAN == 0 and QK_LOOKAHEAD < N_HEADS
    assert (_causal_bucket(np.arange(BIAS_TILE + 1, s)) == NUM_BUCKETS - 1).all()
    per_head = lambda shape, dtype: [pltpu.VMEM(shape, dtype) for _ in range(N_HEADS)]
    return pl.pallas_call(
        _attn_kernel,
        grid=(b, nq // tps),
        in_specs=[
            pl.BlockSpec(memory_space=pltpu.SMEM),
            pl.BlockSpec((1, ATTN_WIDTH, tps * t), lambda bi, qi: (bi, 0, qi)),
            pl.BlockSpec((1, QK_LOOKAHEAD * V_DIM, t),
                         lambda bi, qi: (bi, 0, jnp.minimum((qi + 1) * tps, nq - 1))),
            pl.BlockSpec((1, N_HEADS, s, V_DIM), lambda bi, qi: (bi, 0, 0, 0)),
            pl.BlockSpec((1, N_HEADS, nq, V_DIM, t), lambda bi, qi: (bi, 0, 0, 0, 0)),
            pl.BlockSpec((2, BIAS_TILE, BIAS_TILE), lambda bi, qi: (0, 0, 0)),
        ] + [pl.BlockSpec((1, HEAD_DIM), lambda bi, qi: (0, 0)) for _ in lambdas] + [
            pl.BlockSpec((1, V_DIM), lambda bi, qi: (0, 0)),
        ],
        out_specs=pl.BlockSpec((1, tps * t, ATTN_WIDTH), lambda bi, qi: (bi, qi, 0)),
        out_shape=jax.ShapeDtypeStruct((b, s, ATTN_WIDTH), BF16),
        scratch_shapes=(per_head((1, 2 * t), F32) + per_head((1, 2 * t), F32)
                        + per_head((1, 2 * t), F32)
                        + per_head((V_DIM, 2 * t), F32) + per_head((V_DIM, 2 * t), BF16)
                        + per_head((NEXT_SPAN * t, 2 * t), F32)
                        + [pltpu.VMEM((V_DIM, 2 * t), BF16) for _ in range(QK_LOOKAHEAD)]
                        + [pltpu.VMEM((N_HEADS, 2, BIAS_TILE, BIAS_TILE), F32)]),
        compiler_params=pltpu.CompilerParams(
            dimension_semantics=("arbitrary", "arbitrary"),
            vmem_limit_bytes=VMEM_LIMIT_BYTES),
        name="diff_attn",
    )(rel_bias, qt, qt, k, vt, jnp.asarray(_bucket_tiles(BIAS_TILE)), *lambdas, subln_g)


def _ffn_kernel(x_ref, yp_ref, ya_ref, wo_ref, gf_ref, w1_ref, cw_ref, cb_ref,
                w2_ref, gl_ref, out_ref, carry_ref, gbuf_ref, act_ref, hn_ref):
    si = pl.program_id(1)
    tm = x_ref.shape[1]
    ck = FFN_CHUNK
    rows = tm // FFN_ROW_BLOCKS

    @pl.when(si == 0)
    def _():
        carry_ref[...] = jnp.zeros(carry_ref.shape, F32)

    for r0 in range(0, tm, rows):
        y = (jnp.dot(yp_ref[0, r0:r0 + rows, :], wo_ref[0:POOL_WIDTH, :], preferred_element_type=F32)
             + jnp.dot(ya_ref[0, r0:r0 + rows, :], wo_ref[POOL_WIDTH:, :], preferred_element_type=F32))
        x1 = x_ref[0, r0:r0 + rows, :] + y
        out_ref[0, r0:r0 + rows, :] = x1
        ms = jnp.mean(x1 * x1, axis=-1, keepdims=True)
        hn_ref[r0:r0 + rows, :] = (x1 * lax.rsqrt(ms + NORM_EPS) * gf_ref[...]).astype(BF16)

    for c in range(D_FF // ck):
        lo, hi = c * ck, (c + 1) * ck
        g = jnp.dot(hn_ref[...], w1_ref[:, lo:hi], preferred_element_type=F32)
        u = jnp.dot(hn_ref[...], w1_ref[:, D_FF + lo:D_FF + hi], preferred_element_type=F32)
        gbuf_ref[0:CONV_HALO, :] = carry_ref[:, lo:hi]
        gbuf_ref[CONV_HALO:, :] = g
        g1 = gbuf_ref[pl.ds(CONV_HALO - 1, tm), :]
        g2 = gbuf_ref[pl.ds(CONV_HALO - 2, tm), :]
        carry_ref[:, lo:hi] = g[tm - CONV_HALO:, :]
        cv = (cw_ref[0:1, lo:hi] * g2 + cw_ref[1:2, lo:hi] * g1 + cw_ref[2:3, lo:hi] * g
              + cb_ref[:, lo:hi])
        hc = 0.5 * cv
        a = (hc + hc * jnp.tanh(hc)) * u
        act_ref[:, lo:hi] = a.astype(BF16)

    for r0 in range(0, tm, rows):
        x2 = out_ref[0, r0:r0 + rows, :] + jnp.dot(act_ref[r0:r0 + rows, :], w2_ref[...],
                                                   preferred_element_type=F32)
        ms2 = jnp.mean(x2 * x2, axis=-1, keepdims=True)
        out_ref[0, r0:r0 + rows, :] = x2 * lax.rsqrt(ms2 + NORM_EPS) * gl_ref[...]


def _out_ffn(x, y_pool, y_attn, w_out, g_ffn, w1, conv_w, conv_b, w2, g_final):
    b, s, d = x.shape
    tm = FFN_ROWS
    assert s % tm == 0 and tm % FFN_ROW_BLOCKS == 0 and D_FF % FFN_CHUNK == 0
    const = lambda bi, si: (0, 0)
    resident = functools.partial(pl.BlockSpec, index_map=const, pipeline_mode=pl.Buffered(1))
    return pl.pallas_call(
        _ffn_kernel,
        grid=(b, s // tm),
        in_specs=[
            pl.BlockSpec((1, tm, d), lambda bi, si: (bi, si, 0)),
            pl.BlockSpec((1, tm, POOL_WIDTH), lambda bi, si: (bi, si, 0)),
            pl.BlockSpec((1, tm, ATTN_WIDTH), lambda bi, si: (bi, si, 0)),
            resident((d, d)),
            resident((1, d)),
            resident((d, 2 * D_FF)),
            resident((3, D_FF)),
            resident((1, D_FF)),
            resident((D_FF, d)),
            resident((1, d)),
        ],
        out_specs=pl.BlockSpec((1, tm, d), lambda bi, si: (bi, si, 0)),
        out_shape=jax.ShapeDtypeStruct((b, s, d), F32),
        scratch_shapes=[pltpu.VMEM((CONV_HALO, D_FF), F32),
                        pltpu.VMEM((CONV_HALO + tm, FFN_CHUNK), F32),
                        pltpu.VMEM((tm, D_FF), BF16),
                        pltpu.VMEM((tm, d), BF16)],
        compiler_params=pltpu.CompilerParams(
            dimension_semantics=("arbitrary", "arbitrary"),
            vmem_limit_bytes=VMEM_LIMIT_BYTES),
        name="out_ffn",
    )(x, y_pool, y_attn, w_out, g_ffn, w1, conv_w, conv_b, w2, g_final)


def kernel(x, norm_mix_g, w_in, pool_w, pool_scale, lambda_q1, lambda_k1, lambda_q2, lambda_k2,
           subln_g, rel_bias, w_out, norm_ffn_g, ffn_w_in, ffn_conv_w, ffn_conv_b, ffn_w_out,
           norm_final_g):
    y_pool, qt, k, vt, w_out16, w1_16, w2_16 = _in_proj(
        x, norm_mix_g, w_in[0], pool_w[0], pool_scale, (w_out[0], ffn_w_in[0], ffn_w_out[0]))
    y_attn = _diff_attn(rel_bias, qt, k, vt, (lambda_q1, lambda_k1, lambda_q2, lambda_k2), subln_g)
    return _out_ffn(x, y_pool, y_attn, w_out16, norm_ffn_g, w1_16, ffn_conv_w[0], ffn_conv_b,
                    w2_16, norm_final_g.reshape(1, D_MODEL))
```

```python
import functools
import math

import numpy as np
import jax
import jax.numpy as jnp
from jax import lax
from jax.experimental import pallas as pl
from jax.experimental.pallas import tpu as pltpu

D_MODEL = 1024
POOL_WIDTH = 512
POOL_GROUPS = 4
POOL_GROUP_DIM = 128
POOL_WINDOWS = (2, 4, 8, 16)
POOL_HALO = 16
ATTN_WIDTH = 512
N_HEADS = 4
HEAD_DIM = 64
V_DIM = 128
IN_PROJ_WIDTH = 2048
NUM_BUCKETS = 32
MAX_DISTANCE = 128
D_FF = 2816
NORM_EPS = 1e-6
SUBLN_EPS = 1e-5
NEG_INF = -1e30
LAMBDA_INIT = 0.8 - 0.6 * math.exp(-0.3 * 0)
LOG2E = math.log2(math.e)

ATTN_TILE = 256
BIAS_TILE = 128
QK_LOOKAHEAD = 2
FAR_PER_TRIP = 4
NEXT_SPAN = 2
ATTN_TILES_PER_STEP = 2
IN_ROWS = 1024
IN_ROW_BLOCKS = 2
FFN_ROWS = 1024
FFN_CHUNK = 256
FFN_ROW_BLOCKS = 4
CONV_HALO = 8
WEIGHT_CAST_ROWS = 128
N_LATER_WEIGHTS = 3
VMEM_LIMIT_BYTES = 56 * 1024 * 1024
BF16_SUBLANES = 16

F32 = jnp.float32
BF16 = jnp.bfloat16


def _causal_bucket(n):
    max_exact = NUM_BUCKETS // 2
    nf = np.maximum(n, 1).astype(np.float64)
    large = max_exact + (np.log(nf / max_exact) / math.log(MAX_DISTANCE / max_exact)
                         * (NUM_BUCKETS - max_exact)).astype(np.int64)
    return np.where(n < max_exact, n, np.minimum(large, NUM_BUCKETS - 1))


def _bucket_tiles(tile):
    k = np.arange(tile)[:, None]
    q = np.arange(tile)[None, :]
    out = []
    for offset in (0, tile):
        n = q - k + offset
        out.append(np.where(n < 0, -1, _causal_bucket(n)))
    return np.stack(out).astype(np.int32)


def _in_proj_kernel(x_ref, g_ref, w32_ref, pw_ref, ps_ref, *rest):
    later32_refs = rest[:N_LATER_WEIGHTS]
    ypool_ref, qt_ref, k_ref, vt_ref = rest[N_LATER_WEIGHTS:N_LATER_WEIGHTS + 4]
    later16_refs = rest[N_LATER_WEIGHTS + 4:2 * N_LATER_WEIGHTS + 4]
    ext_ref, w_ref, hn_ref = rest[2 * N_LATER_WEIGHTS + 4:]
    si = pl.program_id(1)
    tm = x_ref.shape[1]

    @pl.when((pl.program_id(0) == 0) & (si == 0))
    def _():
        for r in range(0, D_MODEL, WEIGHT_CAST_ROWS):
            w_ref[r:r + WEIGHT_CAST_ROWS, :] = w32_ref[r:r + WEIGHT_CAST_ROWS, :].astype(BF16)

    @pl.when(si == 0)
    def _():
        ext_ref[0:POOL_HALO, :] = jnp.zeros((POOL_HALO, POOL_WIDTH), F32)

    rows = tm // IN_ROW_BLOCKS
    zp_blocks = []
    for r0 in range(0, tm, rows):
        x = x_ref[0, r0:r0 + rows, :]
        ms = jnp.mean(x * x, axis=-1, keepdims=True)
        hn = (x * lax.rsqrt(ms + NORM_EPS) * g_ref[...]).astype(BF16)
        hn_ref[r0:r0 + rows, :] = hn
        zp_blocks.append(jnp.dot(hn, w_ref[:, :POOL_WIDTH], preferred_element_type=F32))
    zp = jnp.concatenate(zp_blocks, axis=0)
    z = jnp.dot(hn_ref[...], w_ref[:, POOL_WIDTH:], preferred_element_type=F32)

    for src_ref, dst_ref in zip(later32_refs, later16_refs):
        dst_ref[...] = src_ref[...].astype(BF16)

    ext_ref[POOL_HALO:, :] = zp
    tpos = si * tm + lax.broadcasted_iota(jnp.int32, (tm, 1), 0) + 1
    for gi, w in enumerate(POOL_WINDOWS):
        lo, hi = gi * POOL_GROUP_DIM, (gi + 1) * POOL_GROUP_DIM
        tot = ext_ref[:, lo:hi]
        step = 1
        while step < w:
            tot = tot + pltpu.roll(tot, step, axis=0)
            step *= 2
        tot = tot[POOL_HALO:, :]
        cnt = jnp.minimum(tpos, w).astype(F32)
        pooled = tot / cnt - zp[:, lo:hi]
        yg = jnp.dot(pooled.astype(BF16), pw_ref[gi].astype(BF16), preferred_element_type=F32)
        ypool_ref[0, :, lo:hi] = (yg * ps_ref[:, lo:hi]).astype(BF16)
    ext_ref[0:POOL_HALO, :] = zp[tm - POOL_HALO:, :]

    t = ATTN_TILE
    q0, k0, v0 = 0, ATTN_WIDTH, 2 * ATTN_WIDTH
    scale = LOG2E * HEAD_DIM ** -0.5
    for h in range(N_HEADS):
        zq = z[:, q0 + h * V_DIM:q0 + (h + 1) * V_DIM] * scale
        qt_ref[0, h * V_DIM:(h + 1) * V_DIM, :] = zq.T.astype(BF16)
        k_ref[0, h] = z[:, k0 + h * V_DIM:k0 + (h + 1) * V_DIM].astype(BF16)
        vt = z[:, v0 + h * V_DIM:v0 + (h + 1) * V_DIM].T.astype(BF16)
        for j in range(tm // t):
            vt_ref[0, h, j] = vt[:, j * t:(j + 1) * t]


def _in_proj(x, g, w_in, pool_w, pool_scale, later_weights):
    b, s, d = x.shape
    tm, t = IN_ROWS, ATTN_TILE
    steps = s // tm
    assert s % tm == 0 and tm % t == 0 and tm % IN_ROW_BLOCKS == 0
    assert len(later_weights) == N_LATER_WEIGHTS
    assert all(w.shape[0] % (b * steps * BF16_SUBLANES) == 0 for w in later_weights)
    slabs = [(w.shape[0] // (b * steps), w.shape[1]) for w in later_weights]
    slab_specs = [pl.BlockSpec(sl, lambda bi, si: (bi * steps + si, 0)) for sl in slabs]
    return pl.pallas_call(
        _in_proj_kernel,
        grid=(b, s // tm),
        in_specs=[
            pl.BlockSpec((1, tm, d), lambda bi, si: (bi, si, 0)),
            pl.BlockSpec((1, d), lambda bi, si: (0, 0)),
            pl.BlockSpec((d, IN_PROJ_WIDTH), lambda bi, si: (0, 0), pipeline_mode=pl.Buffered(1)),
            pl.BlockSpec((POOL_GROUPS, POOL_GROUP_DIM, POOL_GROUP_DIM), lambda bi, si: (0, 0, 0)),
            pl.BlockSpec((1, POOL_WIDTH), lambda bi, si: (0, 0)),
        ] + slab_specs,
        out_specs=[
            pl.BlockSpec((1, tm, POOL_WIDTH), lambda bi, si: (bi, si, 0)),
            pl.BlockSpec((1, ATTN_WIDTH, tm), lambda bi, si: (bi, 0, si)),
            pl.BlockSpec((1, N_HEADS, tm, V_DIM), lambda bi, si: (bi, 0, si, 0)),
            pl.BlockSpec((1, N_HEADS, tm // t, V_DIM, t), lambda bi, si: (bi, 0, si, 0, 0)),
        ] + slab_specs,
        out_shape=[
            jax.ShapeDtypeStruct((b, s, POOL_WIDTH), BF16),
            jax.ShapeDtypeStruct((b, ATTN_WIDTH, s), BF16),
            jax.ShapeDtypeStruct((b, N_HEADS, s, V_DIM), BF16),
            jax.ShapeDtypeStruct((b, N_HEADS, s // t, V_DIM, t), BF16),
        ] + [jax.ShapeDtypeStruct(w.shape, BF16) for w in later_weights],
        scratch_shapes=[pltpu.VMEM((POOL_HALO + tm, POOL_WIDTH), F32),
                        pltpu.VMEM((d, IN_PROJ_WIDTH), BF16),
                        pltpu.VMEM((tm, d), BF16)],
        compiler_params=pltpu.CompilerParams(
            dimension_semantics=("arbitrary", "arbitrary"),
            vmem_limit_bytes=VMEM_LIMIT_BYTES),
        name="in_proj",
    )(x, g, w_in, pool_w, pool_scale, *later_weights)


def _attn_kernel(rb_ref, qt_ref, qtn_ref, k_ref, vt_ref, bkt_ref, lq1_ref, lk1_ref, lq2_ref,
                 lk2_ref, sg_ref, out_ref, *scratch):
    t = ATTN_TILE
    m_refs, l_refs, smax_refs, acc_refs, qs_refs, s_refs = (
        scratch[i * N_HEADS:(i + 1) * N_HEADS] for i in range(6))
    qsn_refs = scratch[6 * N_HEADS:6 * N_HEADS + QK_LOOKAHEAD]
    bias_ref = scratch[6 * N_HEADS + QK_LOOKAHEAD]

    @pl.when((pl.program_id(0) == 0) & (pl.program_id(1) == 0))
    def _():
        bkt = bkt_ref[...]
        acc = [jnp.full(bkt.shape, NEG_INF, F32) for _ in range(N_HEADS)]
        for i in range(NUM_BUCKETS):
            hit = bkt == i
            acc = [jnp.where(hit, rb_ref[i, h] * LOG2E, acc[h]) for h in range(N_HEADS)]
        for h in range(N_HEADS):
            bias_ref[h] = acc[h]

    def load_q(h, src, dst_refs):
        src_ref, col0 = src
        qt = src_ref[0, h * V_DIM:(h + 1) * V_DIM, col0:col0 + t]
        row = lax.broadcasted_iota(jnp.int32, qt.shape, 0)
        zero = jnp.zeros_like(qt)
        dst_refs[h][:, 0:t] = jnp.where(row < HEAD_DIM, qt, zero)
        dst_refs[h][:, t:2 * t] = jnp.where(row >= HEAD_DIM, qt, zero)

    far_bias = [rb_ref[NUM_BUCKETS - 1, h] * LOG2E for h in range(N_HEADS)]

    lam = (jnp.exp(jnp.sum(lq1_ref[...] * lk1_ref[...], axis=1, keepdims=True))
           - jnp.exp(jnp.sum(lq2_ref[...] * lk2_ref[...], axis=1, keepdims=True))
           + LAMBDA_INIT)

    def qk(h, j, kinds, add_bias=True, q_refs=qs_refs):
        n = len(kinds) * t
        kj = k_ref[0, h, pl.ds(pl.multiple_of(j * t, t), n), :]
        s = jnp.dot(kj, q_refs[h][...], preferred_element_type=F32)
        if add_bias and has_bias(kinds):
            s = with_bias(s, h, kinds)
        s_refs[h][0:n, :] = s
        smax_refs[h][...] = jnp.max(s, axis=0, keepdims=True)

    def has_bias(kinds):
        return any(kd is not None for kd in kinds)

    def with_bias(s, h, kinds):
        b = jnp.concatenate([tile_bias(h, kd) for kd in kinds], axis=0)
        return s + jnp.concatenate([b, b], axis=1)

    def tile_bias(h, diag):
        n = t // BIAS_TILE
        pieces = {0: bias_ref[h, 0], 1: bias_ref[h, 1]}
        far = jnp.full((BIAS_TILE, BIAS_TILE), far_bias[h], F32)
        if diag is None:
            return jnp.full((t, t), far_bias[h], F32)
        masked = jnp.full((BIAS_TILE, BIAS_TILE), NEG_INF, F32)
        rows = []
        for kb in range(n):
            d = [qb - kb + (0 if diag else n) for qb in range(n)]
            rows.append(jnp.concatenate([masked if x < 0 else pieces.get(x, far) for x in d], axis=1))
        return jnp.concatenate(rows, axis=0)

    def softmax_pv(h, j, kinds, bias_pending):
        n = len(kinds) * t
        off = 0.0 if has_bias(kinds) else far_bias[h]
        s = s_refs[h][0:n, :]
        if bias_pending and has_bias(kinds):
            s = with_bias(s, h, kinds)
            smax = jnp.max(s, axis=0, keepdims=True)
        else:
            smax = smax_refs[h][...]
        m_old = m_refs[h][...]
        m_new = jnp.maximum(m_old, smax + off)
        alpha = jnp.exp2(m_old - m_new)
        p = jnp.exp2(s - (m_new - off))
        l_refs[h][...] = alpha * l_refs[h][...] + jnp.sum(p, axis=0, keepdims=True)
        m_refs[h][...] = m_new
        vt = jnp.concatenate([vt_ref[0, h, j + d] for d in range(len(kinds))], axis=1)
        r = jnp.dot(vt, p.astype(BF16), preferred_element_type=F32)
        acc_refs[h][...] = alpha * acc_refs[h][...] + r

    def run(groups, next_j, sub):
        stages = [(h, j, kinds) for (j, kinds) in groups for h in range(N_HEADS)]
        for i, (h, j, kinds) in enumerate(stages):
            ahead = i + QK_LOOKAHEAD
            if ahead < len(stages):
                qk(*stages[ahead])
            elif next_j is not None:
                qk(ahead - len(stages), next_j, (None,) * NEXT_SPAN, add_bias=False)
            else:
                nxt = (qt_ref, (sub + 1) * t) if sub + 1 < ATTN_TILES_PER_STEP else (qtn_ref, 0)
                load_q(ahead - len(stages), nxt, qsn_refs)
                qk(ahead - len(stages), 0, (None,) * NEXT_SPAN, add_bias=False, q_refs=qsn_refs)
            softmax_pv(h, j, kinds, bias_pending=i < QK_LOOKAHEAD)
            if kinds[-1] is True:
                finalize(h, sub)

    def finalize(h, sub):
        o = acc_refs[h][...] * (1.0 / l_refs[h][...])
        o = o[:, :t] - lam * o[:, t:]
        ms = jnp.mean(o * o, axis=0, keepdims=True)
        y = (o * lax.rsqrt(ms + SUBLN_EPS)).T * (sg_ref[...] * (1.0 - LAMBDA_INIT))
        out_ref[0, sub * t:(sub + 1) * t, h * V_DIM:(h + 1) * V_DIM] = y.astype(BF16)

    def query_tile(sub):
        qi = pl.program_id(1) * ATTN_TILES_PER_STEP + sub

        if sub == 0:
            @pl.when(qi == 0)
            def _():
                for h in range(QK_LOOKAHEAD):
                    load_q(h, (qt_ref, 0), qsn_refs)
                    qk(h, 0, (True,), add_bias=False, q_refs=qsn_refs)

        for h in range(QK_LOOKAHEAD):
            qs_refs[h][...] = qsn_refs[h][...]
        for h in range(QK_LOOKAHEAD, N_HEADS):
            load_q(h, (qt_ref, sub * t), qs_refs)
        for h in range(N_HEADS):
            m_refs[h][...] = jnp.full(m_refs[h].shape, -jnp.inf, F32)
            l_refs[h][...] = jnp.zeros(l_refs[h].shape, F32)
            acc_refs[h][...] = jnp.zeros(acc_refs[h].shape, F32)

        n_far = qi - 1

        def far_body(i, carry):
            j = FAR_PER_TRIP * i
            run([(j + d, (None, None)) for d in range(0, FAR_PER_TRIP, 2)], j + FAR_PER_TRIP, sub)
            return carry

        lax.fori_loop(0, jnp.maximum(n_far, 0) // FAR_PER_TRIP, far_body, 0)

        if sub == 0:
            @pl.when(qi == 0)
            def _():
                run([(qi, (True,))], None, sub)

        nq = k_ref.shape[2] // t
        for rem in range(FAR_PER_TRIP):
            if not any((q - 1) % FAR_PER_TRIP == rem
                       for q in range(1, nq) if q % ATTN_TILES_PER_STEP == sub):
                continue

            @pl.when((qi >= 1) & (n_far % FAR_PER_TRIP == rem))
            def _():
                kinds = (None,) * rem + (False, True)
                first = qi - 1 - rem
                run([(first + d, kinds[d:d + 2]) for d in range(0, len(kinds), 2)], None, sub)

    for sub in range(ATTN_TILES_PER_STEP):
        query_tile(sub)


def _diff_attn(rel_bias, qt, k, vt, lambdas, subln_g):
    b, _, s = qt.shape
    t, tps = ATTN_TILE, ATTN_TILES_PER_STEP
    nq = s // t
    assert s % (t * tps) == 0 and t % BIAS_TILE == 0
    assert FAR_PER_TRIP % NEXT_SPAN == 0 and QK_LOOKAHEAD < N_HEADS
    assert (_causal_bucket(np.arange(BIAS_TILE + 1, s)) == NUM_BUCKETS - 1).all()
    per_head = lambda shape, dtype: [pltpu.VMEM(shape, dtype) for _ in range(N_HEADS)]
    return pl.pallas_call(
        _attn_kernel,
        grid=(b, nq // tps),
        in_specs=[
            pl.BlockSpec(memory_space=pltpu.SMEM),
            pl.BlockSpec((1, ATTN_WIDTH, tps * t), lambda bi, qi: (bi, 0, qi)),
            pl.BlockSpec((1, QK_LOOKAHEAD * V_DIM, t),
                         lambda bi, qi: (bi, 0, jnp.minimum((qi + 1) * tps, nq - 1))),
            pl.BlockSpec((1, N_HEADS, s, V_DIM), lambda bi, qi: (bi, 0, 0, 0)),
            pl.BlockSpec((1, N_HEADS, nq, V_DIM, t), lambda bi, qi: (bi, 0, 0, 0, 0)),
            pl.BlockSpec((2, BIAS_TILE, BIAS_TILE), lambda bi, qi: (0, 0, 0)),
        ] + [pl.BlockSpec((1, HEAD_DIM), lambda bi, qi: (0, 0)) for _ in lambdas] + [
            pl.BlockSpec((1, V_DIM), lambda bi, qi: (0, 0)),
        ],
        out_specs=pl.BlockSpec((1, tps * t, ATTN_WIDTH), lambda bi, qi: (bi, qi, 0)),
        out_shape=jax.ShapeDtypeStruct((b, s, ATTN_WIDTH), BF16),
        scratch_shapes=(per_head((1, 2 * t), F32) + per_head((1, 2 * t), F32)
                        + per_head((1, 2 * t), F32)
                        + per_head((V_DIM, 2 * t), F32) + per_head((V_DIM, 2 * t), BF16)
                        + per_head((NEXT_SPAN * t, 2 * t), F32)
                        + [pltpu.VMEM((V_DIM, 2 * t), BF16) for _ in range(QK_LOOKAHEAD)]
                        + [pltpu.VMEM((N_HEADS, 2, BIAS_TILE, BIAS_TILE), F32)]),
        compiler_params=pltpu.CompilerParams(
            dimension_semantics=("arbitrary", "arbitrary"),
            vmem_limit_bytes=VMEM_LIMIT_BYTES),
        name="diff_attn",
    )(rel_bias, qt, qt, k, vt, jnp.asarray(_bucket_tiles(BIAS_TILE)), *lambdas, subln_g)


def _ffn_kernel(x_ref, yp_ref, ya_ref, wo_ref, gf_ref, w1_ref, cw_ref, cb_ref,
                w2_ref, gl_ref, out_ref, carry_ref, gbuf_ref, act_ref, hn_ref):
    si = pl.program_id(1)
    tm = x_ref.shape[1]
    ck = FFN_CHUNK
    rows = tm // FFN_ROW_BLOCKS

    @pl.when(si == 0)
    def _():
        carry_ref[...] = jnp.zeros(carry_ref.shape, F32)

    for r0 in range(0, tm, rows):
        y = (jnp.dot(yp_ref[0, r0:r0 + rows, :], wo_ref[0:POOL_WIDTH, :], preferred_element_type=F32)
             + jnp.dot(ya_ref[0, r0:r0 + rows, :], wo_ref[POOL_WIDTH:, :], preferred_element_type=F32))
        x1 = x_ref[0, r0:r0 + rows, :] + y
        out_ref[0, r0:r0 + rows, :] = x1
        ms = jnp.mean(x1 * x1, axis=-1, keepdims=True)
        hn_ref[r0:r0 + rows, :] = (x1 * lax.rsqrt(ms + NORM_EPS) * gf_ref[...]).astype(BF16)

    for c in range(D_FF // ck):
        lo, hi = c * ck, (c + 1) * ck
        g = jnp.dot(hn_ref[...], w1_ref[:, lo:hi], preferred_element_type=F32)
        u = jnp.dot(hn_ref[...], w1_ref[:, D_FF + lo:D_FF + hi], preferred_element_type=F32)
        gbuf_ref[0:CONV_HALO, :] = carry_ref[:, lo:hi]
        gbuf_ref[CONV_HALO:, :] = g
        g1 = gbuf_ref[pl.ds(CONV_HALO - 1, tm), :]
        g2 = gbuf_ref[pl.ds(CONV_HALO - 2, tm), :]
        carry_ref[:, lo:hi] = g[tm - CONV_HALO:, :]
        cw = 0.5 * cw_ref[:, lo:hi]
        hc = cw[0:1] * g2 + cw[1:2] * g1 + cw[2:3] * g + 0.5 * cb_ref[:, lo:hi]
        a = (hc + hc * jnp.tanh(hc)) * u
        act_ref[:, lo:hi] = a.astype(BF16)

    for r0 in range(0, tm, rows):
        x2 = out_ref[0, r0:r0 + rows, :] + jnp.dot(act_ref[r0:r0 + rows, :], w2_ref[...],
                                                   preferred_element_type=F32)
        ms2 = jnp.mean(x2 * x2, axis=-1, keepdims=True)
        out_ref[0, r0:r0 + rows, :] = x2 * lax.rsqrt(ms2 + NORM_EPS) * gl_ref[...]


def _out_ffn(x, y_pool, y_attn, w_out, g_ffn, w1, conv_w, conv_b, w2, g_final):
    b, s, d = x.shape
    tm = FFN_ROWS
    assert s % tm == 0 and tm % FFN_ROW_BLOCKS == 0 and D_FF % FFN_CHUNK == 0
    const = lambda bi, si: (0, 0)
    resident = functools.partial(pl.BlockSpec, index_map=const, pipeline_mode=pl.Buffered(1))
    return pl.pallas_call(
        _ffn_kernel,
        grid=(b, s // tm),
        in_specs=[
            pl.BlockSpec((1, tm, d), lambda bi, si: (bi, si, 0)),
            pl.BlockSpec((1, tm, POOL_WIDTH), lambda bi, si: (bi, si, 0)),
            pl.BlockSpec((1, tm, ATTN_WIDTH), lambda bi, si: (bi, si, 0)),
            resident((d, d)),
            resident((1, d)),
            resident((d, 2 * D_FF)),
            resident((3, D_FF)),
            resident((1, D_FF)),
            resident((D_FF, d)),
            resident((1, d)),
        ],
        out_specs=pl.BlockSpec((1, tm, d), lambda bi, si: (bi, si, 0)),
        out_shape=jax.ShapeDtypeStruct((b, s, d), F32),
        scratch_shapes=[pltpu.VMEM((CONV_HALO, D_FF), F32),
                        pltpu.VMEM((CONV_HALO + tm, FFN_CHUNK), F32),
                        pltpu.VMEM((tm, D_FF), BF16),
                        pltpu.VMEM((tm, d), BF16)],
        compiler_params=pltpu.CompilerParams(
            dimension_semantics=("arbitrary", "arbitrary"),
            vmem_limit_bytes=VMEM_LIMIT_BYTES),
        name="out_ffn",
    )(x, y_pool, y_attn, w_out, g_ffn, w1, conv_w, conv_b, w2, g_final)


def kernel(x, norm_mix_g, w_in, pool_w, pool_scale, lambda_q1, lambda_k1, lambda_q2, lambda_k2,
           subln_g, rel_bias, w_out, norm_ffn_g, ffn_w_in, ffn_conv_w, ffn_conv_b, ffn_w_out,
           norm_final_g):
    y_pool, qt, k, vt, w_out16, w1_16, w2_16 = _in_proj(
        x, norm_mix_g, w_in[0], pool_w[0], pool_scale, (w_out[0], ffn_w_in[0], ffn_w_out[0]))
    y_attn = _diff_attn(rel_bias, qt, k, vt, (lambda_q1, lambda_k1, lambda_q2, lambda_k2), subln_g)
    return _out_ffn(x, y_pool, y_attn, w_out16, norm_ffn_g, w1_16, ffn_conv_w[0], ffn_conv_b,
                    w2_16, norm_final_g.reshape(1, D_MODEL))
```

```python
import functools
import math

import numpy as np
import jax
import jax.numpy as jnp
from jax import lax
from jax.experimental import pallas as pl
from jax.experimental.pallas import tpu as pltpu

D_MODEL = 1024
POOL_WIDTH = 512
POOL_GROUPS = 4
POOL_GROUP_DIM = 128
POOL_WINDOWS = (2, 4, 8, 16)
POOL_HALO = 16
ATTN_WIDTH = 512
N_HEADS = 4
HEAD_DIM = 64
V_DIM = 128
IN_PROJ_WIDTH = 2048
NUM_BUCKETS = 32
MAX_DISTANCE = 128
D_FF = 2816
NORM_EPS = 1e-6
SUBLN_EPS = 1e-5
NEG_INF = -1e30
LAMBDA_INIT = 0.8 - 0.6 * math.exp(-0.3 * 0)
LOG2E = math.log2(math.e)

ATTN_TILE = 256
BIAS_TILE = 128
QK_LOOKAHEAD = 2
FAR_PER_TRIP = 4
NEXT_SPAN = 2
ATTN_TILES_PER_STEP = 2
IN_ROWS = 1024
IN_ROW_BLOCKS = 2
FFN_ROWS = 1024
FFN_CHUNK = 256
FFN_ROW_BLOCKS = 4
CONV_HALO = 8
WEIGHT_CAST_ROWS = 128
N_LATER_WEIGHTS = 3
VMEM_LIMIT_BYTES = 56 * 1024 * 1024
BF16_SUBLANES = 16

F32 = jnp.float32
BF16 = jnp.bfloat16


def _causal_bucket(n):
    max_exact = NUM_BUCKETS // 2
    nf = np.maximum(n, 1).astype(np.float64)
    large = max_exact + (np.log(nf / max_exact) / math.log(MAX_DISTANCE / max_exact)
                         * (NUM_BUCKETS - max_exact)).astype(np.int64)
    return np.where(n < max_exact, n, np.minimum(large, NUM_BUCKETS - 1))


def _bucket_tiles(tile):
    k = np.arange(tile)[:, None]
    q = np.arange(tile)[None, :]
    out = []
    for offset in (0, tile):
        n = q - k + offset
        out.append(np.where(n < 0, -1, _causal_bucket(n)))
    return np.stack(out).astype(np.int32)


def _in_proj_kernel(x_ref, g_ref, w32_ref, pw_ref, ps_ref, *rest):
    later32_refs = rest[:N_LATER_WEIGHTS]
    ypool_ref, qt_ref, k_ref, vt_ref = rest[N_LATER_WEIGHTS:N_LATER_WEIGHTS + 4]
    later16_refs = rest[N_LATER_WEIGHTS + 4:2 * N_LATER_WEIGHTS + 4]
    ext_ref, w_ref, hn_ref = rest[2 * N_LATER_WEIGHTS + 4:]
    si = pl.program_id(1)
    tm = x_ref.shape[1]

    @pl.when((pl.program_id(0) == 0) & (si == 0))
    def _():
        for r in range(0, D_MODEL, WEIGHT_CAST_ROWS):
            w_ref[r:r + WEIGHT_CAST_ROWS, :] = w32_ref[r:r + WEIGHT_CAST_ROWS, :].astype(BF16)

    @pl.when(si == 0)
    def _():
        ext_ref[0:POOL_HALO, :] = jnp.zeros((POOL_HALO, POOL_WIDTH), F32)

    rows = tm // IN_ROW_BLOCKS
    zp_blocks = []
    for r0 in range(0, tm, rows):
        x = x_ref[0, r0:r0 + rows, :]
        ms = jnp.mean(x * x, axis=-1, keepdims=True)
        hn = (x * lax.rsqrt(ms + NORM_EPS) * g_ref[...]).astype(BF16)
        hn_ref[r0:r0 + rows, :] = hn
        zp_blocks.append(jnp.dot(hn, w_ref[:, :POOL_WIDTH], preferred_element_type=F32))
    zp = jnp.concatenate(zp_blocks, axis=0)
    z = jnp.dot(hn_ref[...], w_ref[:, POOL_WIDTH:], preferred_element_type=F32)

    for src_ref, dst_ref in zip(later32_refs, later16_refs):
        dst_ref[...] = src_ref[...].astype(BF16)

    ext_ref[POOL_HALO:, :] = zp
    tpos = si * tm + lax.broadcasted_iota(jnp.int32, (tm, 1), 0) + 1
    for gi, w in enumerate(POOL_WINDOWS):
        lo, hi = gi * POOL_GROUP_DIM, (gi + 1) * POOL_GROUP_DIM
        tot = ext_ref[:, lo:hi]
        step = 1
        while step < w:
            tot = tot + pltpu.roll(tot, step, axis=0)
            step *= 2
        tot = tot[POOL_HALO:, :]
        cnt = jnp.minimum(tpos, w).astype(F32)
        pooled = tot / cnt - zp[:, lo:hi]
        yg = jnp.dot(pooled.astype(BF16), pw_ref[gi].astype(BF16), preferred_element_type=F32)
        ypool_ref[0, :, lo:hi] = (yg * ps_ref[:, lo:hi]).astype(BF16)
    ext_ref[0:POOL_HALO, :] = zp[tm - POOL_HALO:, :]

    t = ATTN_TILE
    q0, k0, v0 = 0, ATTN_WIDTH, 2 * ATTN_WIDTH
    scale = LOG2E * HEAD_DIM ** -0.5
    for h in range(N_HEADS):
        zq = z[:, q0 + h * V_DIM:q0 + (h + 1) * V_DIM] * scale
        qt_ref[0, h * V_DIM:(h + 1) * V_DIM, :] = zq.T.astype(BF16)
        k_ref[0, h] = z[:, k0 + h * V_DIM:k0 + (h + 1) * V_DIM].astype(BF16)
        vt = z[:, v0 + h * V_DIM:v0 + (h + 1) * V_DIM].T.astype(BF16)
        for j in range(tm // t):
            vt_ref[0, h, j] = vt[:, j * t:(j + 1) * t]


def _in_proj(x, g, w_in, pool_w, pool_scale, later_weights):
    b, s, d = x.shape
    tm, t = IN_ROWS, ATTN_TILE
    steps = s // tm
    assert s % tm == 0 and tm % t == 0 and tm % IN_ROW_BLOCKS == 0
    assert len(later_weights) == N_LATER_WEIGHTS
    assert all(w.shape[0] % (b * steps * BF16_SUBLANES) == 0 for w in later_weights)
    slabs = [(w.shape[0] // (b * steps), w.shape[1]) for w in later_weights]
    slab_specs = [pl.BlockSpec(sl, lambda bi, si: (bi * steps + si, 0)) for sl in slabs]
    return pl.pallas_call(
        _in_proj_kernel,
        grid=(b, s // tm),
        in_specs=[
            pl.BlockSpec((1, tm, d), lambda bi, si: (bi, si, 0)),
            pl.BlockSpec((1, d), lambda bi, si: (0, 0)),
            pl.BlockSpec((d, IN_PROJ_WIDTH), lambda bi, si: (0, 0), pipeline_mode=pl.Buffered(1)),
            pl.BlockSpec((POOL_GROUPS, POOL_GROUP_DIM, POOL_GROUP_DIM), lambda bi, si: (0, 0, 0)),
            pl.BlockSpec((1, POOL_WIDTH), lambda bi, si: (0, 0)),
        ] + slab_specs,
        out_specs=[
            pl.BlockSpec((1, tm, POOL_WIDTH), lambda bi, si: (bi, si, 0)),
            pl.BlockSpec((1, ATTN_WIDTH, tm), lambda bi, si: (bi, 0, si)),
            pl.BlockSpec((1, N_HEADS, tm, V_DIM), lambda bi, si: (bi, 0, si, 0)),
            pl.BlockSpec((1, N_HEADS, tm // t, V_DIM, t), lambda bi, si: (bi, 0, si, 0, 0)),
        ] + slab_specs,
        out_shape=[
            jax.ShapeDtypeStruct((b, s, POOL_WIDTH), BF16),
            jax.ShapeDtypeStruct((b, ATTN_WIDTH, s), BF16),
            jax.ShapeDtypeStruct((b, N_HEADS, s, V_DIM), BF16),
            jax.ShapeDtypeStruct((b, N_HEADS, s // t, V_DIM, t), BF16),
        ] + [jax.ShapeDtypeStruct(w.shape, BF16) for w in later_weights],
        scratch_shapes=[pltpu.VMEM((POOL_HALO + tm, POOL_WIDTH), F32),
                        pltpu.VMEM((d, IN_PROJ_WIDTH), BF16),
                        pltpu.VMEM((tm, d), BF16)],
        compiler_params=pltpu.CompilerParams(
            dimension_semantics=("arbitrary", "arbitrary"),
            vmem_limit_bytes=VMEM_LIMIT_BYTES),
        name="in_proj",
    )(x, g, w_in, pool_w, pool_scale, *later_weights)


def _attn_kernel(rb_ref, qt_ref, qtn_ref, k_ref, vt_ref, bkt_ref, lq1_ref, lk1_ref, lq2_ref,
                 lk2_ref, sg_ref, out_ref, *scratch):
    t = ATTN_TILE
    m_refs, l_refs, smax_refs, acc_refs, qs_refs, s_refs = (
        scratch[i * N_HEADS:(i + 1) * N_HEADS] for i in range(6))
    qsn_refs = scratch[6 * N_HEADS:6 * N_HEADS + QK_LOOKAHEAD]
    bias_ref = scratch[6 * N_HEADS + QK_LOOKAHEAD]

    @pl.when((pl.program_id(0) == 0) & (pl.program_id(1) == 0))
    def _():
        bkt = bkt_ref[...]
        acc = [jnp.full(bkt.shape, NEG_INF, F32) for _ in range(N_HEADS)]
        for i in range(NUM_BUCKETS):
            hit = bkt == i
            acc = [jnp.where(hit, rb_ref[i, h] * LOG2E, acc[h]) for h in range(N_HEADS)]
        for h in range(N_HEADS):
            bias_ref[h] = acc[h]

    def load_q(h, src, dst_refs):
        src_ref, col0 = src
        qt = src_ref[0, h * V_DIM:(h + 1) * V_DIM, col0:col0 + t]
        row = lax.broadcasted_iota(jnp.int32, qt.shape, 0)
        zero = jnp.zeros_like(qt)
        dst_refs[h][:, 0:t] = jnp.where(row < HEAD_DIM, qt, zero)
        dst_refs[h][:, t:2 * t] = jnp.where(row >= HEAD_DIM, qt, zero)

    far_bias = [rb_ref[NUM_BUCKETS - 1, h] * LOG2E for h in range(N_HEADS)]

    lam = (jnp.exp(jnp.sum(lq1_ref[...] * lk1_ref[...], axis=1, keepdims=True))
           - jnp.exp(jnp.sum(lq2_ref[...] * lk2_ref[...], axis=1, keepdims=True))
           + LAMBDA_INIT)

    def qk(h, j, kinds, add_bias=True, q_refs=qs_refs):
        n = len(kinds) * t
        kj = k_ref[0, h, pl.ds(pl.multiple_of(j * t, t), n), :]
        s = jnp.dot(kj, q_refs[h][...], preferred_element_type=F32)
        if add_bias and has_bias(kinds):
            s = with_bias(s, h, kinds)
        s_refs[h][0:n, :] = s
        smax_refs[h][...] = jnp.max(s, axis=0, keepdims=True)

    def has_bias(kinds):
        return any(kd is not None for kd in kinds)

    def with_bias(s, h, kinds):
        b = jnp.concatenate([tile_bias(h, kd) for kd in kinds], axis=0)
        return s + jnp.concatenate([b, b], axis=1)

    def tile_bias(h, diag):
        n = t // BIAS_TILE
        pieces = {0: bias_ref[h, 0], 1: bias_ref[h, 1]}
        far = jnp.full((BIAS_TILE, BIAS_TILE), far_bias[h], F32)
        if diag is None:
            return jnp.full((t, t), far_bias[h], F32)
        masked = jnp.full((BIAS_TILE, BIAS_TILE), NEG_INF, F32)
        rows = []
        for kb in range(n):
            d = [qb - kb + (0 if diag else n) for qb in range(n)]
            rows.append(jnp.concatenate([masked if x < 0 else pieces.get(x, far) for x in d], axis=1))
        return jnp.concatenate(rows, axis=0)

    def softmax_pv(h, j, kinds, bias_pending):
        n = len(kinds) * t
        off = 0.0 if has_bias(kinds) else far_bias[h]
        s = s_refs[h][0:n, :]
        if bias_pending and has_bias(kinds):
            s = with_bias(s, h, kinds)
            smax = jnp.max(s, axis=0, keepdims=True)
        else:
            smax = smax_refs[h][...]
        m_old = m_refs[h][...]
        m_new = jnp.maximum(m_old, smax + off)
        alpha = jnp.exp2(m_old - m_new)
        p = jnp.exp2(s - (m_new - off))
        l_refs[h][...] = alpha * l_refs[h][...] + jnp.sum(p, axis=0, keepdims=True)
        m_refs[h][...] = m_new
        vt = jnp.concatenate([vt_ref[0, h, j + d] for d in range(len(kinds))], axis=1)
        r = jnp.dot(vt, p.astype(BF16), preferred_element_type=F32)
        acc_refs[h][...] = alpha * acc_refs[h][...] + r

    def run(groups, next_j, sub):
        stages = [(h, j, kinds) for (j, kinds) in groups for h in range(N_HEADS)]
        for i, (h, j, kinds) in enumerate(stages):
            ahead = i + QK_LOOKAHEAD
            if ahead < len(stages):
                qk(*stages[ahead])
            elif next_j is not None:
                qk(ahead - len(stages), next_j, (None,) * NEXT_SPAN, add_bias=False)
            else:
                nxt = (qt_ref, (sub + 1) * t) if sub + 1 < ATTN_TILES_PER_STEP else (qtn_ref, 0)
                load_q(ahead - len(stages), nxt, qsn_refs)
                qk(ahead - len(stages), 0, (None,) * NEXT_SPAN, add_bias=False, q_refs=qsn_refs)
            softmax_pv(h, j, kinds, bias_pending=i < QK_LOOKAHEAD)
            if kinds[-1] is True:
                finalize(h, sub)

    def finalize(h, sub):
        o = acc_refs[h][...] * (1.0 / l_refs[h][...])
        o = o[:, :t] - lam * o[:, t:]
        ms = jnp.mean(o * o, axis=0, keepdims=True)
        y = (o * lax.rsqrt(ms + SUBLN_EPS)).T * (sg_ref[...] * (1.0 - LAMBDA_INIT))
        out_ref[0, sub * t:(sub + 1) * t, h * V_DIM:(h + 1) * V_DIM] = y.astype(BF16)

    def query_tile(sub):
        qi = pl.program_id(1) * ATTN_TILES_PER_STEP + sub

        if sub == 0:
            @pl.when(qi == 0)
            def _():
                for h in range(QK_LOOKAHEAD):
                    load_q(h, (qt_ref, 0), qsn_refs)
                    qk(h, 0, (True,), add_bias=False, q_refs=qsn_refs)

        for h in range(QK_LOOKAHEAD):
            qs_refs[h][...] = qsn_refs[h][...]
        for h in range(QK_LOOKAHEAD, N_HEADS):
            load_q(h, (qt_ref, sub * t), qs_refs)
        for h in range(N_HEADS):
            m_refs[h][...] = jnp.full(m_refs[h].shape, -jnp.inf, F32)
            l_refs[h][...] = jnp.zeros(l_refs[h].shape, F32)
            acc_refs[h][...] = jnp.zeros(acc_refs[h].shape, F32)

        n_far = qi - 1

        def far_body(i, carry):
            j = FAR_PER_TRIP * i
            run([(j + d, (None, None)) for d in range(0, FAR_PER_TRIP, 2)], j + FAR_PER_TRIP, sub)
            return carry

        lax.fori_loop(0, jnp.maximum(n_far, 0) // FAR_PER_TRIP, far_body, 0)

        if sub == 0:
            @pl.when(qi == 0)
            def _():
                run([(qi, (True,))], None, sub)

        nq = k_ref.shape[2] // t
        for rem in range(FAR_PER_TRIP):
            if not any((q - 1) % FAR_PER_TRIP == rem
                       for q in range(1, nq) if q % ATTN_TILES_PER_STEP == sub):
                continue

            @pl.when((qi >= 1) & (n_far % FAR_PER_TRIP == rem))
            def _():
                kinds = (None,) * rem + (False, True)
                first = qi - 1 - rem
                run([(first + d, kinds[d:d + 2]) for d in range(0, len(kinds), 2)], None, sub)

    for sub in range(ATTN_TILES_PER_STEP):
        query_tile(sub)


def _diff_attn(rel_bias, qt, k, vt, lambdas, subln_g):
    b, _, s = qt.shape
    t, tps = ATTN_TILE, ATTN_TILES_PER_STEP
    nq = s // t
    assert s % (t * tps) == 0 and t % BIAS_TILE == 0
    assert FAR_PER_TRIP % NEXT_SPAN == 0 and QK_LOOKAHEAD < N_HEADS
    assert (_causal_bucket(np.arange(BIAS_TILE + 1, s)) == NUM_BUCKETS - 1).all()
    per_head = lambda shape, dtype: [pltpu.VMEM(shape, dtype) for _ in range(N_HEADS)]
    return pl.pallas_call(
        _attn_kernel,
        grid=(b, nq // tps),
        in_specs=[
            pl.BlockSpec(memory_space=pltpu.SMEM),
            pl.BlockSpec((1, ATTN_WIDTH, tps * t), lambda bi, qi: (bi, 0, qi)),
            pl.BlockSpec((1, QK_LOOKAHEAD * V_DIM, t),
                         lambda bi, qi: (bi, 0, jnp.minimum((qi + 1) * tps, nq - 1))),
            pl.BlockSpec((1, N_HEADS, s, V_DIM), lambda bi, qi: (bi, 0, 0, 0)),
            pl.BlockSpec((1, N_HEADS, nq, V_DIM, t), lambda bi, qi: (bi, 0, 0, 0, 0)),
            pl.BlockSpec((2, BIAS_TILE, BIAS_TILE), lambda bi, qi: (0, 0, 0)),
        ] + [pl.BlockSpec((1, HEAD_DIM), lambda bi, qi: (0, 0)) for _ in lambdas] + [
            pl.BlockSpec((1, V_DIM), lambda bi, qi: (0, 0)),
        ],
        out_specs=pl.BlockSpec((1, tps * t, ATTN_WIDTH), lambda bi, qi: (bi, qi, 0)),
        out_shape=jax.ShapeDtypeStruct((b, s, ATTN_WIDTH), BF16),
        scratch_shapes=(per_head((1, 2 * t), F32) + per_head((1, 2 * t), F32)
                        + per_head((1, 2 * t), F32)
                        + per_head((V_DIM, 2 * t), F32) + per_head((V_DIM, 2 * t), BF16)
                        + per_head((NEXT_SPAN * t, 2 * t), F32)
                        + [pltpu.VMEM((V_DIM, 2 * t), BF16) for _ in range(QK_LOOKAHEAD)]
                        + [pltpu.VMEM((N_HEADS, 2, BIAS_TILE, BIAS_TILE), F32)]),
        compiler_params=pltpu.CompilerParams(
            dimension_semantics=("arbitrary", "arbitrary"),
            vmem_limit_bytes=VMEM_LIMIT_BYTES),
        name="diff_attn",
    )(rel_bias, qt, qt, k, vt, jnp.asarray(_bucket_tiles(BIAS_TILE)), *lambdas, subln_g)


def _ffn_kernel(x_ref, yp_ref, ya_ref, wo_ref, gf_ref, w1_ref, cw_ref, cb_ref,
                w2_ref, gl_ref, out_ref, carry_ref, act_ref, hn_ref):
    si = pl.program_id(1)
    tm = x_ref.shape[1]
    ck = FFN_CHUNK
    rows = tm // FFN_ROW_BLOCKS

    @pl.when(si == 0)
    def _():
        carry_ref[...] = jnp.zeros(carry_ref.shape, F32)

    for r0 in range(0, tm, rows):
        y = (jnp.dot(yp_ref[0, r0:r0 + rows, :], wo_ref[0:POOL_WIDTH, :], preferred_element_type=F32)
             + jnp.dot(ya_ref[0, r0:r0 + rows, :], wo_ref[POOL_WIDTH:, :], preferred_element_type=F32))
        x1 = x_ref[0, r0:r0 + rows, :] + y
        out_ref[0, r0:r0 + rows, :] = x1
        ms = jnp.mean(x1 * x1, axis=-1, keepdims=True)
        hn_ref[r0:r0 + rows, :] = (x1 * lax.rsqrt(ms + NORM_EPS) * gf_ref[...]).astype(BF16)

    for c in range(D_FF // ck):
        lo, hi = c * ck, (c + 1) * ck
        g = jnp.dot(hn_ref[...], w1_ref[:, lo:hi], preferred_element_type=F32)
        u = jnp.dot(hn_ref[...], w1_ref[:, D_FF + lo:D_FF + hi], preferred_element_type=F32)
        head = jnp.concatenate([carry_ref[:, lo:hi], g[0:CONV_HALO, :]], axis=0)
        g1 = jnp.concatenate([pltpu.roll(head, 1, axis=0)[CONV_HALO:, :],
                              pltpu.roll(g, 1, axis=0)[CONV_HALO:, :]], axis=0)
        g2 = jnp.concatenate([pltpu.roll(head, 2, axis=0)[CONV_HALO:, :],
                              pltpu.roll(g, 2, axis=0)[CONV_HALO:, :]], axis=0)
        carry_ref[:, lo:hi] = g[tm - CONV_HALO:, :]
        cw = 0.5 * cw_ref[:, lo:hi]
        hc = cw[0:1] * g2 + cw[1:2] * g1 + cw[2:3] * g + 0.5 * cb_ref[:, lo:hi]
        a = (hc + hc * jnp.tanh(hc)) * u
        act_ref[:, lo:hi] = a.astype(BF16)

    for r0 in range(0, tm, rows):
        x2 = out_ref[0, r0:r0 + rows, :] + jnp.dot(act_ref[r0:r0 + rows, :], w2_ref[...],
                                                   preferred_element_type=F32)
        ms2 = jnp.mean(x2 * x2, axis=-1, keepdims=True)
        out_ref[0, r0:r0 + rows, :] = x2 * lax.rsqrt(ms2 + NORM_EPS) * gl_ref[...]


def _out_ffn(x, y_pool, y_attn, w_out, g_ffn, w1, conv_w, conv_b, w2, g_final):
    b, s, d = x.shape
    tm = FFN_ROWS
    assert s % tm == 0 and tm % FFN_ROW_BLOCKS == 0 and D_FF % FFN_CHUNK == 0
    const = lambda bi, si: (0, 0)
    resident = functools.partial(pl.BlockSpec, index_map=const, pipeline_mode=pl.Buffered(1))
    return pl.pallas_call(
        _ffn_kernel,
        grid=(b, s // tm),
        in_specs=[
            pl.BlockSpec((1, tm, d), lambda bi, si: (bi, si, 0)),
            pl.BlockSpec((1, tm, POOL_WIDTH), lambda bi, si: (bi, si, 0)),
            pl.BlockSpec((1, tm, ATTN_WIDTH), lambda bi, si: (bi, si, 0)),
            resident((d, d)),
            resident((1, d)),
            resident((d, 2 * D_FF)),
            resident((3, D_FF)),
            resident((1, D_FF)),
            resident((D_FF, d)),
            resident((1, d)),
        ],
        out_specs=pl.BlockSpec((1, tm, d), lambda bi, si: (bi, si, 0)),
        out_shape=jax.ShapeDtypeStruct((b, s, d), F32),
        scratch_shapes=[pltpu.VMEM((CONV_HALO, D_FF), F32),
                        pltpu.VMEM((tm, D_FF), BF16),
                        pltpu.VMEM((tm, d), BF16)],
        compiler_params=pltpu.CompilerParams(
            dimension_semantics=("arbitrary", "arbitrary"),
            vmem_limit_bytes=VMEM_LIMIT_BYTES),
        name="out_ffn",
    )(x, y_pool, y_attn, w_out, g_ffn, w1, conv_w, conv_b, w2, g_final)


def kernel(x, norm_mix_g, w_in, pool_w, pool_scale, lambda_q1, lambda_k1, lambda_q2, lambda_k2,
           subln_g, rel_bias, w_out, norm_ffn_g, ffn_w_in, ffn_conv_w, ffn_conv_b, ffn_w_out,
           norm_final_g):
    y_pool, qt, k, vt, w_out16, w1_16, w2_16 = _in_proj(
        x, norm_mix_g, w_in[0], pool_w[0], pool_scale, (w_out[0], ffn_w_in[0], ffn_w_out[0]))
    y_attn = _diff_attn(rel_bias, qt, k, vt, (lambda_q1, lambda_k1, lambda_q2, lambda_k2), subln_g)
    return _out_ffn(x, y_pool, y_attn, w_out16, norm_ffn_g, w1_16, ffn_conv_w[0], ffn_conv_b,
                    w2_16, norm_final_g.reshape(1, D_MODEL))
```

```python
import functools
import math

import numpy as np
import jax
import jax.numpy as jnp
from jax import lax
from jax.experimental import pallas as pl
from jax.experimental.pallas import tpu as pltpu

D_MODEL = 1024
POOL_WIDTH = 512
POOL_GROUPS = 4
POOL_GROUP_DIM = 128
POOL_WINDOWS = (2, 4, 8, 16)
POOL_HALO = 16
ATTN_WIDTH = 512
N_HEADS = 4
HEAD_DIM = 64
V_DIM = 128
IN_PROJ_WIDTH = 2048
NUM_BUCKETS = 32
MAX_DISTANCE = 128
D_FF = 2816
NORM_EPS = 1e-6
SUBLN_EPS = 1e-5
NEG_INF = -1e30
LAMBDA_INIT = 0.8 - 0.6 * math.exp(-0.3 * 0)
LOG2E = math.log2(math.e)

ATTN_TILE = 256
BIAS_TILE = 128
QK_LOOKAHEAD = 2
FAR_PER_TRIP = 4
NEXT_SPAN = 2
ATTN_TILES_PER_STEP = 2
IN_ROWS = 1024
IN_ROW_BLOCKS = 2
FFN_ROWS = 1024
FFN_CHUNK = 256
FFN_ROW_BLOCKS = 4
CONV_HALO = 8
WEIGHT_CAST_ROWS = 128
N_LATER_WEIGHTS = 3
VMEM_LIMIT_BYTES = 56 * 1024 * 1024
BF16_SUBLANES = 16

F32 = jnp.float32
BF16 = jnp.bfloat16


def _causal_bucket(n):
    max_exact = NUM_BUCKETS // 2
    nf = np.maximum(n, 1).astype(np.float64)
    large = max_exact + (np.log(nf / max_exact) / math.log(MAX_DISTANCE / max_exact)
                         * (NUM_BUCKETS - max_exact)).astype(np.int64)
    return np.where(n < max_exact, n, np.minimum(large, NUM_BUCKETS - 1))


def _bucket_tiles(tile):
    k = np.arange(tile)[:, None]
    q = np.arange(tile)[None, :]
    out = []
    for offset in (0, tile):
        n = q - k + offset
        out.append(np.where(n < 0, -1, _causal_bucket(n)))
    return np.stack(out).astype(np.int32)


def _in_proj_kernel(x_ref, g_ref, w32_ref, pw_ref, ps_ref, *rest):
    later32_refs = rest[:N_LATER_WEIGHTS]
    ypool_ref, qt_ref, k_ref, vt_ref = rest[N_LATER_WEIGHTS:N_LATER_WEIGHTS + 4]
    later16_refs = rest[N_LATER_WEIGHTS + 4:2 * N_LATER_WEIGHTS + 4]
    ext_ref, w_ref, hn_ref = rest[2 * N_LATER_WEIGHTS + 4:]
    si = pl.program_id(1)
    tm = x_ref.shape[1]

    @pl.when((pl.program_id(0) == 0) & (si == 0))
    def _():
        for r in range(0, D_MODEL, WEIGHT_CAST_ROWS):
            w_ref[r:r + WEIGHT_CAST_ROWS, :] = w32_ref[r:r + WEIGHT_CAST_ROWS, :].astype(BF16)

    @pl.when(si == 0)
    def _():
        ext_ref[0:POOL_HALO, :] = jnp.zeros((POOL_HALO, POOL_WIDTH), F32)

    rows = tm // IN_ROW_BLOCKS
    zp_blocks = []
    for r0 in range(0, tm, rows):
        x = x_ref[0, r0:r0 + rows, :]
        ms = jnp.mean(x * x, axis=-1, keepdims=True)
        hn = (x * lax.rsqrt(ms + NORM_EPS) * g_ref[...]).astype(BF16)
        hn_ref[r0:r0 + rows, :] = hn
        zp_blocks.append(jnp.dot(hn, w_ref[:, :POOL_WIDTH], preferred_element_type=F32))
    zp = jnp.concatenate(zp_blocks, axis=0)
    z = jnp.dot(hn_ref[...], w_ref[:, POOL_WIDTH:], preferred_element_type=F32)

    for src_ref, dst_ref in zip(later32_refs, later16_refs):
        dst_ref[...] = src_ref[...].astype(BF16)

    ext_ref[POOL_HALO:, :] = zp
    tpos = si * tm + lax.broadcasted_iota(jnp.int32, (tm, 1), 0) + 1
    for gi, w in enumerate(POOL_WINDOWS):
        lo, hi = gi * POOL_GROUP_DIM, (gi + 1) * POOL_GROUP_DIM
        tot = ext_ref[:, lo:hi]
        step = 1
        while step < w:
            tot = tot + pltpu.roll(tot, step, axis=0)
            step *= 2
        tot = tot[POOL_HALO:, :]
        cnt = jnp.minimum(tpos, w).astype(F32)
        pooled = tot / cnt - zp[:, lo:hi]
        yg = jnp.dot(pooled.astype(BF16), pw_ref[gi].astype(BF16), preferred_element_type=F32)
        ypool_ref[0, :, lo:hi] = (yg * ps_ref[:, lo:hi]).astype(BF16)
    ext_ref[0:POOL_HALO, :] = zp[tm - POOL_HALO:, :]

    t = ATTN_TILE
    q0, k0, v0 = 0, ATTN_WIDTH, 2 * ATTN_WIDTH
    scale = LOG2E * HEAD_DIM ** -0.5
    for h in range(N_HEADS):
        zq = z[:, q0 + h * V_DIM:q0 + (h + 1) * V_DIM] * scale
        qt_ref[0, h * V_DIM:(h + 1) * V_DIM, :] = zq.T.astype(BF16)
        k_ref[0, h] = z[:, k0 + h * V_DIM:k0 + (h + 1) * V_DIM].astype(BF16)
        vt = z[:, v0 + h * V_DIM:v0 + (h + 1) * V_DIM].T.astype(BF16)
        for j in range(tm // t):
            vt_ref[0, h, j] = vt[:, j * t:(j + 1) * t]


def _in_proj(x, g, w_in, pool_w, pool_scale, later_weights):
    b, s, d = x.shape
    tm, t = IN_ROWS, ATTN_TILE
    steps = s // tm
    assert s % tm == 0 and tm % t == 0 and tm % IN_ROW_BLOCKS == 0
    assert len(later_weights) == N_LATER_WEIGHTS
    assert all(w.shape[0] % (b * steps * BF16_SUBLANES) == 0 for w in later_weights)
    slabs = [(w.shape[0] // (b * steps), w.shape[1]) for w in later_weights]
    slab_specs = [pl.BlockSpec(sl, lambda bi, si: (bi * steps + si, 0)) for sl in slabs]
    return pl.pallas_call(
        _in_proj_kernel,
        grid=(b, s // tm),
        in_specs=[
            pl.BlockSpec((1, tm, d), lambda bi, si: (bi, si, 0)),
            pl.BlockSpec((1, d), lambda bi, si: (0, 0)),
            pl.BlockSpec((d, IN_PROJ_WIDTH), lambda bi, si: (0, 0), pipeline_mode=pl.Buffered(1)),
            pl.BlockSpec((POOL_GROUPS, POOL_GROUP_DIM, POOL_GROUP_DIM), lambda bi, si: (0, 0, 0)),
            pl.BlockSpec((1, POOL_WIDTH), lambda bi, si: (0, 0)),
        ] + slab_specs,
        out_specs=[
            pl.BlockSpec((1, tm, POOL_WIDTH), lambda bi, si: (bi, si, 0)),
            pl.BlockSpec((1, ATTN_WIDTH, tm), lambda bi, si: (bi, 0, si)),
            pl.BlockSpec((1, N_HEADS, tm, V_DIM), lambda bi, si: (bi, 0, si, 0)),
            pl.BlockSpec((1, N_HEADS, tm // t, V_DIM, t), lambda bi, si: (bi, 0, si, 0, 0)),
        ] + slab_specs,
        out_shape=[
            jax.ShapeDtypeStruct((b, s, POOL_WIDTH), BF16),
            jax.ShapeDtypeStruct((b, ATTN_WIDTH, s), BF16),
            jax.ShapeDtypeStruct((b, N_HEADS, s, V_DIM), BF16),
            jax.ShapeDtypeStruct((b, N_HEADS, s // t, V_DIM, t), BF16),
        ] + [jax.ShapeDtypeStruct(w.shape, BF16) for w in later_weights],
        scratch_shapes=[pltpu.VMEM((POOL_HALO + tm, POOL_WIDTH), F32),
                        pltpu.VMEM((d, IN_PROJ_WIDTH), BF16),
                        pltpu.VMEM((tm, d), BF16)],
        compiler_params=pltpu.CompilerParams(
            dimension_semantics=("arbitrary", "arbitrary"),
            vmem_limit_bytes=VMEM_LIMIT_BYTES),
        name="in_proj",
    )(x, g, w_in, pool_w, pool_scale, *later_weights)


def _attn_kernel(rb_ref, qt_ref, qtn_ref, k_ref, vt_ref, bkt_ref, lq1_ref, lk1_ref, lq2_ref,
                 lk2_ref, sg_ref, out_ref, *scratch):
    t = ATTN_TILE
    m_refs, l_refs, smax_refs, acc_refs, qs_refs, s_refs = (
        scratch[i * N_HEADS:(i + 1) * N_HEADS] for i in range(6))
    qsn_refs = scratch[6 * N_HEADS:6 * N_HEADS + QK_LOOKAHEAD]
    bias_ref = scratch[6 * N_HEADS + QK_LOOKAHEAD]

    @pl.when((pl.program_id(0) == 0) & (pl.program_id(1) == 0))
    def _():
        bkt = bkt_ref[...]
        acc = [jnp.full(bkt.shape, NEG_INF, F32) for _ in range(N_HEADS)]
        for i in range(NUM_BUCKETS):
            hit = bkt == i
            acc = [jnp.where(hit, rb_ref[i, h] * LOG2E, acc[h]) for h in range(N_HEADS)]
        for h in range(N_HEADS):
            bias_ref[h] = acc[h]

    def load_q(h, src, dst_refs):
        src_ref, col0 = src
        qt = src_ref[0, h * V_DIM:(h + 1) * V_DIM, col0:col0 + t]
        row = lax.broadcasted_iota(jnp.int32, qt.shape, 0)
        zero = jnp.zeros_like(qt)
        dst_refs[h][:, 0:t] = jnp.where(row < HEAD_DIM, qt, zero)
        dst_refs[h][:, t:2 * t] = jnp.where(row >= HEAD_DIM, qt, zero)

    far_bias = [rb_ref[NUM_BUCKETS - 1, h] * LOG2E for h in range(N_HEADS)]

    lam = (jnp.exp(jnp.sum(lq1_ref[...] * lk1_ref[...], axis=1, keepdims=True))
           - jnp.exp(jnp.sum(lq2_ref[...] * lk2_ref[...], axis=1, keepdims=True))
           + LAMBDA_INIT)

    def qk(h, j, kinds, add_bias=True, q_refs=qs_refs):
        n = len(kinds) * t
        kj = k_ref[0, h, pl.ds(pl.multiple_of(j * t, t), n), :]
        s = jnp.dot(kj, q_refs[h][...], preferred_element_type=F32)
        if add_bias and has_bias(kinds):
            s = with_bias(s, h, kinds)
        s_refs[h][0:n, :] = s
        smax_refs[h][...] = jnp.max(s, axis=0, keepdims=True)

    def has_bias(kinds):
        return any(kd is not None for kd in kinds)

    def with_bias(s, h, kinds):
        b = jnp.concatenate([tile_bias(h, kd) for kd in kinds], axis=0)
        return s + jnp.concatenate([b, b], axis=1)

    def tile_bias(h, diag):
        n = t // BIAS_TILE
        pieces = {0: bias_ref[h, 0], 1: bias_ref[h, 1]}
        far = jnp.full((BIAS_TILE, BIAS_TILE), far_bias[h], F32)
        if diag is None:
            return jnp.full((t, t), far_bias[h], F32)
        masked = jnp.full((BIAS_TILE, BIAS_TILE), NEG_INF, F32)
        rows = []
        for kb in range(n):
            d = [qb - kb + (0 if diag else n) for qb in range(n)]
            rows.append(jnp.concatenate([masked if x < 0 else pieces.get(x, far) for x in d], axis=1))
        return jnp.concatenate(rows, axis=0)

    def softmax_pv(h, j, kinds, bias_pending):
        n = len(kinds) * t
        off = 0.0 if has_bias(kinds) else far_bias[h]
        s = s_refs[h][0:n, :]
        if bias_pending and has_bias(kinds):
            s = with_bias(s, h, kinds)
            smax = jnp.max(s, axis=0, keepdims=True)
        else:
            smax = smax_refs[h][...]
        m_old = m_refs[h][...]
        m_new = jnp.maximum(m_old, smax + off)
        alpha = jnp.exp2(m_old - m_new)
        p = jnp.exp2(s - (m_new - off))
        l_refs[h][...] = alpha * l_refs[h][...] + jnp.sum(p, axis=0, keepdims=True)
        m_refs[h][...] = m_new
        vt = jnp.concatenate([vt_ref[0, h, j + d] for d in range(len(kinds))], axis=1)
        r = jnp.dot(vt, p.astype(BF16), preferred_element_type=F32)
        acc_refs[h][...] = alpha * acc_refs[h][...] + r

    def run(groups, next_j, sub):
        stages = [(h, j, kinds) for (j, kinds) in groups for h in range(N_HEADS)]
        for i, (h, j, kinds) in enumerate(stages):
            ahead = i + QK_LOOKAHEAD
            if ahead < len(stages):
                qk(*stages[ahead])
            elif next_j is not None:
                qk(ahead - len(stages), next_j, (None,) * NEXT_SPAN, add_bias=False)
            else:
                nxt = (qt_ref, (sub + 1) * t) if sub + 1 < ATTN_TILES_PER_STEP else (qtn_ref, 0)
                load_q(ahead - len(stages), nxt, qsn_refs)
                qk(ahead - len(stages), 0, (None,) * NEXT_SPAN, add_bias=False, q_refs=qsn_refs)
            softmax_pv(h, j, kinds, bias_pending=i < QK_LOOKAHEAD)
            if kinds[-1] is True:
                finalize(h, sub)

    def finalize(h, sub):
        o = acc_refs[h][...] * (1.0 / l_refs[h][...])
        o = o[:, :t] - lam * o[:, t:]
        ms = jnp.mean(o * o, axis=0, keepdims=True)
        y = (o * lax.rsqrt(ms + SUBLN_EPS)).T * (sg_ref[...] * (1.0 - LAMBDA_INIT))
        out_ref[0, sub * t:(sub + 1) * t, h * V_DIM:(h + 1) * V_DIM] = y.astype(BF16)

    def query_tile(sub):
        qi = pl.program_id(1) * ATTN_TILES_PER_STEP + sub

        if sub == 0:
            @pl.when(qi == 0)
            def _():
                for h in range(QK_LOOKAHEAD):
                    load_q(h, (qt_ref, 0), qsn_refs)
                    qk(h, 0, (True,), add_bias=False, q_refs=qsn_refs)

        for h in range(QK_LOOKAHEAD):
            qs_refs[h][...] = qsn_refs[h][...]
        for h in range(QK_LOOKAHEAD, N_HEADS):
            load_q(h, (qt_ref, sub * t), qs_refs)
        for h in range(N_HEADS):
            m_refs[h][...] = jnp.full(m_refs[h].shape, -jnp.inf, F32)
            l_refs[h][...] = jnp.zeros(l_refs[h].shape, F32)
            acc_refs[h][...] = jnp.zeros(acc_refs[h].shape, F32)

        n_far = qi - 1

        def far_body(i, carry):
            j = FAR_PER_TRIP * i
            run([(j + d, (None, None)) for d in range(0, FAR_PER_TRIP, 2)], j + FAR_PER_TRIP, sub)
            return carry

        lax.fori_loop(0, jnp.maximum(n_far, 0) // FAR_PER_TRIP, far_body, 0)

        if sub == 0:
            @pl.when(qi == 0)
            def _():
                run([(qi, (True,))], None, sub)

        nq = k_ref.shape[2] // t
        for rem in range(FAR_PER_TRIP):
            if not any((q - 1) % FAR_PER_TRIP == rem
                       for q in range(1, nq) if q % ATTN_TILES_PER_STEP == sub):
                continue

            @pl.when((qi >= 1) & (n_far % FAR_PER_TRIP == rem))
            def _():
                kinds = (None,) * rem + (False, True)
                first = qi - 1 - rem
                run([(first + d, kinds[d:d + 2]) for d in range(0, len(kinds), 2)], None, sub)

    for sub in range(ATTN_TILES_PER_STEP):
        query_tile(sub)


def _diff_attn(rel_bias, qt, k, vt, lambdas, subln_g):
    b, _, s = qt.shape
    t, tps = ATTN_TILE, ATTN_TILES_PER_STEP
    nq = s // t
    assert s % (t * tps) == 0 and t % BIAS_TILE == 0
    assert FAR_PER_TRIP % NEXT_SPAN == 0 and QK_LOOKAHEAD < N_HEADS
    assert (_causal_bucket(np.arange(BIAS_TILE + 1, s)) == NUM_BUCKETS - 1).all()
    per_head = lambda shape, dtype: [pltpu.VMEM(shape, dtype) for _ in range(N_HEADS)]
    return pl.pallas_call(
        _attn_kernel,
        grid=(b, nq // tps),
        in_specs=[
            pl.BlockSpec(memory_space=pltpu.SMEM),
            pl.BlockSpec((1, ATTN_WIDTH, tps * t), lambda bi, qi: (bi, 0, qi)),
            pl.BlockSpec((1, QK_LOOKAHEAD * V_DIM, t),
                         lambda bi, qi: (bi, 0, jnp.minimum((qi + 1) * tps, nq - 1))),
            pl.BlockSpec((1, N_HEADS, s, V_DIM), lambda bi, qi: (bi, 0, 0, 0)),
            pl.BlockSpec((1, N_HEADS, nq, V_DIM, t), lambda bi, qi: (bi, 0, 0, 0, 0)),
            pl.BlockSpec((2, BIAS_TILE, BIAS_TILE), lambda bi, qi: (0, 0, 0)),
        ] + [pl.BlockSpec((1, HEAD_DIM), lambda bi, qi: (0, 0)) for _ in lambdas] + [
            pl.BlockSpec((1, V_DIM), lambda bi, qi: (0, 0)),
        ],
        out_specs=pl.BlockSpec((1, tps * t, ATTN_WIDTH), lambda bi, qi: (bi, qi, 0)),
        out_shape=jax.ShapeDtypeStruct((b, s, ATTN_WIDTH), BF16),
        scratch_shapes=(per_head((1, 2 * t), F32) + per_head((1, 2 * t), F32)
                        + per_head((1, 2 * t), F32)
                        + per_head((V_DIM, 2 * t), F32) + per_head((V_DIM, 2 * t), BF16)
                        + per_head((NEXT_SPAN * t, 2 * t), F32)
                        + [pltpu.VMEM((V_DIM, 2 * t), BF16) for _ in range(QK_LOOKAHEAD)]
                        + [pltpu.VMEM((N_HEADS, 2, BIAS_TILE, BIAS_TILE), F32)]),
        compiler_params=pltpu.CompilerParams(
            dimension_semantics=("arbitrary", "arbitrary"),
            vmem_limit_bytes=VMEM_LIMIT_BYTES),
        name="diff_attn",
    )(rel_bias, qt, qt, k, vt, jnp.asarray(_bucket_tiles(BIAS_TILE)), *lambdas, subln_g)


def _ffn_kernel(x_ref, yp_ref, ya_ref, wo_ref, gf_ref, w1_ref, cw_ref, cb_ref,
                w2_ref, gl_ref, out_ref, carry_ref, act_ref, hn_ref):
    si = pl.program_id(1)
    tm = x_ref.shape[1]
    ck = FFN_CHUNK
    rows = tm // FFN_ROW_BLOCKS

    @pl.when(si == 0)
    def _():
        carry_ref[...] = jnp.zeros(carry_ref.shape, F32)

    for r0 in range(0, tm, rows):
        y = (jnp.dot(yp_ref[0, r0:r0 + rows, :], wo_ref[0:POOL_WIDTH, :], preferred_element_type=F32)
             + jnp.dot(ya_ref[0, r0:r0 + rows, :], wo_ref[POOL_WIDTH:, :], preferred_element_type=F32))
        x1 = x_ref[0, r0:r0 + rows, :] + y
        out_ref[0, r0:r0 + rows, :] = x1
        ms = jnp.mean(x1 * x1, axis=-1, keepdims=True)
        hn_ref[r0:r0 + rows, :] = (x1 * lax.rsqrt(ms + NORM_EPS) * gf_ref[...]).astype(BF16)

    for c in range(D_FF // ck):
        lo, hi = c * ck, (c + 1) * ck
        g = jnp.dot(hn_ref[...], w1_ref[:, lo:hi], preferred_element_type=F32)
        u = jnp.dot(hn_ref[...], w1_ref[:, D_FF + lo:D_FF + hi], preferred_element_type=F32)
        head = jnp.concatenate([carry_ref[:, lo:hi], g[0:CONV_HALO, :]], axis=0)
        g1 = jnp.concatenate([pltpu.roll(head, 1, axis=0)[CONV_HALO:, :],
                              pltpu.roll(g, 1, axis=0)[CONV_HALO:, :]], axis=0)
        g2 = jnp.concatenate([pltpu.roll(head, 2, axis=0)[CONV_HALO:, :],
                              pltpu.roll(g, 2, axis=0)[CONV_HALO:, :]], axis=0)
        carry_ref[:, lo:hi] = g[tm - CONV_HALO:, :]
        cv = (cw_ref[0:1, lo:hi] * g2 + cw_ref[1:2, lo:hi] * g1 + cw_ref[2:3, lo:hi] * g
              + cb_ref[:, lo:hi])
        hc = 0.5 * cv
        a = (hc + hc * jnp.tanh(hc)) * u
        act_ref[:, lo:hi] = a.astype(BF16)

    for r0 in range(0, tm, rows):
        x2 = out_ref[0, r0:r0 + rows, :] + jnp.dot(act_ref[r0:r0 + rows, :], w2_ref[...],
                                                   preferred_element_type=F32)
        ms2 = jnp.mean(x2 * x2, axis=-1, keepdims=True)
        out_ref[0, r0:r0 + rows, :] = x2 * lax.rsqrt(ms2 + NORM_EPS) * gl_ref[...]


def _out_ffn(x, y_pool, y_attn, w_out, g_ffn, w1, conv_w, conv_b, w2, g_final):
    b, s, d = x.shape
    tm = FFN_ROWS
    assert s % tm == 0 and tm % FFN_ROW_BLOCKS == 0 and D_FF % FFN_CHUNK == 0
    const = lambda bi, si: (0, 0)
    resident = functools.partial(pl.BlockSpec, index_map=const, pipeline_mode=pl.Buffered(1))
    return pl.pallas_call(
        _ffn_kernel,
        grid=(b, s // tm),
        in_specs=[
            pl.BlockSpec((1, tm, d), lambda bi, si: (bi, si, 0)),
            pl.BlockSpec((1, tm, POOL_WIDTH), lambda bi, si: (bi, si, 0)),
            pl.BlockSpec((1, tm, ATTN_WIDTH), lambda bi, si: (bi, si, 0)),
            resident((d, d)),
            resident((1, d)),
            resident((d, 2 * D_FF)),
            resident((3, D_FF)),
            resident((1, D_FF)),
            resident((D_FF, d)),
            resident((1, d)),
        ],
        out_specs=pl.BlockSpec((1, tm, d), lambda bi, si: (bi, si, 0)),
        out_shape=jax.ShapeDtypeStruct((b, s, d), F32),
        scratch_shapes=[pltpu.VMEM((CONV_HALO, D_FF), F32),
                        pltpu.VMEM((tm, D_FF), BF16),
                        pltpu.VMEM((tm, d), BF16)],
        compiler_params=pltpu.CompilerParams(
            dimension_semantics=("arbitrary", "arbitrary"),
            vmem_limit_bytes=VMEM_LIMIT_BYTES),
        name="out_ffn",
    )(x, y_pool, y_attn, w_out, g_ffn, w1, conv_w, conv_b, w2, g_final)


def kernel(x, norm_mix_g, w_in, pool_w, pool_scale, lambda_q1, lambda_k1, lambda_q2, lambda_k2,
           subln_g, rel_bias, w_out, norm_ffn_g, ffn_w_in, ffn_conv_w, ffn_conv_b, ffn_w_out,
           norm_final_g):
    y_pool, qt, k, vt, w_out16, w1_16, w2_16 = _in_proj(
        x, norm_mix_g, w_in[0], pool_w[0], pool_scale, (w_out[0], ffn_w_in[0], ffn_w_out[0]))
    y_attn = _diff_attn(rel_bias, qt, k, vt, (lambda_q1, lambda_k1, lambda_q2, lambda_k2), subln_g)
    return _out_ffn(x, y_pool, y_attn, w_out16, norm_ffn_g, w1_16, ffn_conv_w[0], ffn_conv_b,
                    w2_16, norm_final_g.reshape(1, D_MODEL))
```

```python
import functools
import math

import numpy as np
import jax
import jax.numpy as jnp
from jax import lax
from jax.experimental import pallas as pl
from jax.experimental.pallas import tpu as pltpu

D_MODEL = 1024
POOL_WIDTH = 512
POOL_GROUPS = 4
POOL_GROUP_DIM = 128
POOL_WINDOWS = (2, 4, 8, 16)
POOL_HALO = 16
ATTN_WIDTH = 512
N_HEADS = 4
HEAD_DIM = 64
V_DIM = 128
IN_PROJ_WIDTH = 2048
NUM_BUCKETS = 32
MAX_DISTANCE = 128
D_FF = 2816
NORM_EPS = 1e-6
SUBLN_EPS = 1e-5
NEG_INF = -1e30
LAMBDA_INIT = 0.8 - 0.6 * math.exp(-0.3 * 0)
LOG2E = math.log2(math.e)

ATTN_TILE = 256
BIAS_TILE = 128
QK_LOOKAHEAD = 2
FAR_PER_TRIP = 4
NEXT_SPAN = 2
ATTN_TILES_PER_STEP = 2
IN_ROWS = 1024
IN_ROW_BLOCKS = 2
FFN_ROWS = 1024
FFN_CHUNK = 256
FFN_ROW_BLOCKS = 4
CONV_HALO = 8
WEIGHT_CAST_ROWS = 128
N_LATER_WEIGHTS = 3
VMEM_LIMIT_BYTES = 56 * 1024 * 1024
BF16_SUBLANES = 16

F32 = jnp.float32
BF16 = jnp.bfloat16


def _causal_bucket(n):
    max_exact = NUM_BUCKETS // 2
    nf = np.maximum(n, 1).astype(np.float64)
    large = max_exact + (np.log(nf / max_exact) / math.log(MAX_DISTANCE / max_exact)
                         * (NUM_BUCKETS - max_exact)).astype(np.int64)
    return np.where(n < max_exact, n, np.minimum(large, NUM_BUCKETS - 1))


def _bucket_tiles(tile):
    k = np.arange(tile)[:, None]
    q = np.arange(tile)[None, :]
    out = []
    for offset in (0, tile):
        n = q - k + offset
        out.append(np.where(n < 0, -1, _causal_bucket(n)))
    return np.stack(out).astype(np.int32)


def _in_proj_kernel(x_ref, g_ref, w32_ref, pw_ref, ps_ref, *rest):
    later32_refs = rest[:N_LATER_WEIGHTS]
    ypool_ref, qt_ref, k_ref, vt_ref = rest[N_LATER_WEIGHTS:N_LATER_WEIGHTS + 4]
    later16_refs = rest[N_LATER_WEIGHTS + 4:2 * N_LATER_WEIGHTS + 4]
    ext_ref, w_ref, hn_ref = rest[2 * N_LATER_WEIGHTS + 4:]
    si = pl.program_id(1)
    tm = x_ref.shape[1]

    @pl.when((pl.program_id(0) == 0) & (si == 0))
    def _():
        for r in range(0, D_MODEL, WEIGHT_CAST_ROWS):
            w_ref[r:r + WEIGHT_CAST_ROWS, :] = w32_ref[r:r + WEIGHT_CAST_ROWS, :].astype(BF16)

    @pl.when(si == 0)
    def _():
        ext_ref[0:POOL_HALO, :] = jnp.zeros((POOL_HALO, POOL_WIDTH), F32)

    rows = tm // IN_ROW_BLOCKS
    zp_blocks = []
    for r0 in range(0, tm, rows):
        x = x_ref[0, r0:r0 + rows, :]
        ms = jnp.mean(x * x, axis=-1, keepdims=True)
        hn = (x * lax.rsqrt(ms + NORM_EPS) * g_ref[...]).astype(BF16)
        hn_ref[r0:r0 + rows, :] = hn
        zp_blocks.append(jnp.dot(hn, w_ref[:, :POOL_WIDTH], preferred_element_type=F32))
    zp = jnp.concatenate(zp_blocks, axis=0)
    z = jnp.dot(hn_ref[...], w_ref[:, POOL_WIDTH:], preferred_element_type=F32)

    for src_ref, dst_ref in zip(later32_refs, later16_refs):
        dst_ref[...] = src_ref[...].astype(BF16)

    ext_ref[POOL_HALO:, :] = zp
    tpos = si * tm + lax.broadcasted_iota(jnp.int32, (tm, 1), 0) + 1
    for gi, w in enumerate(POOL_WINDOWS):
        lo, hi = gi * POOL_GROUP_DIM, (gi + 1) * POOL_GROUP_DIM
        tot = ext_ref[:, lo:hi]
        step = 1
        while step < w:
            tot = tot + pltpu.roll(tot, step, axis=0)
            step *= 2
        tot = tot[POOL_HALO:, :]
        inv_cnt = 1.0 / jnp.minimum(tpos, w).astype(F32)
        pooled = tot * inv_cnt - zp[:, lo:hi]
        yg = jnp.dot(pooled.astype(BF16), pw_ref[gi].astype(BF16), preferred_element_type=F32)
        ypool_ref[0, :, lo:hi] = (yg * ps_ref[:, lo:hi]).astype(BF16)
    ext_ref[0:POOL_HALO, :] = zp[tm - POOL_HALO:, :]

    t = ATTN_TILE
    q0, k0, v0 = 0, ATTN_WIDTH, 2 * ATTN_WIDTH
    scale = LOG2E * HEAD_DIM ** -0.5
    for h in range(N_HEADS):
        zq = z[:, q0 + h * V_DIM:q0 + (h + 1) * V_DIM] * scale
        qt_ref[0, h * V_DIM:(h + 1) * V_DIM, :] = zq.T.astype(BF16)
        k_ref[0, h] = z[:, k0 + h * V_DIM:k0 + (h + 1) * V_DIM].astype(BF16)
        vt = z[:, v0 + h * V_DIM:v0 + (h + 1) * V_DIM].T.astype(BF16)
        for j in range(tm // t):
            vt_ref[0, h, j] = vt[:, j * t:(j + 1) * t]


def _in_proj(x, g, w_in, pool_w, pool_scale, later_weights):
    b, s, d = x.shape
    tm, t = IN_ROWS, ATTN_TILE
    steps = s // tm
    assert s % tm == 0 and tm % t == 0 and tm % IN_ROW_BLOCKS == 0
    assert len(later_weights) == N_LATER_WEIGHTS
    assert all(w.shape[0] % (b * steps * BF16_SUBLANES) == 0 for w in later_weights)
    slabs = [(w.shape[0] // (b * steps), w.shape[1]) for w in later_weights]
    slab_specs = [pl.BlockSpec(sl, lambda bi, si: (bi * steps + si, 0)) for sl in slabs]
    return pl.pallas_call(
        _in_proj_kernel,
        grid=(b, s // tm),
        in_specs=[
            pl.BlockSpec((1, tm, d), lambda bi, si: (bi, si, 0)),
            pl.BlockSpec((1, d), lambda bi, si: (0, 0)),
            pl.BlockSpec((d, IN_PROJ_WIDTH), lambda bi, si: (0, 0), pipeline_mode=pl.Buffered(1)),
            pl.BlockSpec((POOL_GROUPS, POOL_GROUP_DIM, POOL_GROUP_DIM), lambda bi, si: (0, 0, 0)),
            pl.BlockSpec((1, POOL_WIDTH), lambda bi, si: (0, 0)),
        ] + slab_specs,
        out_specs=[
            pl.BlockSpec((1, tm, POOL_WIDTH), lambda bi, si: (bi, si, 0)),
            pl.BlockSpec((1, ATTN_WIDTH, tm), lambda bi, si: (bi, 0, si)),
            pl.BlockSpec((1, N_HEADS, tm, V_DIM), lambda bi, si: (bi, 0, si, 0)),
            pl.BlockSpec((1, N_HEADS, tm // t, V_DIM, t), lambda bi, si: (bi, 0, si, 0, 0)),
        ] + slab_specs,
        out_shape=[
            jax.ShapeDtypeStruct((b, s, POOL_WIDTH), BF16),
            jax.ShapeDtypeStruct((b, ATTN_WIDTH, s), BF16),
            jax.ShapeDtypeStruct((b, N_HEADS, s, V_DIM), BF16),
            jax.ShapeDtypeStruct((b, N_HEADS, s // t, V_DIM, t), BF16),
        ] + [jax.ShapeDtypeStruct(w.shape, BF16) for w in later_weights],
        scratch_shapes=[pltpu.VMEM((POOL_HALO + tm, POOL_WIDTH), F32),
                        pltpu.VMEM((d, IN_PROJ_WIDTH), BF16),
                        pltpu.VMEM((tm, d), BF16)],
        compiler_params=pltpu.CompilerParams(
            dimension_semantics=("arbitrary", "arbitrary"),
            vmem_limit_bytes=VMEM_LIMIT_BYTES),
        name="in_proj",
    )(x, g, w_in, pool_w, pool_scale, *later_weights)


def _attn_kernel(rb_ref, qt_ref, qtn_ref, k_ref, vt_ref, bkt_ref, lq1_ref, lk1_ref, lq2_ref,
                 lk2_ref, sg_ref, out_ref, *scratch):
    t = ATTN_TILE
    m_refs, l_refs, smax_refs, acc_refs, qs_refs, s_refs = (
        scratch[i * N_HEADS:(i + 1) * N_HEADS] for i in range(6))
    qsn_refs = scratch[6 * N_HEADS:6 * N_HEADS + QK_LOOKAHEAD]
    bias_ref = scratch[6 * N_HEADS + QK_LOOKAHEAD]

    @pl.when((pl.program_id(0) == 0) & (pl.program_id(1) == 0))
    def _():
        bkt = bkt_ref[...]
        acc = [jnp.full(bkt.shape, NEG_INF, F32) for _ in range(N_HEADS)]
        for i in range(NUM_BUCKETS):
            hit = bkt == i
            acc = [jnp.where(hit, rb_ref[i, h] * LOG2E, acc[h]) for h in range(N_HEADS)]
        for h in range(N_HEADS):
            bias_ref[h] = acc[h]

    def load_q(h, src, dst_refs):
        src_ref, col0 = src
        qt = src_ref[0, h * V_DIM:(h + 1) * V_DIM, col0:col0 + t]
        row = lax.broadcasted_iota(jnp.int32, qt.shape, 0)
        zero = jnp.zeros_like(qt)
        dst_refs[h][:, 0:t] = jnp.where(row < HEAD_DIM, qt, zero)
        dst_refs[h][:, t:2 * t] = jnp.where(row >= HEAD_DIM, qt, zero)

    far_bias = [rb_ref[NUM_BUCKETS - 1, h] * LOG2E for h in range(N_HEADS)]

    lam = (jnp.exp(jnp.sum(lq1_ref[...] * lk1_ref[...], axis=1, keepdims=True))
           - jnp.exp(jnp.sum(lq2_ref[...] * lk2_ref[...], axis=1, keepdims=True))
           + LAMBDA_INIT)

    def qk(h, j, kinds, add_bias=True, q_refs=qs_refs):
        n = len(kinds) * t
        kj = k_ref[0, h, pl.ds(pl.multiple_of(j * t, t), n), :]
        s = jnp.dot(kj, q_refs[h][...], preferred_element_type=F32)
        if add_bias and has_bias(kinds):
            s = with_bias(s, h, kinds)
        s_refs[h][0:n, :] = s
        smax_refs[h][...] = jnp.max(s, axis=0, keepdims=True)

    def has_bias(kinds):
        return any(kd is not None for kd in kinds)

    def with_bias(s, h, kinds):
        b = jnp.concatenate([tile_bias(h, kd) for kd in kinds], axis=0)
        return s + jnp.concatenate([b, b], axis=1)

    def tile_bias(h, diag):
        n = t // BIAS_TILE
        pieces = {0: bias_ref[h, 0], 1: bias_ref[h, 1]}
        far = jnp.full((BIAS_TILE, BIAS_TILE), far_bias[h], F32)
        if diag is None:
            return jnp.full((t, t), far_bias[h], F32)
        masked = jnp.full((BIAS_TILE, BIAS_TILE), NEG_INF, F32)
        rows = []
        for kb in range(n):
            d = [qb - kb + (0 if diag else n) for qb in range(n)]
            rows.append(jnp.concatenate([masked if x < 0 else pieces.get(x, far) for x in d], axis=1))
        return jnp.concatenate(rows, axis=0)

    def softmax_pv(h, j, kinds, bias_pending):
        n = len(kinds) * t
        off = 0.0 if has_bias(kinds) else far_bias[h]
        s = s_refs[h][0:n, :]
        if bias_pending and has_bias(kinds):
            s = with_bias(s, h, kinds)
            smax = jnp.max(s, axis=0, keepdims=True)
        else:
            smax = smax_refs[h][...]
        m_old = m_refs[h][...]
        m_new = jnp.maximum(m_old, smax + off)
        alpha = jnp.exp2(m_old - m_new)
        p = jnp.exp2(s - (m_new - off))
        l_refs[h][...] = alpha * l_refs[h][...] + jnp.sum(p, axis=0, keepdims=True)
        m_refs[h][...] = m_new
        vt = jnp.concatenate([vt_ref[0, h, j + d] for d in range(len(kinds))], axis=1)
        r = jnp.dot(vt, p.astype(BF16), preferred_element_type=F32)
        acc_refs[h][...] = alpha * acc_refs[h][...] + r

    def run(groups, next_j, sub):
        stages = [(h, j, kinds) for (j, kinds) in groups for h in range(N_HEADS)]
        for i, (h, j, kinds) in enumerate(stages):
            ahead = i + QK_LOOKAHEAD
            if ahead < len(stages):
                qk(*stages[ahead])
            elif next_j is not None:
                qk(ahead - len(stages), next_j, (None,) * NEXT_SPAN, add_bias=False)
            else:
                nxt = (qt_ref, (sub + 1) * t) if sub + 1 < ATTN_TILES_PER_STEP else (qtn_ref, 0)
                load_q(ahead - len(stages), nxt, qsn_refs)
                qk(ahead - len(stages), 0, (None,) * NEXT_SPAN, add_bias=False, q_refs=qsn_refs)
            softmax_pv(h, j, kinds, bias_pending=i < QK_LOOKAHEAD)
            if kinds[-1] is True:
                finalize(h, sub)

    def finalize(h, sub):
        o = acc_refs[h][...] * (1.0 / l_refs[h][...])
        o = o[:, :t] - lam * o[:, t:]
        ms = jnp.mean(o * o, axis=0, keepdims=True)
        y = (o * lax.rsqrt(ms + SUBLN_EPS)).T * (sg_ref[...] * (1.0 - LAMBDA_INIT))
        out_ref[0, sub * t:(sub + 1) * t, h * V_DIM:(h + 1) * V_DIM] = y.astype(BF16)

    def query_tile(sub):
        qi = pl.program_id(1) * ATTN_TILES_PER_STEP + sub

        if sub == 0:
            @pl.when(qi == 0)
            def _():
                for h in range(QK_LOOKAHEAD):
                    load_q(h, (qt_ref, 0), qsn_refs)
                    qk(h, 0, (True,), add_bias=False, q_refs=qsn_refs)

        for h in range(QK_LOOKAHEAD):
            qs_refs[h][...] = qsn_refs[h][...]
        for h in range(QK_LOOKAHEAD, N_HEADS):
            load_q(h, (qt_ref, sub * t), qs_refs)
        for h in range(N_HEADS):
            m_refs[h][...] = jnp.full(m_refs[h].shape, -jnp.inf, F32)
            l_refs[h][...] = jnp.zeros(l_refs[h].shape, F32)
            acc_refs[h][...] = jnp.zeros(acc_refs[h].shape, F32)

        n_far = qi - 1

        def far_body(i, carry):
            j = FAR_PER_TRIP * i
            run([(j + d, (None, None)) for d in range(0, FAR_PER_TRIP, 2)], j + FAR_PER_TRIP, sub)
            return carry

        lax.fori_loop(0, jnp.maximum(n_far, 0) // FAR_PER_TRIP, far_body, 0)

        if sub == 0:
            @pl.when(qi == 0)
            def _():
                run([(qi, (True,))], None, sub)

        nq = k_ref.shape[2] // t
        for rem in range(FAR_PER_TRIP):
            if not any((q - 1) % FAR_PER_TRIP == rem
                       for q in range(1, nq) if q % ATTN_TILES_PER_STEP == sub):
                continue

            @pl.when((qi >= 1) & (n_far % FAR_PER_TRIP == rem))
            def _():
                kinds = (None,) * rem + (False, True)
                first = qi - 1 - rem
                run([(first + d, kinds[d:d + 2]) for d in range(0, len(kinds), 2)], None, sub)

    for sub in range(ATTN_TILES_PER_STEP):
        query_tile(sub)


def _diff_attn(rel_bias, qt, k, vt, lambdas, subln_g):
    b, _, s = qt.shape
    t, tps = ATTN_TILE, ATTN_TILES_PER_STEP
    nq = s // t
    assert s % (t * tps) == 0 and t % BIAS_TILE == 0
    assert FAR_PER_TRIP % NEXT_SPAN == 0 and QK_LOOKAHEAD < N_HEADS
    assert (_causal_bucket(np.arange(BIAS_TILE + 1, s)) == NUM_BUCKETS - 1).all()
    per_head = lambda shape, dtype: [pltpu.VMEM(shape, dtype) for _ in range(N_HEADS)]
    return pl.pallas_call(
        _attn_kernel,
        grid=(b, nq // tps),
        in_specs=[
            pl.BlockSpec(memory_space=pltpu.SMEM),
            pl.BlockSpec((1, ATTN_WIDTH, tps * t), lambda bi, qi: (bi, 0, qi)),
            pl.BlockSpec((1, QK_LOOKAHEAD * V_DIM, t),
                         lambda bi, qi: (bi, 0, jnp.minimum((qi + 1) * tps, nq - 1))),
            pl.BlockSpec((1, N_HEADS, s, V_DIM), lambda bi, qi: (bi, 0, 0, 0)),
            pl.BlockSpec((1, N_HEADS, nq, V_DIM, t), lambda bi, qi: (bi, 0, 0, 0, 0)),
            pl.BlockSpec((2, BIAS_TILE, BIAS_TILE), lambda bi, qi: (0, 0, 0)),
        ] + [pl.BlockSpec((1, HEAD_DIM), lambda bi, qi: (0, 0)) for _ in lambdas] + [
            pl.BlockSpec((1, V_DIM), lambda bi, qi: (0, 0)),
        ],
        out_specs=pl.BlockSpec((1, tps * t, ATTN_WIDTH), lambda bi, qi: (bi, qi, 0)),
        out_shape=jax.ShapeDtypeStruct((b, s, ATTN_WIDTH), BF16),
        scratch_shapes=(per_head((1, 2 * t), F32) + per_head((1, 2 * t), F32)
                        + per_head((1, 2 * t), F32)
                        + per_head((V_DIM, 2 * t), F32) + per_head((V_DIM, 2 * t), BF16)
                        + per_head((NEXT_SPAN * t, 2 * t), F32)
                        + [pltpu.VMEM((V_DIM, 2 * t), BF16) for _ in range(QK_LOOKAHEAD)]
                        + [pltpu.VMEM((N_HEADS, 2, BIAS_TILE, BIAS_TILE), F32)]),
        compiler_params=pltpu.CompilerParams(
            dimension_semantics=("arbitrary", "arbitrary"),
            vmem_limit_bytes=VMEM_LIMIT_BYTES),
        name="diff_attn",
    )(rel_bias, qt, qt, k, vt, jnp.asarray(_bucket_tiles(BIAS_TILE)), *lambdas, subln_g)


def _ffn_kernel(x_ref, yp_ref, ya_ref, wo_ref, gf_ref, w1_ref, cw_ref, cb_ref,
                w2_ref, gl_ref, out_ref, carry_ref, gbuf_ref, act_ref, hn_ref):
    si = pl.program_id(1)
    tm = x_ref.shape[1]
    ck = FFN_CHUNK
    rows = tm // FFN_ROW_BLOCKS

    @pl.when(si == 0)
    def _():
        carry_ref[...] = jnp.zeros(carry_ref.shape, F32)

    for r0 in range(0, tm, rows):
        y = (jnp.dot(yp_ref[0, r0:r0 + rows, :], wo_ref[0:POOL_WIDTH, :], preferred_element_type=F32)
             + jnp.dot(ya_ref[0, r0:r0 + rows, :], wo_ref[POOL_WIDTH:, :], preferred_element_type=F32))
        x1 = x_ref[0, r0:r0 + rows, :] + y
        out_ref[0, r0:r0 + rows, :] = x1
        ms = jnp.mean(x1 * x1, axis=-1, keepdims=True)
        hn_ref[r0:r0 + rows, :] = (x1 * lax.rsqrt(ms + NORM_EPS) * gf_ref[...]).astype(BF16)

    for c in range(D_FF // ck):
        lo, hi = c * ck, (c + 1) * ck
        g = jnp.dot(hn_ref[...], w1_ref[:, lo:hi], preferred_element_type=F32)
        u = jnp.dot(hn_ref[...], w1_ref[:, D_FF + lo:D_FF + hi], preferred_element_type=F32)
        gbuf_ref[0:CONV_HALO, :] = carry_ref[:, lo:hi]
        gbuf_ref[CONV_HALO:, :] = g
        g1 = gbuf_ref[pl.ds(CONV_HALO - 1, tm), :]
        g2 = gbuf_ref[pl.ds(CONV_HALO - 2, tm), :]
        carry_ref[:, lo:hi] = g[tm - CONV_HALO:, :]
        cv = (cw_ref[0:1, lo:hi] * g2 + cw_ref[1:2, lo:hi] * g1 + cw_ref[2:3, lo:hi] * g
              + cb_ref[:, lo:hi])
        hc = 0.5 * cv
        a = (hc + hc * jnp.tanh(hc)) * u
        act_ref[:, lo:hi] = a.astype(BF16)

    for r0 in range(0, tm, rows):
        x2 = out_ref[0, r0:r0 + rows, :] + jnp.dot(act_ref[r0:r0 + rows, :], w2_ref[...],
                                                   preferred_element_type=F32)
        ms2 = jnp.mean(x2 * x2, axis=-1, keepdims=True)
        out_ref[0, r0:r0 + rows, :] = x2 * lax.rsqrt(ms2 + NORM_EPS) * gl_ref[...]


def _out_ffn(x, y_pool, y_attn, w_out, g_ffn, w1, conv_w, conv_b, w2, g_final):
    b, s, d = x.shape
    tm = FFN_ROWS
    assert s % tm == 0 and tm % FFN_ROW_BLOCKS == 0 and D_FF % FFN_CHUNK == 0
    const = lambda bi, si: (0, 0)
    resident = functools.partial(pl.BlockSpec, index_map=const, pipeline_mode=pl.Buffered(1))
    return pl.pallas_call(
        _ffn_kernel,
        grid=(b, s // tm),
        in_specs=[
            pl.BlockSpec((1, tm, d), lambda bi, si: (bi, si, 0)),
            pl.BlockSpec((1, tm, POOL_WIDTH), lambda bi, si: (bi, si, 0)),
            pl.BlockSpec((1, tm, ATTN_WIDTH), lambda bi, si: (bi, si, 0)),
            resident((d, d)),
            resident((1, d)),
            resident((d, 2 * D_FF)),
            resident((3, D_FF)),
            resident((1, D_FF)),
            resident((D_FF, d)),
            resident((1, d)),
        ],
        out_specs=pl.BlockSpec((1, tm, d), lambda bi, si: (bi, si, 0)),
        out_shape=jax.ShapeDtypeStruct((b, s, d), F32),
        scratch_shapes=[pltpu.VMEM((CONV_HALO, D_FF), F32),
                        pltpu.VMEM((CONV_HALO + tm, FFN_CHUNK), F32),
                        pltpu.VMEM((tm, D_FF), BF16),
                        pltpu.VMEM((tm, d), BF16)],
        compiler_params=pltpu.CompilerParams(
            dimension_semantics=("arbitrary", "arbitrary"),
            vmem_limit_bytes=VMEM_LIMIT_BYTES),
        name="out_ffn",
    )(x, y_pool, y_attn, w_out, g_ffn, w1, conv_w, conv_b, w2, g_final)


def kernel(x, norm_mix_g, w_in, pool_w, pool_scale, lambda_q1, lambda_k1, lambda_q2, lambda_k2,
           subln_g, rel_bias, w_out, norm_ffn_g, ffn_w_in, ffn_conv_w, ffn_conv_b, ffn_w_out,
           norm_final_g):
    y_pool, qt, k, vt, w_out16, w1_16, w2_16 = _in_proj(
        x, norm_mix_g, w_in[0], pool_w[0], pool_scale, (w_out[0], ffn_w_in[0], ffn_w_out[0]))
    y_attn = _diff_attn(rel_bias, qt, k, vt, (lambda_q1, lambda_k1, lambda_q2, lambda_k2), subln_g)
    return _out_ffn(x, y_pool, y_attn, w_out16, norm_ffn_g, w1_16, ffn_conv_w[0], ffn_conv_b,
                    w2_16, norm_final_g.reshape(1, D_MODEL))
```
